```python
import jax, jax.numpy as jnp
from jax import lax
import numpy as np

D_MODEL = 1024
BATCH = 16
SEQ = 2048
DEPTH = 1

N_MEM = 256
HEAD_DIM = 64
NORM_EPS = 1e-6
ROPE_THETA = 10000.0
RW_HEADS = 8
RW_WIDTH = RW_HEADS * HEAD_DIM
DECAY_LORA = 64
AAA_LORA = 64
GATE_LORA = 128
RW_GN_EPS = 64e-5
RW_IN = 3 * RW_WIDTH + DECAY_LORA + AAA_LORA + GATE_LORA
NSA_HEADS = 8
NSA_KV_GROUPS = 2
NSA_HPG = NSA_HEADS // NSA_KV_GROUPS
NSA_WIDTH = NSA_HEADS * HEAD_DIM
KV_WIDTH = NSA_KV_GROUPS * HEAD_DIM
CMP_BLOCK = 32
CMP_STRIDE = 16
CMP_HIDDEN = 128
SEL_BLOCK = 64
SEL_TOPK = 8
WINDOW = 512
Q_CHUNK = 32
XA_HEADS = 4
XA_HEAD_DIM = D_MODEL // XA_HEADS
D_FF = -(-8 * D_MODEL // (3 * 256)) * 256
IN_SIZES = [RW_IN, NSA_WIDTH] + [KV_WIDTH] * 6 + [3 * NSA_HEADS, 2 * D_MODEL]
D_IN = RW_IN + NSA_WIDTH + 6 * KV_WIDTH + 3 * NSA_HEADS + 2 * D_MODEL

kernel_name = 'hybrid_rwkv7_nsa_block'


def split_cols(p, sizes):
    return jnp.split(p, np.cumsum(sizes)[:-1].tolist(), axis=-1)


def rms_norm(x, g):
    xf = x.astype(jnp.float32)
    y = xf * lax.rsqrt(jnp.mean(xf * xf, axis=-1, keepdims=True) + NORM_EPS)
    return (y * g.astype(jnp.float32)).astype(x.dtype)


def rope(x, pos):
    half = x.shape[-1] // 2
    inv_freq = ROPE_THETA ** (-jnp.arange(half, dtype=jnp.float32) / half)
    ang = pos.astype(jnp.float32)[:, None] * inv_freq[None, :]
    cos, sin = jnp.cos(ang)[:, None, :], jnp.sin(ang)[:, None, :]
    xf = x.astype(jnp.float32)
    x1, x2 = xf[..., :half], xf[..., half:]
    return jnp.concatenate([x1 * cos - x2 * sin, x2 * cos + x1 * sin], axis=-1).astype(x.dtype)


def masked_softmax(s, mask):
    s = jnp.where(mask, s.astype(jnp.float32), -1e30)
    e = jnp.where(mask, jnp.exp(s - jnp.max(s, axis=-1, keepdims=True)), 0.0)
    return e / jnp.maximum(jnp.sum(e, axis=-1, keepdims=True), 1e-30)


def token_shift(p, mu):
    prev = jnp.pad(p[:, :-1], ((0, 0), (1, 0), (0, 0)))
    return p + mu * (prev - p)


def rwkv7_time_mix(p_rw, w_up, w0, a_up, a0, g_up, k_k, k_a, r_k, ln_g, ln_b):
    B, T, _ = p_rw.shape
    f32 = jnp.float32
    r, k, v, xw, xa, xg = split_cols(p_rw, [RW_WIDTH] * 3 + [DECAY_LORA, AAA_LORA, GATE_LORA])
    w_log = -jax.nn.softplus(-(w0 + jnp.tanh(xw) @ w_up).astype(f32)) - 0.5
    decay = jnp.exp(-jnp.exp(w_log))
    a = jax.nn.sigmoid((a0 + xa @ a_up).astype(f32))
    g = jax.nn.sigmoid(xg) @ g_up

    def heads(t):
        return t.astype(f32).reshape(B, T, RW_HEADS, HEAD_DIM)

    r, k, v, decay, a = heads(r), heads(k), heads(v), heads(decay), heads(a)
    kk = k * k_k.astype(f32).reshape(RW_HEADS, HEAD_DIM)
    kk = kk * lax.rsqrt(jnp.maximum(jnp.sum(kk * kk, axis=-1, keepdims=True), 1e-12))
    k = k * (1.0 + (a - 1.0) * k_a.astype(f32).reshape(RW_HEADS, HEAD_DIM))

    def step(S, inp):
        r_t, w_t, k_t, v_t, kk_t, a_t = inp
        sa = jnp.einsum('bhij,bhj->bhi', S, -kk_t)
        S = S * w_t[:, :, None, :] + sa[..., :, None] * (kk_t * a_t)[..., None, :] \
            + v_t[..., :, None] * k_t[..., None, :]
        return S, jnp.einsum('bhij,bhj->bhi', S, r_t)

    xs = tuple(jnp.moveaxis(t, 1, 0) for t in (r, decay, k, v, kk, a))
    S0 = jnp.zeros((B, RW_HEADS, HEAD_DIM, HEAD_DIM), f32)
    _, ys = lax.scan(step, S0, xs)
    y = jnp.moveaxis(ys, 0, 1)
    mu = jnp.mean(y, axis=-1, keepdims=True)
    var = jnp.mean(jnp.square(y - mu), axis=-1, keepdims=True)
    y = ((y - mu) * lax.rsqrt(var + RW_GN_EPS)).reshape(B, T, RW_WIDTH)
    y = y * ln_g.astype(f32) + ln_b.astype(f32)
    bonus = (jnp.sum(r * k * r_k.astype(f32), axis=-1, keepdims=True) * v).reshape(B, T, RW_WIDTH)
    return ((y + bonus) * g.astype(f32)).astype(p_rw.dtype)


def nsa_attention(q, kc, vc, ks, vs, kw, vw, gate_logits, pe_k, pe_v, ck1, ck2, cv1, cv2):
    B, T, _ = q.shape
    G, HPG, D = NSA_KV_GROUPS, NSA_HPG, HEAD_DIM
    scale = D ** -0.5
    pos = jnp.arange(T)
    q = q.reshape(B, T, NSA_HEADS, D)
    q_rot = rope(q, pos).reshape(B, T, G, HPG, D)
    q_cmp = q.reshape(B, T, G, HPG, D)
    kc, vc, ks, vs, kw, vw = (t.reshape(B, T, G, D) for t in (kc, vc, ks, vs, kw, vw))
    ks, kw = rope(ks, pos), rope(kw, pos)

    n_cmp = (T - CMP_BLOCK) // CMP_STRIDE + 1
    cmp_start = np.arange(n_cmp) * CMP_STRIDE
    cmp_idx = cmp_start[:, None] + np.arange(CMP_BLOCK)[None, :]
    cmp_last = jnp.asarray(cmp_start + CMP_BLOCK - 1)

    def compress(t, pe, w1, w2):
        blk = t[:, cmp_idx] + pe[:, None, :]
        blk = jnp.transpose(blk, (0, 1, 3, 2, 4)).reshape(B, n_cmp, G, CMP_BLOCK * D)
        return jax.nn.gelu(blk @ w1) @ w2

    k_cmp = compress(kc, pe_k, ck1, ck2)
    v_cmp = compress(vc, pe_v, cv1, cv2)

    n_sel = T // SEL_BLOCK
    top_k = min(SEL_TOPK, n_sel)
    sel_start = np.arange(n_sel) * SEL_BLOCK
    overlap = np.clip(np.minimum(cmp_start[:, None] + CMP_BLOCK, sel_start[None, :] + SEL_BLOCK)
                      - np.maximum(cmp_start[:, None], sel_start[None, :]), 0, None)
    overlap = jnp.asarray(overlap / CMP_BLOCK, dtype=jnp.float32)
    ks_blk = jnp.transpose(ks.reshape(B, n_sel, SEL_BLOCK, G, D), (0, 3, 1, 2, 4))
    vs_blk = jnp.transpose(vs.reshape(B, n_sel, SEL_BLOCK, G, D), (0, 3, 1, 2, 4))
    kw_pad = jnp.pad(kw, ((0, 0), (WINDOW, 0), (0, 0), (0, 0)))
    vw_pad = jnp.pad(vw, ((0, 0), (WINDOW, 0), (0, 0), (0, 0)))
    b_ix = jnp.arange(B)[:, None, None, None]
    g_ix = jnp.arange(G)[None, :, None, None]
    blk_ids = jnp.arange(n_sel)
    span = Q_CHUNK + WINDOW

    def chunk(c):
        t0 = c * Q_CHUNK
        tq = t0 + jnp.arange(Q_CHUNK)
        qc = lax.dynamic_slice_in_dim(q_cmp, t0, Q_CHUNK, axis=1)
        qr = lax.dynamic_slice_in_dim(q_rot, t0, Q_CHUNK, axis=1)
        s = jnp.einsum('bcghd,bngd->bghcn', qc, k_cmp) * scale
        p_cmp = masked_softmax(s, cmp_last[None, :] <= tq[:, None])
        o_cmp = jnp.einsum('bghcn,bngd->bcghd', p_cmp.astype(v_cmp.dtype), v_cmp)
        imp = jnp.einsum('bghcn,nj->bgcj', p_cmp, overlap)
        cur = tq[:, None] // SEL_BLOCK
        forced = (blk_ids[None] == 0) | (blk_ids[None] == cur) | (blk_ids[None] == cur - 1)
        imp = jnp.where(forced, 1e4, jnp.where(blk_ids[None] <= cur, imp, -1.0))
        _, sel = lax.top_k(imp, top_k)
        kb = ks_blk[b_ix, g_ix, sel]
        vb = vs_blk[b_ix, g_ix, sel]
        s = jnp.einsum('bcghd,bgcksd->bghcks', qr, kb) * scale
        key_pos = sel[..., None] * SEL_BLOCK + jnp.arange(SEL_BLOCK)
        mask = (key_pos <= tq[None, None, :, None, None]).reshape(B, G, 1, Q_CHUNK, top_k * SEL_BLOCK)
        p = masked_softmax(s.reshape(B, G, HPG, Q_CHUNK, top_k * SEL_BLOCK), mask)
        o_sel = jnp.einsum('bghcks,bgcksd->bcghd', p.reshape(s.shape).astype(vb.dtype), vb)
        kwc = lax.dynamic_slice_in_dim(kw_pad, t0, span, axis=1)
        vwc = lax.dynamic_slice_in_dim(vw_pad, t0, span, axis=1)
        kpos = t0 - WINDOW + jnp.arange(span)
        wmask = (kpos[None] <= tq[:, None]) & (kpos[None] > tq[:, None] - WINDOW) & (kpos[None] >= 0)
        s = jnp.einsum('bcghd,bsgd->bghcs', qr, kwc) * scale
        p = masked_softmax(s, wmask)
        o_win = jnp.einsum('bghcs,bsgd->bcghd', p.astype(vwc.dtype), vwc)
        return o_cmp, o_sel, o_win

    outs = lax.map(chunk, jnp.arange(T // Q_CHUNK))
    o_cmp, o_sel, o_win = (jnp.moveaxis(o, 0, 1).reshape(B, T, NSA_HEADS, D) for o in outs)
    gates = jax.nn.sigmoid(gate_logits.astype(jnp.float32)).reshape(B, T, NSA_HEADS, 3)
    o = gates[..., 0:1] * o_cmp + gates[..., 1:2] * o_sel + gates[..., 2:3] * o_win
    return o.reshape(B, T, NSA_WIDTH).astype(q.dtype)


def hybrid_mixer(h_n, w_in, shift_mu, rw_w_up, rw_w0, rw_a_up, rw_a0, rw_g_up, rw_k_k, rw_k_a,
                 rw_r_k, rw_ln_g, rw_ln_b, nsa_pe_k, nsa_pe_v, nsa_ck1, nsa_ck2, nsa_cv1, nsa_cv2,
                 w_up_rw, w_up_nsa, w_out):
    p = h_n @ w_in
    p_rw, q, kc, vc, ks, vs, kw, vw, g_nsa, g_merge = split_cols(p, IN_SIZES)
    p_rw = token_shift(p_rw, shift_mu)
    y_rw = rwkv7_time_mix(p_rw, rw_w_up, rw_w0, rw_a_up, rw_a0, rw_g_up, rw_k_k, rw_k_a,
                          rw_r_k, rw_ln_g, rw_ln_b) @ w_up_rw
    y_nsa = nsa_attention(q, kc, vc, ks, vs, kw, vw, g_nsa, nsa_pe_k, nsa_pe_v,
                          nsa_ck1, nsa_ck2, nsa_cv1, nsa_cv2) @ w_up_nsa
    g_rw, g_ns = jnp.split(jax.nn.sigmoid(g_merge), 2, axis=-1)
    return (g_rw * y_rw + g_ns * y_nsa) @ w_out


def memory_cross_attention(h_n, mem_n, wq, wkv, wo):
    B, T, _ = h_n.shape
    M = mem_n.shape[1]
    q = (h_n @ wq).reshape(B, T, XA_HEADS, XA_HEAD_DIM)
    k, v = jnp.split(mem_n @ wkv, 2, axis=-1)
    k = k.reshape(B, M, XA_HEADS, XA_HEAD_DIM)
    v = v.reshape(B, M, XA_HEADS, XA_HEAD_DIM)
    s = jnp.einsum('bthd,bmhd->bhtm', q, k).astype(jnp.float32) * (XA_HEAD_DIM ** -0.5)
    p = jax.nn.softmax(s, axis=-1)
    o = jnp.einsum('bhtm,bmhd->bthd', p.astype(v.dtype), v).reshape(B, T, D_MODEL)
    return o @ wo


def swiglu_ffn(h_n, w_gu, w_down):
    g, u = jnp.split(h_n @ w_gu, 2, axis=-1)
    return (jax.nn.silu(g) * u) @ w_down


def setup_inputs(seed: int = 0) -> dict:
    key = jax.random.key(seed)
    keys = iter(jax.random.split(key, 40))
    L = (DEPTH,)

    def nrm(shape, scale):
        return jax.random.normal(next(keys), shape, jnp.float32) * scale

    def gain(n):
        return 1.0 + nrm(L + (n,), 0.02)

    inputs = {}
    inputs['x'] = nrm((BATCH, SEQ, D_MODEL), 1.0)
    inputs['mem'] = nrm((BATCH, N_MEM, D_MODEL), 1.0)
    inputs['norm_mix_g'] = gain(D_MODEL)
    inputs['w_in'] = nrm(L + (D_MODEL, D_IN), D_MODEL ** -0.5)
    inputs['shift_mu'] = jax.random.uniform(next(keys), L + (RW_IN,), jnp.float32)
    inputs['rw_w_up'] = nrm(L + (DECAY_LORA, RW_WIDTH), 0.5 * DECAY_LORA ** -0.5)
    inputs['rw_w0'] = -2.0 + nrm(L + (RW_WIDTH,), 0.5)
    inputs['rw_a_up'] = nrm(L + (AAA_LORA, RW_WIDTH), AAA_LORA ** -0.5)
    inputs['rw_a0'] = nrm(L + (RW_WIDTH,), 0.5)
    inputs['rw_g_up'] = nrm(L + (GATE_LORA, RW_WIDTH), GATE_LORA ** -0.5)
    inputs['rw_k_k'] = 0.85 + nrm(L + (RW_WIDTH,), 0.05)
    inputs['rw_k_a'] = 1.0 + nrm(L + (RW_WIDTH,), 0.05)
    inputs['rw_r_k'] = nrm(L + (RW_HEADS, HEAD_DIM), 0.1)
    inputs['rw_ln_g'] = gain(RW_WIDTH)
    inputs['rw_ln_b'] = nrm(L + (RW_WIDTH,), 0.02)
    inputs['nsa_pe_k'] = nrm(L + (CMP_BLOCK, HEAD_DIM), 0.02)
    inputs['nsa_pe_v'] = nrm(L + (CMP_BLOCK, HEAD_DIM), 0.02)
    inputs['nsa_ck1'] = nrm(L + (CMP_BLOCK * HEAD_DIM, CMP_HIDDEN), (CMP_BLOCK * HEAD_DIM) ** -0.5)
    inputs['nsa_ck2'] = nrm(L + (CMP_HIDDEN, HEAD_DIM), CMP_HIDDEN ** -0.5)
    inputs['nsa_cv1'] = nrm(L + (CMP_BLOCK * HEAD_DIM, CMP_HIDDEN), (CMP_BLOCK * HEAD_DIM) ** -0.5)
    inputs['nsa_cv2'] = nrm(L + (CMP_HIDDEN, HEAD_DIM), CMP_HIDDEN ** -0.5)
    inputs['w_up_rw'] = nrm(L + (RW_WIDTH, D_MODEL), RW_WIDTH ** -0.5)
    inputs['w_up_nsa'] = nrm(L + (NSA_WIDTH, D_MODEL), NSA_WIDTH ** -0.5)
    inputs['w_out'] = nrm(L + (D_MODEL, D_MODEL), D_MODEL ** -0.5)
    inputs['norm_xa_g'] = gain(D_MODEL)
    inputs['norm_mem_g'] = gain(D_MODEL)
    inputs['xa_wq'] = nrm(L + (D_MODEL, D_MODEL), D_MODEL ** -0.5)
    inputs['xa_wkv'] = nrm(L + (D_MODEL, 2 * D_MODEL), D_MODEL ** -0.5)
    inputs['xa_wo'] = nrm(L + (D_MODEL, D_MODEL), D_MODEL ** -0.5)
    inputs['norm_ffn_g'] = gain(D_MODEL)
    inputs['ffn_w_gu'] = nrm(L + (D_MODEL, 2 * D_FF), D_MODEL ** -0.5)
    inputs['ffn_w_down'] = nrm(L + (D_FF, D_MODEL), D_FF ** -0.5)
    inputs['final_norm_g'] = 1.0 + nrm((D_MODEL,), 0.02)
    return inputs


def reference(x, mem, norm_mix_g, w_in, shift_mu, rw_w_up, rw_w0, rw_a_up, rw_a0, rw_g_up,
              rw_k_k, rw_k_a, rw_r_k, rw_ln_g, rw_ln_b, nsa_pe_k, nsa_pe_v, nsa_ck1, nsa_ck2,
              nsa_cv1, nsa_cv2, w_up_rw, w_up_nsa, w_out, norm_xa_g, norm_mem_g, xa_wq, xa_wkv,
              xa_wo, norm_ffn_g, ffn_w_gu, ffn_w_down, final_norm_g):
    h = x
    for l in range(DEPTH):
        h = h + hybrid_mixer(rms_norm(h, norm_mix_g[l]), w_in[l], shift_mu[l], rw_w_up[l], rw_w0[l],
                             rw_a_up[l], rw_a0[l], rw_g_up[l], rw_k_k[l], rw_k_a[l], rw_r_k[l],
                             rw_ln_g[l], rw_ln_b[l], nsa_pe_k[l], nsa_pe_v[l], nsa_ck1[l], nsa_ck2[l],
                             nsa_cv1[l], nsa_cv2[l], w_up_rw[l], w_up_nsa[l], w_out[l])
        h = h + memory_cross_attention(rms_norm(h, norm_xa_g[l]), rms_norm(mem, norm_mem_g[l]),
                                       xa_wq[l], xa_wkv[l], xa_wo[l])
        h = h + swiglu_ffn(rms_norm(h, norm_ffn_g[l]), ffn_w_gu[l], ffn_w_down[l])
    return rms_norm(h, final_norm_g)
```

```python
import functools
import math

import numpy as np
import jax
import jax.numpy as jnp
from jax import lax
from jax.experimental import pallas as pl
from jax.experimental.pallas import tpu as pltpu

F32 = jnp.float32
BF16 = jnp.bfloat16

HEAD_DIM = 64
NORM_EPS = 1e-6
ROPE_THETA = 10000.0
RW_HEADS = 8
RW_WIDTH = RW_HEADS * HEAD_DIM
DECAY_LORA = 64
AAA_LORA = 64
GATE_LORA = 128
RW_GN_EPS = 64e-5
RW_IN = 3 * RW_WIDTH + DECAY_LORA + AAA_LORA + GATE_LORA
NSA_HEADS = 8
NSA_KV_GROUPS = 2
NSA_WIDTH = NSA_HEADS * HEAD_DIM
KV_WIDTH = NSA_KV_GROUPS * HEAD_DIM
CMP_BLOCK = 32
CMP_STRIDE = 16
CMP_HIDDEN = 128
SEL_BLOCK = 64
SEL_TOPK = 8
WINDOW = 512
XA_HEADS = 4

LANES = 128
VMEM_LIMIT = 56 * 1024 * 1024
RW_CHUNK = 64
NEG_BIG = -1e30


def _cparams(*sem):
    return pltpu.CompilerParams(dimension_semantics=sem, vmem_limit_bytes=VMEM_LIMIT)


def _mm(a, b):
    return jnp.dot(a.astype(BF16), b.astype(BF16), preferred_element_type=F32)


def _mm_nt(a, b):
    return lax.dot_general(a.astype(BF16), b.astype(BF16), (((1,), (1,)), ((), ())),
                           preferred_element_type=F32)


def _mm_tn(a, b):
    return lax.dot_general(a.astype(BF16), b.astype(BF16), (((0,), (0,)), ((), ())),
                           preferred_element_type=F32)


def _split3(x):
    hi = x.astype(BF16)
    r1 = x - hi.astype(F32)
    mid = r1.astype(BF16)
    lo = (r1 - mid.astype(F32)).astype(BF16)
    return hi, mid, lo


def _mm_exact_rhs(x, m_bf16):
    hi, mid, lo = _split3(x)
    f = lambda p: jnp.dot(p, m_bf16, preferred_element_type=F32)
    return f(hi) + f(mid) + f(lo)


def _head_sums(x, pair_ones):
    return jnp.concatenate([_mm_exact_rhs(x[:, c:c + LANES], pair_ones)
                            for c in range(0, x.shape[1], LANES)], axis=1)


def _mm_exact_lhs(m_bf16, x):
    hi, mid, lo = _split3(x)
    f = lambda p: jnp.dot(m_bf16, p, preferred_element_type=F32)
    return f(hi) + f(mid) + f(lo)


def _rms(x, g):
    return x * lax.rsqrt(jnp.mean(x * x, axis=-1, keepdims=True) + NORM_EPS) * g


def _sigmoid(x):
    return 1.0 / (1.0 + jnp.exp(-x))


def _const_spec(shape):
    nd = len(shape)
    return pl.BlockSpec(shape, lambda *_: (0,) * nd)


def _in_proj_kernel(x_ref, g_ref, w_ref, prw_ref, q_ref, kv_ref, gn_ref, gm_ref):
    hn = _rms(x_ref[...], g_ref[...]).astype(BF16)
    off = 0
    for o_ref in (prw_ref, q_ref, kv_ref, gn_ref, gm_ref):
        width = o_ref.shape[1]
        for c0 in range(0, width, 512):
            c1 = min(c0 + 512, width)
            o_ref[:, c0:c1] = jnp.dot(hn, w_ref[:, off + c0:off + c1], preferred_element_type=F32)
        off += width


def _in_proj(x2, g, w_pad, tm):
    n, d = x2.shape
    widths = (RW_IN, NSA_WIDTH, 6 * KV_WIDTH, LANES, 2 * d)
    assert sum(widths) == w_pad.shape[1] and n % tm == 0
    return pl.pallas_call(
        _in_proj_kernel,
        grid=(n // tm,),
        in_specs=[pl.BlockSpec((tm, d), lambda i: (i, 0)),
                  _const_spec((1, d)),
                  _const_spec(w_pad.shape)],
        out_specs=[pl.BlockSpec((tm, w), lambda i: (i, 0)) for w in widths],
        out_shape=[jax.ShapeDtypeStruct((n, w), F32) for w in widths],
        compiler_params=_cparams("parallel"),
        name="in_proj",
    )(x2, g, w_pad)


def _rwkv_prep_kernel(p_ref, prev_ref, mu_ref, wlora_ref, w0_ref, a0_ref, gup_ref, kk_ref, ka_ref,
                      rk_ref, tri_ref, hsum_ref,
                      abar_ref, rbar_ref, btil_ref, ktil_ref, bhat_ref, khat_ref, v_ref, gam_ref,
                      g_ref, bonus_ref):
    i = pl.program_id(1)
    tt = p_ref.shape[1]
    w = RW_WIDTH
    cur = p_ref[0]
    prev_row = jnp.where(i == 0, 0.0, prev_ref[0, 7:8, :])
    row = lax.broadcasted_iota(jnp.int32, (tt, 1), 0)
    prev = jnp.where(row == 0, prev_row, pltpu.roll(cur, 1, 0))
    p = cur + mu_ref[...] * (prev - cur)

    r = p[:, 0:w]
    k = p[:, w:2 * w]
    v = p[:, 2 * w:3 * w]
    x_wa = p[:, 3 * w:3 * w + LANES]
    x_g = p[:, 3 * w + LANES:3 * w + 2 * LANES]
    lane = lax.broadcasted_iota(jnp.int32, (1, LANES), 1)
    x_wa = jnp.where(lane < DECAY_LORA, jnp.tanh(x_wa), x_wa)
    lin = _mm(x_wa, wlora_ref[...])
    w_lin = w0_ref[...] + lin[:, 0:w]
    a = _sigmoid(a0_ref[...] + lin[:, w:2 * w])
    g_ref[0] = _mm(_sigmoid(x_g), gup_ref[...])

    z = -w_lin
    softplus = jnp.maximum(z, 0.0) + jnp.log(1.0 + jnp.exp(-jnp.abs(z)))
    logw = -jnp.exp(-softplus - 0.5)

    hsum = hsum_ref[...]
    kk = k * kk_ref[...]
    kk = kk * lax.rsqrt(jnp.maximum(_head_sums(kk * kk, hsum), 1e-12))
    k2 = k * (1.0 + (a - 1.0) * ka_ref[...])
    bonus_ref[0] = _head_sums(r * k2 * rk_ref[...], hsum) * v
    v_ref[0] = v.astype(BF16)

    L = RW_CHUNK
    tri = tri_ref[...]
    kka = kk * a
    gam_rows = []
    for c in range(tt // L):
        sl = slice(c * L, (c + 1) * L)
        cs = _mm_exact_lhs(tri, logw[sl])
        cum, tot = cs[0:L], cs[L:2 * L]
        e_neg = jnp.exp(-cum)
        e_end = jnp.exp(tot - cum)
        abar_ref[0, sl, :] = (-kk[sl] * jnp.exp(cum - logw[sl])).astype(BF16)
        rbar_ref[0, sl, :] = (r[sl] * jnp.exp(cum)).astype(BF16)
        btil_ref[0, sl, :] = (kka[sl] * e_neg).astype(BF16)
        ktil_ref[0, sl, :] = (k2[sl] * e_neg).astype(BF16)
        bhat_ref[0, sl, :] = (kka[sl] * e_end).astype(BF16)
        khat_ref[0, sl, :] = (k2[sl] * e_end).astype(BF16)
        gam_rows.append(jnp.exp(tot[0:1]))
    gam_ref[0] = jnp.concatenate(gam_rows, axis=0)


def _rwkv_prep(p_rw, mu, wlora, w0, a0, gup, k_k, k_a, r_k, tri, hsum, tt):
    b, t, _ = p_rw.shape
    w = RW_WIDTH
    nck = tt // RW_CHUNK
    assert t % tt == 0 and nck % 8 == 0
    tile = lambda: pl.BlockSpec((1, tt, w), lambda bi, i: (bi, i, 0))
    seq_bf = jax.ShapeDtypeStruct((b, t, w), BF16)
    seq_f = jax.ShapeDtypeStruct((b, t, w), F32)
    return pl.pallas_call(
        _rwkv_prep_kernel,
        grid=(b, t // tt),
        in_specs=[pl.BlockSpec((1, tt, RW_IN), lambda bi, i: (bi, i, 0)),
                  pl.BlockSpec((1, 8, RW_IN), lambda bi, i: (bi, jnp.maximum(i * (tt // 8) - 1, 0), 0)),
                  _const_spec(mu.shape), _const_spec(wlora.shape), _const_spec(w0.shape),
                  _const_spec(a0.shape), _const_spec(gup.shape), _const_spec(k_k.shape),
                  _const_spec(k_a.shape), _const_spec(r_k.shape), _const_spec(tri.shape),
                  _const_spec(hsum.shape)],
        out_specs=[tile(), tile(), tile(), tile(), tile(), tile(), tile(),
                   pl.BlockSpec((1, nck, w), lambda bi, i: (bi, i, 0)),
                   tile(), tile()],
        out_shape=[seq_bf] * 7 + [jax.ShapeDtypeStruct((b, t // RW_CHUNK, w), F32), seq_f, seq_f],
        compiler_params=_cparams("parallel", "parallel"),
        name="rwkv_prep",
    )(p_rw, p_rw, mu, wlora, w0, a0, gup, k_k, k_a, r_k, tri, hsum)


def _rwkv_scan_kernel(abar_ref, rbar_ref, btil_ref, ktil_ref, bhat_ref, khat_ref, v_ref, gam_ref,
                      hsum_ref, y_ref, s_ref):
    c = pl.program_id(1)
    L = RW_CHUNK

    @pl.when(c == 0)
    def _():
        s_ref[...] = jnp.zeros_like(s_ref)

    lane = lax.broadcasted_iota(jnp.int32, (1, LANES), 1)
    m0 = lane < HEAD_DIM
    ri = lax.broadcasted_iota(jnp.int32, (2 * L, 2 * L), 0)
    ci = lax.broadcasted_iota(jnp.int32, (2 * L, 2 * L), 1)
    same = (ri // L) == (ci // L)
    low_strict = same & (ci < ri)
    low_incl = same & (ci <= ri)
    gam = gam_ref[0, pl.ds(c % 8, 1), :]
    zero = jnp.zeros((), BF16)

    def stack(x):
        return jnp.concatenate([jnp.where(m0, x, zero), jnp.where(m0, zero, x)], axis=0)

    for pr in range(RW_HEADS // 2):
        ls = slice(pr * LANES, (pr + 1) * LANES)
        xa, xr = stack(abar_ref[0, :, ls]), stack(rbar_ref[0, :, ls])
        bt, kt = stack(btil_ref[0, :, ls]), stack(ktil_ref[0, :, ls])
        bh, kh = stack(bhat_ref[0, :, ls]), stack(khat_ref[0, :, ls])
        vs = stack(v_ref[0, :, ls])
        s0 = s_ref[pr]

        x_all = jnp.concatenate([xa, xr], axis=0)
        amat = _mm_nt(x_all, jnp.concatenate([bt, kt], axis=0))
        a_ab = jnp.where(low_strict, amat[0:2 * L, 0:2 * L], 0.0)
        a_kb = jnp.where(low_strict, amat[0:2 * L, 2 * L:4 * L], 0.0)
        a_br = jnp.where(low_incl, amat[2 * L:4 * L, 0:2 * L], 0.0)
        a_kr = jnp.where(low_incl, amat[2 * L:4 * L, 2 * L:4 * L], 0.0)
        xs = _mm_nt(x_all, s0)

        u = xs[0:2 * L] + _mm(a_kb, vs)
        pw = a_ab
        n_sq = int(math.log2(L))
        for step in range(n_sq):
            u = u + _mm(pw, u)
            if step + 1 < n_sq:
                pw = _mm(pw, pw)

        uv = jnp.concatenate([u.astype(BF16), vs], axis=0)
        y2 = xs[2 * L:4 * L] + _mm(jnp.concatenate([a_br, a_kr], axis=1), uv)
        y = y2[0:L] + y2[L:2 * L]
        s_ref[pr] = s0 * gam[:, ls] + _mm_tn(uv, jnp.concatenate([bh, kh], axis=0))

        hs = hsum_ref[...]
        mu = _mm_exact_rhs(y, hs) * (1.0 / HEAD_DIM)
        d = y - mu
        var = _mm_exact_rhs(d * d, hs) * (1.0 / HEAD_DIM)
        y_ref[0, :, ls] = d * lax.rsqrt(var + RW_GN_EPS)


def _rwkv_scan(abar, rbar, btil, ktil, bhat, khat, v, gam, hsum):
    b, t, w = abar.shape
    L = RW_CHUNK
    tile = lambda: pl.BlockSpec((1, L, w), lambda bi, c: (bi, c, 0))
    return pl.pallas_call(
        _rwkv_scan_kernel,
        grid=(b, t // L),
        in_specs=[tile(), tile(), tile(), tile(), tile(), tile(), tile(),
                  pl.BlockSpec((1, 8, w), lambda bi, c: (bi, c // 8, 0)),
                  _const_spec(hsum.shape)],
        out_specs=tile(),
        out_shape=jax.ShapeDtypeStruct((b, t, w), F32),
        scratch_shapes=[pltpu.VMEM((RW_HEADS // 2, LANES, LANES), F32)],
        compiler_params=_cparams("parallel", "arbitrary"),
        name="rwkv_scan",
    )(abar, rbar, btil, ktil, bhat, khat, v, gam, hsum)


def _gelu_tanh(x):
    return 0.5 * x * (1.0 + jnp.tanh(math.sqrt(2.0 / math.pi) * (x + 0.044715 * x * x * x)))


def _nsa_cmp_kernel(kh_ref, vh_ref, pek_ref, pev_ref, k1_ref, k1f_ref, k2_ref, v1_ref, v1f_ref, v2_ref,
                    ko_ref, vo_ref):
    r, nh, wid = kh_ref.shape
    for h_ref, pe_ref, w1_ref, w1f_ref, w2_ref, o_ref in (
            (kh_ref, pek_ref, k1_ref, k1f_ref, k2_ref, ko_ref),
            (vh_ref, pev_ref, v1_ref, v1f_ref, v2_ref, vo_ref)):
        halves = h_ref[...].reshape(r * nh, wid)
        z = _mm(halves, w1_ref[...])
        bias = _mm(jnp.broadcast_to(pe_ref[...], (8, pe_ref.shape[1])), w1f_ref[...])[0:1]
        pre = z[:, 0:CMP_HIDDEN] + pltpu.roll(z[:, CMP_HIDDEN:], r * nh - 1, 0) + bias
        o_ref[...] = _mm(_gelu_tanh(pre), w2_ref[...]).reshape(r, nh, LANES)


def _nsa_cmp(k_halves, v_halves, pe_k, pe_v, k1, k1f, k2d, v1, v1f, v2d, r):
    n, nh, wid = k_halves.shape
    assert n % r == 0
    tile_in = lambda: pl.BlockSpec((r, nh, wid), lambda i: (i, 0, 0))
    tile_out = lambda: pl.BlockSpec((r, nh, LANES), lambda i: (i, 0, 0))
    out = jax.ShapeDtypeStruct((n, nh, LANES), F32)
    consts = (pe_k, pe_v, k1, k1f, k2d, v1, v1f, v2d)
    return pl.pallas_call(
        _nsa_cmp_kernel,
        grid=(n // r,),
        in_specs=[tile_in(), tile_in()] + [_const_spec(c.shape) for c in consts],
        out_specs=[tile_out(), tile_out()],
        out_shape=[out, out],
        compiler_params=_cparams("parallel"),
        name="nsa_cmp",
    )(k_halves, v_halves, pe_k, pe_v, k1, k1f, k2d, v1, v1f, v2d)


NSA_TQ = 128
NSA_TK = 256


def _rope_swap(x):
    lane = lax.broadcasted_iota(jnp.int32, (1, LANES), 1)
    first = (lane % HEAD_DIM) < (HEAD_DIM // 2)
    return jnp.where(first, pltpu.roll(x, LANES - HEAD_DIM // 2, 1), pltpu.roll(x, HEAD_DIM // 2, 1))


def _masked_softmax_parts(s, mask):
    s = jnp.where(mask, s, NEG_BIG)
    e = jnp.where(mask, jnp.exp(s - jnp.max(s, axis=-1, keepdims=True)), 0.0)
    return e / jnp.maximum(jnp.sum(e, axis=-1, keepdims=True), 1e-30)


def _nsa_attn_kernel(q_ref, gn_ref, ks_ref, vs_ref, kw_ref, vw_ref, kc_ref, vc_ref,
                     cosq_ref, sinq_ref, cosk_ref, sink_ref, ov_ref, ex_ref,
                     o_ref,
                     ks2, vs2, kw2, vw2, maskbuf, m_sc, l_sc, acc_sc):
    i = pl.program_id(1)
    tq, tk = NSA_TQ, NSA_TK
    t = ks_ref.shape[1]
    n_kt = t // tk
    lane = lax.broadcasted_iota(jnp.int32, (1, LANES), 1)
    lo_half = lane < HEAD_DIM

    @pl.when(i == 0)
    def _():
        for r0 in range(0, t, 256):
            rs = slice(r0, r0 + 256)
            cos, sin = cosk_ref[rs, :], sink_ref[rs, :]
            for src, dst, roped in ((ks_ref, ks2, True), (vs_ref, vs2, False),
                                    (kw_ref, kw2, True), (vw_ref, vw2, False)):
                x = src[0, rs, :]
                if roped:
                    x = x * cos + _rope_swap(x) * sin
                xr = pltpu.roll(x, HEAD_DIM, 1)
                dst[0, rs, :] = jnp.where(lo_half, x, xr).astype(BF16)
                dst[1, rs, :] = jnp.where(lo_half, xr, x).astype(BF16)

    tq_col = i * tq + lax.broadcasted_iota(jnp.int32, (tq, 1), 0)
    tq4 = jnp.concatenate([tq_col] * 4, axis=0)
    gates = _sigmoid(gn_ref[0])
    cosq, sinq = cosq_ref[...], sinq_ref[...]
    scale = HEAD_DIM ** -0.5

    def stack_heads(xa, xb):
        parts = [jnp.where(lo_half, xa, 0.0), jnp.where(lo_half, 0.0, xa),
                 jnp.where(lo_half, xb, 0.0), jnp.where(lo_half, 0.0, xb)]
        return (jnp.concatenate(parts, axis=0) * scale).astype(BF16)

    def attend(qs, k_sc, v_sc, g, j_lo, j_hi, use_sel):
        m_sc[...] = jnp.full(m_sc.shape, NEG_BIG, F32)
        l_sc[...] = jnp.zeros(l_sc.shape, F32)
        acc_sc[...] = jnp.zeros(acc_sc.shape, F32)

        def body(j, carry):
            r0 = pl.multiple_of(j * tk, tk)
            kt = k_sc[g, pl.ds(r0, tk), :]
            vt = v_sc[g, pl.ds(r0, tk), :]
            s = lax.dot_general(qs, kt, (((1,), (1,)), ((), ())), preferred_element_type=F32)
            kpos = r0 + lax.broadcasted_iota(jnp.int32, (1, tk), 1)
            mask = kpos <= tq4
            if use_sel:
                sel = maskbuf[g, j] > 0.5
                mask = mask & jnp.concatenate([sel] * 4, axis=0)
            else:
                mask = mask & (kpos > tq4 - WINDOW)
            s = jnp.where(mask, s, NEG_BIG)
            m_old = m_sc[...]
            m_new = jnp.maximum(m_old, jnp.max(s, axis=-1, keepdims=True))
            alpha = jnp.exp(m_old - m_new)
            p = jnp.where(mask, jnp.exp(s - m_new), 0.0)
            l_sc[...] = alpha * l_sc[...] + jnp.sum(p, axis=-1, keepdims=True)
            acc_sc[...] = alpha * acc_sc[...] + jnp.dot(p.astype(BF16), vt, preferred_element_type=F32)
            m_sc[...] = m_new
            return carry

        lax.fori_loop(j_lo, j_hi + 1, body, 0)
        return acc_sc[...] / jnp.maximum(l_sc[...], 1e-30)

    j_last = (i * tq + tq - 1) // tk
    j_first_win = jnp.maximum(i * tq - (WINDOW - 1), 0) // tk

    for g in range(NSA_KV_GROUPS):
        qa = q_ref[0, :, (2 * g) * LANES:(2 * g + 1) * LANES]
        qb = q_ref[0, :, (2 * g + 1) * LANES:(2 * g + 2) * LANES]
        q_cmp = stack_heads(qa, qb)
        q_rot = stack_heads(qa * cosq + _rope_swap(qa) * sinq, qb * cosq + _rope_swap(qb) * sinq)

        s_c = lax.dot_general(q_cmp, kc_ref[g].astype(BF16), (((1,), (1,)), ((), ())),
                              preferred_element_type=F32)
        n_valid = (t - CMP_BLOCK) // CMP_STRIDE + 1
        cmask = (lane * CMP_STRIDE + (CMP_BLOCK - 1) <= tq4) & (lane < n_valid)
        p_c = _masked_softmax_parts(s_c, cmask)
        o_cmp = _mm(p_c, vc_ref[g])

        p_sum = p_c[0:tq] + p_c[tq:2 * tq] + p_c[2 * tq:3 * tq] + p_c[3 * tq:4 * tq]
        imp = _mm_exact_rhs(p_sum, ov_ref[...])
        cur = tq_col // SEL_BLOCK
        forced = (lane == 0) | (lane == cur) | (lane == cur - 1)
        imp = jnp.where(forced, 1e4, jnp.where(lane <= cur, imp, -1.0))
        n_sel = t // SEL_BLOCK
        rank = jnp.zeros((tq, LANES), F32)
        for b in range(n_sel):
            col = jnp.broadcast_to(imp[:, b:b + 1], (tq, LANES))
            ahead = (col > imp) | ((col == imp) & (b < lane))
            rank = rank + jnp.where(ahead, 1.0, 0.0)
        sel = jnp.where((rank < min(SEL_TOPK, n_sel)) & (lane < n_sel), 1.0, 0.0).astype(BF16)
        for jt in range(n_kt):
            maskbuf[g, jt] = jnp.dot(sel, ex_ref[:, jt * tk:(jt + 1) * tk],
                                     preferred_element_type=F32).astype(BF16)

        o_sel = attend(q_rot, ks2, vs2, g, 0, j_last, True)
        o_win = attend(q_rot, kw2, vw2, g, j_first_win, j_last, False)

        for pr in range(2):
            halves = []
            for hh in (2 * pr, 2 * pr + 1):
                rows = slice(hh * tq, (hh + 1) * tq)
                base = (4 * g + hh) * 3
                halves.append(gates[:, base:base + 1] * o_cmp[rows]
                              + gates[:, base + 1:base + 2] * o_sel[rows]
                              + gates[:, base + 2:base + 3] * o_win[rows])
            o_ref[0, :, (2 * g + pr) * LANES:(2 * g + pr + 1) * LANES] = jnp.where(lo_half, halves[0], halves[1])


def _nsa_attn(q, gn, kv, kc2, vc2, cos, sin, ov, ex):
    b, t, _ = q.shape
    tq, tk = NSA_TQ, NSA_TK
    assert t % tq == 0 and t % tk == 0
    full = lambda col: pl.BlockSpec((1, t, LANES), lambda bi, i: (bi, 0, col))
    return pl.pallas_call(
        _nsa_attn_kernel,
        grid=(b, t // tq),
        in_specs=[pl.BlockSpec((1, tq, NSA_WIDTH), lambda bi, i: (bi, i, 0)),
                  pl.BlockSpec((1, tq, LANES), lambda bi, i: (bi, i, 0)),
                  full(2), full(3), full(4), full(5),
                  pl.BlockSpec((NSA_KV_GROUPS, kc2.shape[1], LANES), lambda bi, i: (bi, 0, 0)),
                  pl.BlockSpec((NSA_KV_GROUPS, vc2.shape[1], LANES), lambda bi, i: (bi, 0, 0)),
                  pl.BlockSpec((tq, LANES), lambda bi, i: (i, 0)),
                  pl.BlockSpec((tq, LANES), lambda bi, i: (i, 0)),
                  _const_spec(cos.shape), _const_spec(sin.shape),
                  _const_spec(ov.shape), _const_spec(ex.shape)],
        out_specs=pl.BlockSpec((1, tq, NSA_WIDTH), lambda bi, i: (bi, i, 0)),
        out_shape=jax.ShapeDtypeStruct((b, t, NSA_WIDTH), F32),
        scratch_shapes=[pltpu.VMEM((NSA_KV_GROUPS, t, LANES), BF16) for _ in range(4)]
        + [pltpu.VMEM((NSA_KV_GROUPS, t // tk, tq, tk), BF16),
           pltpu.VMEM((4 * tq, 1), F32), pltpu.VMEM((4 * tq, 1), F32),
           pltpu.VMEM((4 * tq, LANES), F32)],
        compiler_params=_cparams("parallel", "arbitrary"),
        name="nsa_attn",
    )(q, gn, kv, kv, kv, kv, kc2, vc2, cos, sin, cos, sin, ov, ex)


def _merge_kernel(x_ref, yn_ref, bonus_ref, g_ref, lng_ref, lnb_ref, on_ref, gm_ref,
                  wrw_ref, wnsa_ref, wout_ref, o_ref):
    d = x_ref.shape[1]
    y_rw = (yn_ref[...] * lng_ref[...] + lnb_ref[...] + bonus_ref[...]) * g_ref[...]
    t_rw = _mm(y_rw, wrw_ref[...])
    t_ns = _mm(on_ref[...], wnsa_ref[...])
    gm = gm_ref[...]
    mix = _sigmoid(gm[:, 0:d]) * t_rw + _sigmoid(gm[:, d:2 * d]) * t_ns
    o_ref[...] = x_ref[...] + _mm(mix, wout_ref[...])


def _merge_out(x2, yn, bonus, g, ln_g, ln_b, o_nsa, gm, w_rw, w_nsa, w_out, tm):
    n, d = x2.shape
    row = lambda w: pl.BlockSpec((tm, w), lambda i: (i, 0))
    return pl.pallas_call(
        _merge_kernel,
        grid=(n // tm,),
        in_specs=[row(d), row(RW_WIDTH), row(RW_WIDTH), row(RW_WIDTH),
                  _const_spec(ln_g.shape), _const_spec(ln_b.shape),
                  row(NSA_WIDTH), row(2 * d),
                  _const_spec(w_rw.shape), _const_spec(w_nsa.shape), _const_spec(w_out.shape)],
        out_specs=row(d),
        out_shape=jax.ShapeDtypeStruct((n, d), F32),
        compiler_params=_cparams("parallel"),
        name="merge_out",
    )(x2, yn, bonus, g, ln_g, ln_b, o_nsa, gm, w_rw, w_nsa, w_out)


def _mem_kv_kernel(m_ref, g_ref, w_ref, o_ref):
    o_ref[0] = _mm(_rms(m_ref[0], g_ref[...]), w_ref[...]).astype(BF16)


def _mem_kv(mem, g, wkv):
    b, m, d = mem.shape
    return pl.pallas_call(
        _mem_kv_kernel,
        grid=(b,),
        in_specs=[pl.BlockSpec((1, m, d), lambda bi: (bi, 0, 0)), _const_spec(g.shape),
                  _const_spec(wkv.shape)],
        out_specs=pl.BlockSpec((1, m, 2 * d), lambda bi: (bi, 0, 0)),
        out_shape=jax.ShapeDtypeStruct((b, m, 2 * d), BF16),
        compiler_params=_cparams("parallel"),
        name="mem_kv",
    )(mem, g, wkv)


def _xattn_kernel(h_ref, g_ref, wq_ref, kv_ref, wo_ref, o_ref):
    d = h_ref.shape[2]
    hd = d // XA_HEADS
    h = h_ref[0]
    q = (_mm(_rms(h, g_ref[...]), wq_ref[...]) * hd ** -0.5).astype(BF16)
    outs = []
    for hh in range(XA_HEADS):
        cs = slice(hh * hd, (hh + 1) * hd)
        s = lax.dot_general(q[:, cs], kv_ref[0, :, cs], (((1,), (1,)), ((), ())),
                            preferred_element_type=F32)
        e = jnp.exp(s - jnp.max(s, axis=-1, keepdims=True))
        p = e / jnp.sum(e, axis=-1, keepdims=True)
        outs.append(jnp.dot(p.astype(BF16), kv_ref[0, :, d + hh * hd:d + (hh + 1) * hd],
                            preferred_element_type=F32))
    o_ref[0] = h + _mm(jnp.concatenate(outs, axis=1), wo_ref[...])


def _xattn(h3, g, wq, kv, wo, tm):
    b, t, d = h3.shape
    m = kv.shape[1]
    return pl.pallas_call(
        _xattn_kernel,
        grid=(b, t // tm),
        in_specs=[pl.BlockSpec((1, tm, d), lambda bi, i: (bi, i, 0)), _const_spec(g.shape),
                  _const_spec(wq.shape),
                  pl.BlockSpec((1, m, 2 * d), lambda bi, i: (bi, 0, 0)),
                  _const_spec(wo.shape)],
        out_specs=pl.BlockSpec((1, tm, d), lambda bi, i: (bi, i, 0)),
        out_shape=jax.ShapeDtypeStruct((b, t, d), F32),
        compiler_params=_cparams("parallel", "parallel"),
        name="xattn",
    )(h3, g, wq, kv, wo)


FFN_CHUNK = 256


def _ffn_kernel(h_ref, g_ref, wgu_ref, wd_ref, gf_ref, o_ref):
    h = h_ref[...]
    hn = _rms(h, g_ref[...]).astype(BF16)
    dff = wd_ref.shape[0]
    acc = jnp.zeros(h.shape, F32)
    for c0 in range(0, dff, FFN_CHUNK):
        gate = jnp.dot(hn, wgu_ref[:, c0:c0 + FFN_CHUNK], preferred_element_type=F32)
        up = jnp.dot(hn, wgu_ref[:, dff + c0:dff + c0 + FFN_CHUNK], preferred_element_type=F32)
        act = (gate * _sigmoid(gate) * up).astype(BF16)
        acc = acc + jnp.dot(act, wd_ref[c0:c0 + FFN_CHUNK, :], preferred_element_type=F32)
    o_ref[...] = _rms(h + acc, gf_ref[...])


def _ffn(h2, g, wgu, wd, gf, tm):
    n, d = h2.shape
    assert wd.shape[0] % FFN_CHUNK == 0
    return pl.pallas_call(
        _ffn_kernel,
        grid=(n // tm,),
        in_specs=[pl.BlockSpec((tm, d), lambda i: (i, 0)), _const_spec(g.shape),
                  _const_spec(wgu.shape), _const_spec(wd.shape), _const_spec(gf.shape)],
        out_specs=pl.BlockSpec((tm, d), lambda i: (i, 0)),
        out_shape=jax.ShapeDtypeStruct((n, d), F32),
        compiler_params=_cparams("parallel"),
        name="ffn",
    )(h2, g, wgu, wd, gf)


def _head_sum_matrix():
    idx = np.arange(LANES) // HEAD_DIM
    return jnp.asarray(idx[:, None] == idx[None, :], BF16)


def _prefix_matrix():
    L = RW_CHUNK
    tri = np.tril(np.ones((L, L), np.float32))
    return jnp.asarray(np.concatenate([tri, np.ones((L, L), np.float32)], axis=0), BF16)


def _rope_tables(t):
    half = HEAD_DIM // 2
    inv_freq = ROPE_THETA ** (-jnp.arange(half, dtype=F32) / half)
    ang = jnp.arange(t, dtype=F32)[:, None] * inv_freq[None, :]
    cos, sin = jnp.cos(ang), jnp.sin(ang)
    cos128 = jnp.tile(cos, (1, LANES // half))
    sin128 = jnp.tile(jnp.concatenate([-sin, sin], axis=1), (1, LANES // HEAD_DIM))
    return cos128, sin128


def _overlap_matrix(t):
    n_cmp = (t - CMP_BLOCK) // CMP_STRIDE + 1
    n_sel = t // SEL_BLOCK
    cs = np.arange(n_cmp) * CMP_STRIDE
    ss = np.arange(n_sel) * SEL_BLOCK
    ov = np.clip(np.minimum(cs[:, None] + CMP_BLOCK, ss[None, :] + SEL_BLOCK)
                 - np.maximum(cs[:, None], ss[None, :]), 0, None) / CMP_BLOCK
    out = np.zeros((LANES, LANES), np.float32)
    out[:n_cmp, :n_sel] = ov
    return jnp.asarray(out, BF16)


def _expand_matrix(t):
    blk = np.arange(t) // SEL_BLOCK
    return jnp.asarray(np.arange(LANES)[:, None] == blk[None, :], BF16)


def kernel(x, mem, norm_mix_g, w_in, shift_mu, rw_w_up, rw_w0, rw_a_up, rw_a0, rw_g_up, rw_k_k, rw_k_a,
           rw_r_k, rw_ln_g, rw_ln_b, nsa_pe_k, nsa_pe_v, nsa_ck1, nsa_ck2, nsa_cv1, nsa_cv2, w_up_rw,
           w_up_nsa, w_out, norm_xa_g, norm_mem_g, xa_wq, xa_wkv, xa_wo, norm_ffn_g, ffn_w_gu,
           ffn_w_down, final_norm_g):
    b, t, d = x.shape
    n = b * t
    h = x.reshape(n, d)
    hsum = _head_sum_matrix()
    tri = _prefix_matrix()
    cos, sin = _rope_tables(t)
    ov = _overlap_matrix(t)
    ex = _expand_matrix(t)
    row = lambda a: a.reshape(1, -1)
    n_half = t // CMP_STRIDE
    assert t // SEL_BLOCK <= LANES and n_half <= LANES

    for l in range(w_in.shape[0]):
        c_g = RW_IN + NSA_WIDTH + 6 * KV_WIDTH
        n_gate = 3 * NSA_HEADS
        w_pad = jnp.concatenate(
            [w_in[l][:, :c_g + n_gate], jnp.zeros((d, LANES - n_gate), F32), w_in[l][:, c_g + n_gate:]],
            axis=1).astype(BF16)
        p_rw, q, kv, gn, gm = _in_proj(h, row(norm_mix_g[l]), w_pad, 256)

        z64 = jnp.zeros((DECAY_LORA, RW_WIDTH), F32)
        wlora = jnp.concatenate([jnp.concatenate([rw_w_up[l], z64], axis=1),
                                 jnp.concatenate([z64, rw_a_up[l]], axis=1)], axis=0).astype(BF16)
        prep = _rwkv_prep(p_rw.reshape(b, t, RW_IN), row(shift_mu[l]), wlora, row(rw_w0[l]),
                          row(rw_a0[l]), rw_g_up[l].astype(BF16), row(rw_k_k[l]), row(rw_k_a[l]),
                          row(rw_r_k[l]), tri, hsum, 512)
        abar, rbar, btil, ktil, bhat, khat, v_rw, gam, g_rw, bonus = prep
        yn = _rwkv_scan(abar, rbar, btil, ktil, bhat, khat, v_rw, gam, hsum)

        def halves(cols):
            a = kv[:, cols * LANES:(cols + 1) * LANES].reshape(b, t, NSA_KV_GROUPS, HEAD_DIM)
            return a.transpose(0, 2, 1, 3).reshape(b * NSA_KV_GROUPS, n_half, CMP_STRIDE * HEAD_DIM)

        def w1_halves(w1):
            hw = CMP_STRIDE * HEAD_DIM
            return jnp.concatenate([w1[:hw], w1[hw:]], axis=1).astype(BF16)

        dup = lambda w2: jnp.concatenate([w2, w2], axis=1).astype(BF16)
        kc2, vc2 = _nsa_cmp(halves(0), halves(1), nsa_pe_k[l].reshape(1, -1), nsa_pe_v[l].reshape(1, -1),
                            w1_halves(nsa_ck1[l]), nsa_ck1[l].astype(BF16), dup(nsa_ck2[l]),
                            w1_halves(nsa_cv1[l]), nsa_cv1[l].astype(BF16), dup(nsa_cv2[l]), 4)
        o_nsa = _nsa_attn(q.reshape(b, t, NSA_WIDTH), gn.reshape(b, t, LANES),
                          kv.reshape(b, t, 6 * KV_WIDTH), kc2, vc2, cos, sin, ov, ex)

        h = _merge_out(h, yn.reshape(n, RW_WIDTH), bonus.reshape(n, RW_WIDTH), g_rw.reshape(n, RW_WIDTH),
                       row(rw_ln_g[l]), row(rw_ln_b[l]), o_nsa.reshape(n, NSA_WIDTH), gm,
                       w_up_rw[l].astype(BF16), w_up_nsa[l].astype(BF16), w_out[l].astype(BF16), 512)

        mkv = _mem_kv(mem, row(norm_mem_g[l]), xa_wkv[l].astype(BF16))
        h = _xattn(h.reshape(b, t, d), row(norm_xa_g[l]), xa_wq[l].astype(BF16), mkv,
                   xa_wo[l].astype(BF16), 512).reshape(n, d)

        last = l == w_in.shape[0] - 1
        gf = row(final_norm_g) if last else None
        assert last, "only a single layer is fused with the final norm"
        h = _ffn(h, row(norm_ffn_g[l]), ffn_w_gu[l].astype(BF16), ffn_w_down[l].astype(BF16), gf, 512)
    return h.reshape(b, t, d)
```

```python
import functools
import math

import numpy as np
import jax
import jax.numpy as jnp
from jax import lax
from jax.experimental import pallas as pl
from jax.experimental.pallas import tpu as pltpu

F32 = jnp.float32
BF16 = jnp.bfloat16

HEAD_DIM = 64
NORM_EPS = 1e-6
ROPE_THETA = 10000.0
RW_HEADS = 8
RW_WIDTH = RW_HEADS * HEAD_DIM
DECAY_LORA = 64
AAA_LORA = 64
GATE_LORA = 128
RW_GN_EPS = 64e-5
RW_IN = 3 * RW_WIDTH + DECAY_LORA + AAA_LORA + GATE_LORA
NSA_HEADS = 8
NSA_KV_GROUPS = 2
NSA_WIDTH = NSA_HEADS * HEAD_DIM
KV_WIDTH = NSA_KV_GROUPS * HEAD_DIM
CMP_BLOCK = 32
CMP_STRIDE = 16
CMP_HIDDEN = 128
SEL_BLOCK = 64
SEL_TOPK = 8
WINDOW = 512
XA_HEADS = 4

LANES = 128
VMEM_LIMIT = 56 * 1024 * 1024
RW_CHUNK = 64
NEG_BIG = -1e30


def _cparams(*sem):
    return pltpu.CompilerParams(dimension_semantics=sem, vmem_limit_bytes=VMEM_LIMIT)


def _mm(a, b):
    return jnp.dot(a.astype(BF16), b.astype(BF16), preferred_element_type=F32)


def _mm_nt(a, b):
    return lax.dot_general(a.astype(BF16), b.astype(BF16), (((1,), (1,)), ((), ())),
                           preferred_element_type=F32)


def _mm_tn(a, b):
    return lax.dot_general(a.astype(BF16), b.astype(BF16), (((0,), (0,)), ((), ())),
                           preferred_element_type=F32)


def _split3(x):
    hi = x.astype(BF16)
    r1 = x - hi.astype(F32)
    mid = r1.astype(BF16)
    lo = (r1 - mid.astype(F32)).astype(BF16)
    return hi, mid, lo


def _mm_exact_rhs(x, m_bf16):
    hi, mid, lo = _split3(x)
    f = lambda p: jnp.dot(p, m_bf16, preferred_element_type=F32)
    return f(hi) + f(mid) + f(lo)


def _head_sums(x, pair_ones):
    return jnp.concatenate([_mm_exact_rhs(x[:, c:c + LANES], pair_ones)
                            for c in range(0, x.shape[1], LANES)], axis=1)


def _mm_exact_lhs(m_bf16, x):
    hi, mid, lo = _split3(x)
    f = lambda p: jnp.dot(m_bf16, p, preferred_element_type=F32)
    return f(hi) + f(mid) + f(lo)


def _rms(x, g):
    return x * lax.rsqrt(jnp.mean(x * x, axis=-1, keepdims=True) + NORM_EPS) * g


def _sigmoid(x):
    return 1.0 / (1.0 + jnp.exp(-x))


def _const_spec(shape):
    nd = len(shape)
    return pl.BlockSpec(shape, lambda *_: (0,) * nd)


def _in_proj_kernel(x_ref, g_ref, w_ref, prw_ref, q_ref, kv_ref, gn_ref, gm_ref):
    hn = _rms(x_ref[...], g_ref[...]).astype(BF16)
    off = 0
    for o_ref in (prw_ref, q_ref, kv_ref, gn_ref, gm_ref):
        width = o_ref.shape[1]
        for c0 in range(0, width, 512):
            c1 = min(c0 + 512, width)
            o_ref[:, c0:c1] = jnp.dot(hn, w_ref[:, off + c0:off + c1], preferred_element_type=F32)
        off += width


def _in_proj(x2, g, w_pad, tm):
    n, d = x2.shape
    widths = (RW_IN, NSA_WIDTH, 6 * KV_WIDTH, LANES, 2 * d)
    assert sum(widths) == w_pad.shape[1] and n % tm == 0
    return pl.pallas_call(
        _in_proj_kernel,
        grid=(n // tm,),
        in_specs=[pl.BlockSpec((tm, d), lambda i: (i, 0)),
                  _const_spec((1, d)),
                  _const_spec(w_pad.shape)],
        out_specs=[pl.BlockSpec((tm, w), lambda i: (i, 0)) for w in widths],
        out_shape=[jax.ShapeDtypeStruct((n, w), F32) for w in widths],
        compiler_params=_cparams("parallel"),
        name="in_proj",
    )(x2, g, w_pad)


def _rwkv_prep_kernel(p_ref, prev_ref, mu_ref, wlora_ref, w0_ref, a0_ref, gup_ref, kk_ref, ka_ref,
                      rk_ref, tri_ref, hsum_ref,
                      abar_ref, rbar_ref, btil_ref, ktil_ref, bhat_ref, khat_ref, v_ref, gam_ref,
                      g_ref, bonus_ref):
    i = pl.program_id(1)
    tt = p_ref.shape[1]
    w = RW_WIDTH
    cur = p_ref[0]
    prev_row = jnp.where(i == 0, 0.0, prev_ref[0, 7:8, :])
    row = lax.broadcasted_iota(jnp.int32, (tt, 1), 0)
    prev = jnp.where(row == 0, prev_row, pltpu.roll(cur, 1, 0))
    p = cur + mu_ref[...] * (prev - cur)

    r = p[:, 0:w]
    k = p[:, w:2 * w]
    v = p[:, 2 * w:3 * w]
    x_wa = p[:, 3 * w:3 * w + LANES]
    x_g = p[:, 3 * w + LANES:3 * w + 2 * LANES]
    lane = lax.broadcasted_iota(jnp.int32, (1, LANES), 1)
    x_wa = jnp.where(lane < DECAY_LORA, jnp.tanh(x_wa), x_wa)
    lin = _mm(x_wa, wlora_ref[...])
    w_lin = w0_ref[...] + lin[:, 0:w]
    a = _sigmoid(a0_ref[...] + lin[:, w:2 * w])
    g_ref[0] = _mm(_sigmoid(x_g), gup_ref[...])

    z = -w_lin
    softplus = jnp.maximum(z, 0.0) + jnp.log(1.0 + jnp.exp(-jnp.abs(z)))
    logw = -jnp.exp(-softplus - 0.5)

    hsum = hsum_ref[...]
    kk = k * kk_ref[...]
    kk = kk * lax.rsqrt(jnp.maximum(_head_sums(kk * kk, hsum), 1e-12))
    k2 = k * (1.0 + (a - 1.0) * ka_ref[...])
    bonus_ref[0] = _head_sums(r * k2 * rk_ref[...], hsum) * v
    v_ref[0] = v.astype(BF16)

    L = RW_CHUNK
    tri = tri_ref[...]
    kka = kk * a
    gam_rows = []
    for c in range(tt // L):
        sl = slice(c * L, (c + 1) * L)
        cs = _mm_exact_lhs(tri, logw[sl])
        cum, tot = cs[0:L], cs[L:2 * L]
        e_neg = jnp.exp(-cum)
        e_end = jnp.exp(tot - cum)
        abar_ref[0, sl, :] = (-kk[sl] * jnp.exp(cum - logw[sl])).astype(BF16)
        rbar_ref[0, sl, :] = (r[sl] * jnp.exp(cum)).astype(BF16)
        btil_ref[0, sl, :] = (kka[sl] * e_neg).astype(BF16)
        ktil_ref[0, sl, :] = (k2[sl] * e_neg).astype(BF16)
        bhat_ref[0, sl, :] = (kka[sl] * e_end).astype(BF16)
        khat_ref[0, sl, :] = (k2[sl] * e_end).astype(BF16)
        gam_rows.append(jnp.exp(tot[0:1]))
    gam_ref[0] = jnp.concatenate(gam_rows, axis=0)


def _rwkv_prep(p_rw, mu, wlora, w0, a0, gup, k_k, k_a, r_k, tri, hsum, tt):
    b, t, _ = p_rw.shape
    w = RW_WIDTH
    nck = tt // RW_CHUNK
    assert t % tt == 0 and nck % 8 == 0
    tile = lambda: pl.BlockSpec((1, tt, w), lambda bi, i: (bi, i, 0))
    seq_bf = jax.ShapeDtypeStruct((b, t, w), BF16)
    seq_f = jax.ShapeDtypeStruct((b, t, w), F32)
    return pl.pallas_call(
        _rwkv_prep_kernel,
        grid=(b, t // tt),
        in_specs=[pl.BlockSpec((1, tt, RW_IN), lambda bi, i: (bi, i, 0)),
                  pl.BlockSpec((1, 8, RW_IN), lambda bi, i: (bi, jnp.maximum(i * (tt // 8) - 1, 0), 0)),
                  _const_spec(mu.shape), _const_spec(wlora.shape), _const_spec(w0.shape),
                  _const_spec(a0.shape), _const_spec(gup.shape), _const_spec(k_k.shape),
                  _const_spec(k_a.shape), _const_spec(r_k.shape), _const_spec(tri.shape),
                  _const_spec(hsum.shape)],
        out_specs=[tile(), tile(), tile(), tile(), tile(), tile(), tile(),
                   pl.BlockSpec((1, nck, w), lambda bi, i: (bi, i, 0)),
                   tile(), tile()],
        out_shape=[seq_bf] * 7 + [jax.ShapeDtypeStruct((b, t // RW_CHUNK, w), F32), seq_f, seq_f],
        compiler_params=_cparams("parallel", "parallel"),
        name="rwkv_prep",
    )(p_rw, p_rw, mu, wlora, w0, a0, gup, k_k, k_a, r_k, tri, hsum)


def _rwkv_scan_kernel(abar_ref, rbar_ref, btil_ref, ktil_ref, bhat_ref, khat_ref, v_ref, gam_ref,
                      y_ref, s_ref):
    c = pl.program_id(1)
    L = RW_CHUNK

    @pl.when(c == 0)
    def _():
        s_ref[...] = jnp.zeros_like(s_ref)

    lane = lax.broadcasted_iota(jnp.int32, (1, LANES), 1)
    m0 = lane < HEAD_DIM
    ri = lax.broadcasted_iota(jnp.int32, (2 * L, 2 * L), 0)
    ci = lax.broadcasted_iota(jnp.int32, (2 * L, 2 * L), 1)
    same = (ri // L) == (ci // L)
    low_strict = same & (ci < ri)
    low_incl = same & (ci <= ri)
    zero = jnp.zeros((), BF16)
    nb = abar_ref.shape[0]
    n_pair = RW_HEADS // 2

    def stack(ref, bi, ls):
        x = ref[bi, :, ls]
        return jnp.concatenate([jnp.where(m0, x, zero), jnp.where(m0, zero, x)], axis=0)

    chains = [(bi, pr) for bi in range(nb) for pr in range(n_pair)]
    nch = len(chains)
    lanes_of = lambda pr: slice(pr * LANES, (pr + 1) * LANES)

    x_all = [jnp.concatenate([stack(abar_ref, bi, lanes_of(pr)), stack(rbar_ref, bi, lanes_of(pr))], axis=0)
             for bi, pr in chains]
    vs = [stack(v_ref, bi, lanes_of(pr)) for bi, pr in chains]
    s0 = [s_ref[bi * n_pair + pr] for bi, pr in chains]
    amat = [_mm_nt(x_all[i], jnp.concatenate([stack(btil_ref, bi, lanes_of(pr)),
                                              stack(ktil_ref, bi, lanes_of(pr))], axis=0))
            for i, (bi, pr) in enumerate(chains)]
    xs = [_mm_nt(x_all[i], s0[i]) for i in range(nch)]

    u = [xs[i][0:2 * L] + _mm(jnp.where(low_strict, amat[i][0:2 * L, 2 * L:4 * L], 0.0), vs[i])
         for i in range(nch)]
    pw = [jnp.where(low_strict, a[0:2 * L, 0:2 * L], 0.0).astype(BF16) for a in amat]
    n_sq = int(math.log2(L))
    for step in range(n_sq):
        u = [u[i] + _mm(pw[i], u[i]) for i in range(nch)]
        if step + 1 < n_sq:
            pw = [_mm(q, q).astype(BF16) for q in pw]

    uv = [jnp.concatenate([u[i].astype(BF16), vs[i]], axis=0) for i in range(nch)]
    gam_rows = [gam_ref[bi, pl.ds(c % 8, 1), :] for bi in range(nb)]
    for i, (bi, pr) in enumerate(chains):
        gam = gam_rows[bi][:, lanes_of(pr)]
        bk_hat = jnp.concatenate([stack(bhat_ref, bi, lanes_of(pr)), stack(khat_ref, bi, lanes_of(pr))],
                                 axis=0)
        s_ref[bi * n_pair + pr] = s0[i] * gam + _mm_tn(uv[i], bk_hat)
    def head_mean(z):
        lo = jnp.sum(jnp.where(m0, z, 0.0), axis=-1, keepdims=True)
        hi = jnp.sum(jnp.where(m0, 0.0, z), axis=-1, keepdims=True)
        return jnp.where(m0, lo, hi) * (1.0 / HEAD_DIM)

    for i, (bi, pr) in enumerate(chains):
        a_r =jnp.concatenate([jnp.where(low_incl, amat[i][2 * L:4 * L, 0:2 * L], 0.0),
                               jnp.where(low_incl, amat[i][2 * L:4 * L, 2 * L:4 * L], 0.0)], axis=1)
        y2 = xs[i][2 * L:4 * L] + _mm(a_r, uv[i])
        y = y2[0:L] + y2[L:2 * L]
        d = y - head_mean(y)
        y_ref[bi, :, lanes_of(pr)] = d * lax.rsqrt(head_mean(d * d) + RW_GN_EPS)


RW_SCAN_BATCH = 4


def _rwkv_scan(abar, rbar, btil, ktil, bhat, khat, v, gam):
    b, t, w = abar.shape
    L = RW_CHUNK
    nb = RW_SCAN_BATCH
    assert b % nb == 0
    tile = lambda: pl.BlockSpec((nb, L, w), lambda bi, c: (bi, c, 0))
    return pl.pallas_call(
        _rwkv_scan_kernel,
        grid=(b // nb, t // L),
        in_specs=[tile(), tile(), tile(), tile(), tile(), tile(), tile(),
                  pl.BlockSpec((nb, 8, w), lambda bi, c: (bi, c // 8, 0))],
        out_specs=tile(),
        out_shape=jax.ShapeDtypeStruct((b, t, w), F32),
        scratch_shapes=[pltpu.VMEM((nb * (RW_HEADS // 2), LANES, LANES), F32)],
        compiler_params=_cparams("parallel", "arbitrary"),
        name="rwkv_scan",
    )(abar, rbar, btil, ktil, bhat, khat, v, gam)


def _gelu_tanh(x):
    return 0.5 * x * (1.0 + jnp.tanh(math.sqrt(2.0 / math.pi) * (x + 0.044715 * x * x * x)))


def _nsa_cmp_kernel(kh_ref, vh_ref, pek_ref, pev_ref, k1_ref, k1f_ref, k2_ref, v1_ref, v1f_ref, v2_ref,
                    ko_ref, vo_ref):
    r, nh, wid = kh_ref.shape
    for h_ref, pe_ref, w1_ref, w1f_ref, w2_ref, o_ref in (
            (kh_ref, pek_ref, k1_ref, k1f_ref, k2_ref, ko_ref),
            (vh_ref, pev_ref, v1_ref, v1f_ref, v2_ref, vo_ref)):
        halves = h_ref[...].reshape(r * nh, wid)
        z = _mm(halves, w1_ref[...])
        bias = _mm(jnp.broadcast_to(pe_ref[...], (8, pe_ref.shape[1])), w1f_ref[...])[0:1]
        pre = z[:, 0:CMP_HIDDEN] + pltpu.roll(z[:, CMP_HIDDEN:], r * nh - 1, 0) + bias
        o_ref[...] = _mm(_gelu_tanh(pre), w2_ref[...]).reshape(r, nh, LANES)


def _nsa_cmp(k_halves, v_halves, pe_k, pe_v, k1, k1f, k2d, v1, v1f, v2d, r):
    n, nh, wid = k_halves.shape
    assert n % r == 0
    tile_in = lambda: pl.BlockSpec((r, nh, wid), lambda i: (i, 0, 0))
    tile_out = lambda: pl.BlockSpec((r, nh, LANES), lambda i: (i, 0, 0))
    out = jax.ShapeDtypeStruct((n, nh, LANES), F32)
    consts = (pe_k, pe_v, k1, k1f, k2d, v1, v1f, v2d)
    return pl.pallas_call(
        _nsa_cmp_kernel,
        grid=(n // r,),
        in_specs=[tile_in(), tile_in()] + [_const_spec(c.shape) for c in consts],
        out_specs=[tile_out(), tile_out()],
        out_shape=[out, out],
        compiler_params=_cparams("parallel"),
        name="nsa_cmp",
    )(k_halves, v_halves, pe_k, pe_v, k1, k1f, k2d, v1, v1f, v2d)


NSA_TQ = 128
NSA_TK = 256


def _rope_swap(x):
    lane = lax.broadcasted_iota(jnp.int32, (1, LANES), 1)
    first = (lane % HEAD_DIM) < (HEAD_DIM // 2)
    return jnp.where(first, pltpu.roll(x, LANES - HEAD_DIM // 2, 1), pltpu.roll(x, HEAD_DIM // 2, 1))


def _masked_softmax_parts(s, mask):
    s = jnp.where(mask, s, NEG_BIG)
    e = jnp.where(mask, jnp.exp(s - jnp.max(s, axis=-1, keepdims=True)), 0.0)
    return e / jnp.maximum(jnp.sum(e, axis=-1, keepdims=True), 1e-30)


def _nsa_attn_kernel(q_ref, gn_ref, ks_ref, vs_ref, kw_ref, vw_ref, kc_ref, vc_ref,
                     cosq_ref, sinq_ref, cosk_ref, sink_ref, ovt_ref, ext_ref,
                     o_ref,
                     k2_sc, vt_sc, qrot_sc, bias_sc, m_sc, acc_sc):
    i = pl.program_id(1)
    tq, tk = NSA_TQ, NSA_TK
    t = ks_ref.shape[1]
    n_kt = t // tk
    n_sel = t // SEL_BLOCK
    lane = lax.broadcasted_iota(jnp.int32, (1, LANES), 1)
    lo_half = lane < HEAD_DIM

    @pl.when(i == 0)
    def _():
        ones = jnp.ones((HEAD_DIM, tk), BF16)
        for jt in range(n_kt):
            rs = slice(jt * tk, (jt + 1) * tk)
            cos, sin = cosk_ref[rs, :], sink_ref[rs, :]
            for br, k_src, v_src in ((0, ks_ref, vs_ref), (1, kw_ref, vw_ref)):
                x = k_src[0, rs, :]
                x = x * cos + _rope_swap(x) * sin
                xr = pltpu.roll(x, HEAD_DIM, 1)
                k2_sc[br, 0, rs, :] = jnp.where(lo_half, x, xr).astype(BF16)
                k2_sc[br, 1, rs, :] = jnp.where(lo_half, xr, x).astype(BF16)
                v_t = v_src[0, rs, :].T.astype(BF16)
                for g in range(NSA_KV_GROUPS):
                    vt_sc[br, g, jt, 0:HEAD_DIM, :] = v_t[g * HEAD_DIM:(g + 1) * HEAD_DIM]
                    vt_sc[br, g, jt, HEAD_DIM:2 * HEAD_DIM, :] = ones

    tq_col = i * tq + lax.broadcasted_iota(jnp.int32, (tq, 1), 0)
    tq_row = i * tq + lax.broadcasted_iota(jnp.int32, (1, tq), 1)
    tq4 = jnp.concatenate([tq_col] * 4, axis=0)
    gates = _sigmoid(gn_ref[0])
    cosq, sinq = cosq_ref[...], sinq_ref[...]
    scale = HEAD_DIM ** -0.5
    nt_dims = (((1,), (1,)), ((), ()))

    def stack_heads(xa, xb):
        parts = [jnp.where(lo_half, xa, 0.0), jnp.where(lo_half, 0.0, xa),
                 jnp.where(lo_half, xb, 0.0), jnp.where(lo_half, 0.0, xb)]
        return (jnp.concatenate(parts, axis=0) * scale).astype(BF16)

    j_last = (i * tq + tq - 1) // tk
    j_first_win = jnp.maximum(i * tq - (WINDOW - 1), 0) // tk
    blk = lax.broadcasted_iota(jnp.int32, (n_sel, 1), 0)
    cur = tq_row // SEL_BLOCK
    forced = (blk == 0) | (blk == cur) | (blk == cur - 1)
    allowed = blk <= cur
    n_valid = (t - CMP_BLOCK) // CMP_STRIDE + 1
    cmask = (lane * CMP_STRIDE + (CMP_BLOCK - 1) <= tq4) & (lane < n_valid)

    o_cmp = []
    for g in range(NSA_KV_GROUPS):
        qa = q_ref[0, :, (2 * g) * LANES:(2 * g + 1) * LANES]
        qb = q_ref[0, :, (2 * g + 1) * LANES:(2 * g + 2) * LANES]
        q_cmp = stack_heads(qa, qb)
        qrot_sc[g] = stack_heads(qa * cosq + _rope_swap(qa) * sinq, qb * cosq + _rope_swap(qb) * sinq)

        s_c = lax.dot_general(q_cmp, kc_ref[g].astype(BF16), nt_dims, preferred_element_type=F32)
        p_c = _masked_softmax_parts(s_c, cmask)
        o_cmp.append(_mm(p_c, vc_ref[g]))

        p_sum = p_c[0:tq] + p_c[tq:2 * tq] + p_c[2 * tq:3 * tq] + p_c[3 * tq:4 * tq]
        imp = sum(lax.dot_general(ovt_ref[...], part, nt_dims, preferred_element_type=F32)
                  for part in _split3(p_sum))
        imp = jnp.where(forced, 1e4, jnp.where(allowed, imp, -1.0))
        rank = jnp.zeros((n_sel, tq), F32)
        for b in range(n_sel):
            row_b = imp[b:b + 1, :]
            rank = rank + jnp.where((row_b > imp) | ((row_b == imp) & (b < blk)), 1.0, 0.0)
        sel_t = jnp.where(rank < min(SEL_TOPK, n_sel), 1.0, 0.0).astype(BF16)
        for jt in range(n_kt):
            hit = jnp.dot(ext_ref[jt * tk:(jt + 1) * tk, :], sel_t, preferred_element_type=F32)
            kpos = jt * tk + lax.broadcasted_iota(jnp.int32, (tk, 1), 0)
            bias_sc[g, jt] = jnp.where((hit > 0.5) & (kpos <= tq_row), 0.0, NEG_BIG)

    for ch in range(2 * NSA_KV_GROUPS):
        m_sc[ch] = jnp.full(m_sc.shape[1:], NEG_BIG, F32)
        acc_sc[ch] = jnp.zeros(acc_sc.shape[1:], F32)

    def step(j, chains):
        r0 = pl.multiple_of(j * tk, tk)
        kpos = r0 + lax.broadcasted_iota(jnp.int32, (tk, 1), 0)
        wbias = jnp.where((kpos <= tq_row) & (kpos > tq_row - WINDOW), 0.0, NEG_BIG)
        scores = []
        for br, g in chains:
            s = lax.dot_general(k2_sc[br, g, pl.ds(r0, tk), :], qrot_sc[g], nt_dims,
                                preferred_element_type=F32)
            bias = bias_sc[g, j] if br == 0 else wbias
            scores.append(s + jnp.concatenate([bias] * 4, axis=1))
        probs, alphas = [], []
        for n, (br, g) in enumerate(chains):
            ch = br * NSA_KV_GROUPS + g
            m_old = m_sc[ch]
            m_new = jnp.maximum(m_old, jnp.max(scores[n], axis=0, keepdims=True))
            alphas.append(jnp.exp(m_old - m_new))
            probs.append(jnp.exp(scores[n] - m_new).astype(BF16))
            m_sc[ch] = m_new
        for n, (br, g) in enumerate(chains):
            ch = br * NSA_KV_GROUPS + g
            acc_sc[ch] = alphas[n] * acc_sc[ch] + jnp.dot(vt_sc[br, g, j], probs[n],
                                                          preferred_element_type=F32)

    sel_chains = [(0, g) for g in range(NSA_KV_GROUPS)]
    all_chains = sel_chains + [(1, g) for g in range(NSA_KV_GROUPS)]

    def sel_only(j, carry):
        step(j, sel_chains)
        return carry

    def sel_and_win(j, carry):
        step(j, all_chains)
        return carry

    lax.fori_loop(0, j_first_win, sel_only, 0)
    lax.fori_loop(j_first_win, j_last + 1, sel_and_win, 0)

    def finish(ch):
        acc = acc_sc[ch]
        out = acc[0:HEAD_DIM] / jnp.maximum(acc[HEAD_DIM:HEAD_DIM + 1], 1e-30)
        return jnp.where(m_sc[ch] > 0.5 * NEG_BIG, out, 0.0)

    for g in range(NSA_KV_GROUPS):
        o_sel, o_win = finish(g), finish(NSA_KV_GROUPS + g)
        for pr in range(2):
            he, ho = 2 * pr, 2 * pr + 1
            cols = lambda hh: slice(hh * tq, (hh + 1) * tq)
            pair = lambda o_t: jnp.concatenate([o_t[:, cols(he)], o_t[:, cols(ho)]], axis=0).T
            gate = lambda k: jnp.where(lo_half, gates[:, (4 * g + he) * 3 + k:(4 * g + he) * 3 + k + 1],
                                       gates[:, (4 * g + ho) * 3 + k:(4 * g + ho) * 3 + k + 1])
            o_c = jnp.where(lo_half, o_cmp[g][cols(he)], o_cmp[g][cols(ho)])
            o_ref[0, :, (2 * g + pr) * LANES:(2 * g + pr + 1) * LANES] = (
                gate(0) * o_c + gate(1) * pair(o_sel) + gate(2) * pair(o_win))


def _nsa_attn(q, gn, kv, kc2, vc2, cos, sin, ovt, ex):
    b, t, _ = q.shape
    tq, tk = NSA_TQ, NSA_TK
    g = NSA_KV_GROUPS
    assert t % tq == 0 and t % tk == 0
    full = lambda col: pl.BlockSpec((1, t, LANES), lambda bi, i: (bi, 0, col))
    return pl.pallas_call(
        _nsa_attn_kernel,
        grid=(b, t // tq),
        in_specs=[pl.BlockSpec((1, tq, NSA_WIDTH), lambda bi, i: (bi, i, 0)),
                  pl.BlockSpec((1, tq, LANES), lambda bi, i: (bi, i, 0)),
                  full(2), full(3), full(4), full(5),
                  pl.BlockSpec((g, kc2.shape[1], LANES), lambda bi, i: (bi, 0, 0)),
                  pl.BlockSpec((g, vc2.shape[1], LANES), lambda bi, i: (bi, 0, 0)),
                  pl.BlockSpec((tq, LANES), lambda bi, i: (i, 0)),
                  pl.BlockSpec((tq, LANES), lambda bi, i: (i, 0)),
                  _const_spec(cos.shape), _const_spec(sin.shape),
                  _const_spec(ovt.shape), _const_spec(ex.shape)],
        out_specs=pl.BlockSpec((1, tq, NSA_WIDTH), lambda bi, i: (bi, i, 0)),
        out_shape=jax.ShapeDtypeStruct((b, t, NSA_WIDTH), F32),
        scratch_shapes=[pltpu.VMEM((2, g, t, LANES), BF16),
                        pltpu.VMEM((2, g, t // tk, 2 * HEAD_DIM, tk), BF16),
                        pltpu.VMEM((g, 4 * tq, LANES), BF16),
                        pltpu.VMEM((g, t // tk, tk, tq), F32),
                        pltpu.VMEM((2 * g, 1, 4 * tq), F32),
                        pltpu.VMEM((2 * g, 2 * HEAD_DIM, 4 * tq), F32)],
        compiler_params=_cparams("parallel", "arbitrary"),
        name="nsa_attn",
    )(q, gn, kv, kv, kv, kv, kc2, vc2, cos, sin, cos, sin, ovt, ex)


def _merge_kernel(x_ref, yn_ref, bonus_ref, g_ref, lng_ref, lnb_ref, on_ref, gm_ref,
                  wrw_ref, wnsa_ref, wout_ref, o_ref):
    d = x_ref.shape[1]
    y_rw = (yn_ref[...] * lng_ref[...] + lnb_ref[...] + bonus_ref[...]) * g_ref[...]
    t_rw = _mm(y_rw, wrw_ref[...])
    t_ns = _mm(on_ref[...], wnsa_ref[...])
    gm = gm_ref[...]
    mix = _sigmoid(gm[:, 0:d]) * t_rw + _sigmoid(gm[:, d:2 * d]) * t_ns
    o_ref[...] = x_ref[...] + _mm(mix, wout_ref[...])


def _merge_out(x2, yn, bonus, g, ln_g, ln_b, o_nsa, gm, w_rw, w_nsa, w_out, tm):
    n, d = x2.shape
    row = lambda w: pl.BlockSpec((tm, w), lambda i: (i, 0))
    return pl.pallas_call(
        _merge_kernel,
        grid=(n // tm,),
        in_specs=[row(d), row(RW_WIDTH), row(RW_WIDTH), row(RW_WIDTH),
                  _const_spec(ln_g.shape), _const_spec(ln_b.shape),
                  row(NSA_WIDTH), row(2 * d),
                  _const_spec(w_rw.shape), _const_spec(w_nsa.shape), _const_spec(w_out.shape)],
        out_specs=row(d),
        out_shape=jax.ShapeDtypeStruct((n, d), F32),
        compiler_params=_cparams("parallel"),
        name="merge_out",
    )(x2, yn, bonus, g, ln_g, ln_b, o_nsa, gm, w_rw, w_nsa, w_out)


def _mem_kv_kernel(m_ref, g_ref, w_ref, o_ref):
    o_ref[0] = _mm(_rms(m_ref[0], g_ref[...]), w_ref[...]).astype(BF16)


def _mem_kv(mem, g, wkv):
    b, m, d = mem.shape
    return pl.pallas_call(
        _mem_kv_kernel,
        grid=(b,),
        in_specs=[pl.BlockSpec((1, m, d), lambda bi: (bi, 0, 0)), _const_spec(g.shape),
                  _const_spec(wkv.shape)],
        out_specs=pl.BlockSpec((1, m, 2 * d), lambda bi: (bi, 0, 0)),
        out_shape=jax.ShapeDtypeStruct((b, m, 2 * d), BF16),
        compiler_params=_cparams("parallel"),
        name="mem_kv",
    )(mem, g, wkv)


def _xattn_kernel(h_ref, g_ref, wq_ref, kv_ref, wo_ref, o_ref):
    d = h_ref.shape[2]
    hd = d // XA_HEADS
    h = h_ref[0]
    q = (_mm(_rms(h, g_ref[...]), wq_ref[...]) * hd ** -0.5).astype(BF16)
    outs = []
    for hh in range(XA_HEADS):
        cs = slice(hh * hd, (hh + 1) * hd)
        s = lax.dot_general(q[:, cs], kv_ref[0, :, cs], (((1,), (1,)), ((), ())),
                            preferred_element_type=F32)
        e = jnp.exp(s - jnp.max(s, axis=-1, keepdims=True))
        p = e / jnp.sum(e, axis=-1, keepdims=True)
        outs.append(jnp.dot(p.astype(BF16), kv_ref[0, :, d + hh * hd:d + (hh + 1) * hd],
                            preferred_element_type=F32))
    o_ref[0] = h + _mm(jnp.concatenate(outs, axis=1), wo_ref[...])


def _xattn(h3, g, wq, kv, wo, tm):
    b, t, d = h3.shape
    m = kv.shape[1]
    return pl.pallas_call(
        _xattn_kernel,
        grid=(b, t // tm),
        in_specs=[pl.BlockSpec((1, tm, d), lambda bi, i: (bi, i, 0)), _const_spec(g.shape),
                  _const_spec(wq.shape),
                  pl.BlockSpec((1, m, 2 * d), lambda bi, i: (bi, 0, 0)),
                  _const_spec(wo.shape)],
        out_specs=pl.BlockSpec((1, tm, d), lambda bi, i: (bi, i, 0)),
        out_shape=jax.ShapeDtypeStruct((b, t, d), F32),
        compiler_params=_cparams("parallel", "parallel"),
        name="xattn",
    )(h3, g, wq, kv, wo)


FFN_CHUNK = 256


def _ffn_kernel(h_ref, g_ref, wgu_ref, wd_ref, gf_ref, o_ref):
    h = h_ref[...]
    hn = _rms(h, g_ref[...]).astype(BF16)
    dff = wd_ref.shape[0]
    acc = jnp.zeros(h.shape, F32)
    for c0 in range(0, dff, FFN_CHUNK):
        gate = jnp.dot(hn, wgu_ref[:, c0:c0 + FFN_CHUNK], preferred_element_type=F32)
        up = jnp.dot(hn, wgu_ref[:, dff + c0:dff + c0 + FFN_CHUNK], preferred_element_type=F32)
        act = (gate * _sigmoid(gate) * up).astype(BF16)
        acc = acc + jnp.dot(act, wd_ref[c0:c0 + FFN_CHUNK, :], preferred_element_type=F32)
    o_ref[...] = _rms(h + acc, gf_ref[...])


def _ffn(h2, g, wgu, wd, gf, tm):
    n, d = h2.shape
    assert wd.shape[0] % FFN_CHUNK == 0
    return pl.pallas_call(
        _ffn_kernel,
        grid=(n // tm,),
        in_specs=[pl.BlockSpec((tm, d), lambda i: (i, 0)), _const_spec(g.shape),
                  _const_spec(wgu.shape), _const_spec(wd.shape), _const_spec(gf.shape)],
        out_specs=pl.BlockSpec((tm, d), lambda i: (i, 0)),
        out_shape=jax.ShapeDtypeStruct((n, d), F32),
        compiler_params=_cparams("parallel"),
        name="ffn",
    )(h2, g, wgu, wd, gf)


def _head_sum_matrix():
    idx = np.arange(LANES) // HEAD_DIM
    return jnp.asarray(idx[:, None] == idx[None, :], BF16)


def _prefix_matrix():
    L = RW_CHUNK
    tri = np.tril(np.ones((L, L), np.float32))
    return jnp.asarray(np.concatenate([tri, np.ones((L, L), np.float32)], axis=0), BF16)


def _rope_tables(t):
    half = HEAD_DIM // 2
    inv_freq = ROPE_THETA ** (-jnp.arange(half, dtype=F32) / half)
    ang = jnp.arange(t, dtype=F32)[:, None] * inv_freq[None, :]
    cos, sin = jnp.cos(ang), jnp.sin(ang)
    cos128 = jnp.tile(cos, (1, LANES // half))
    sin128 = jnp.tile(jnp.concatenate([-sin, sin], axis=1), (1, LANES // HEAD_DIM))
    return cos128, sin128


def _overlap_matrix(t):
    n_cmp = (t - CMP_BLOCK) // CMP_STRIDE + 1
    n_sel = t // SEL_BLOCK
    cs = np.arange(n_cmp) * CMP_STRIDE
    ss = np.arange(n_sel) * SEL_BLOCK
    ov = np.clip(np.minimum(cs[:, None] + CMP_BLOCK, ss[None, :] + SEL_BLOCK)
                 - np.maximum(cs[:, None], ss[None, :]), 0, None) / CMP_BLOCK
    out = np.zeros((n_sel, LANES), np.float32)
    out[:, :n_cmp] = ov.T
    return jnp.asarray(out, BF16)


def _expand_matrix(t):
    blk = np.arange(t) // SEL_BLOCK
    return jnp.asarray(blk[:, None] == np.arange(t // SEL_BLOCK)[None, :], BF16)


def kernel(x, mem, norm_mix_g, w_in, shift_mu, rw_w_up, rw_w0, rw_a_up, rw_a0, rw_g_up, rw_k_k, rw_k_a,
           rw_r_k, rw_ln_g, rw_ln_b, nsa_pe_k, nsa_pe_v, nsa_ck1, nsa_ck2, nsa_cv1, nsa_cv2, w_up_rw,
           w_up_nsa, w_out, norm_xa_g, norm_mem_g, xa_wq, xa_wkv, xa_wo, norm_ffn_g, ffn_w_gu,
           ffn_w_down, final_norm_g):
    b, t, d = x.shape
    n = b * t
    h = x.reshape(n, d)
    hsum = _head_sum_matrix()
    tri = _prefix_matrix()
    cos, sin = _rope_tables(t)
    ov = _overlap_matrix(t)
    ex = _expand_matrix(t)
    row = lambda a: a.reshape(1, -1)
    n_half = t // CMP_STRIDE
    assert t // SEL_BLOCK <= LANES and n_half <= LANES

    for l in range(w_in.shape[0]):
        c_g = RW_IN + NSA_WIDTH + 6 * KV_WIDTH
        n_gate = 3 * NSA_HEADS
        w_pad = jnp.concatenate(
            [w_in[l][:, :c_g + n_gate], jnp.zeros((d, LANES - n_gate), F32), w_in[l][:, c_g + n_gate:]],
            axis=1).astype(BF16)
        p_rw, q, kv, gn, gm = _in_proj(h, row(norm_mix_g[l]), w_pad, 256)

        z64 = jnp.zeros((DECAY_LORA, RW_WIDTH), F32)
        wlora = jnp.concatenate([jnp.concatenate([rw_w_up[l], z64], axis=1),
                                 jnp.concatenate([z64, rw_a_up[l]], axis=1)], axis=0).astype(BF16)
        prep = _rwkv_prep(p_rw.reshape(b, t, RW_IN), row(shift_mu[l]), wlora, row(rw_w0[l]),
                          row(rw_a0[l]), rw_g_up[l].astype(BF16), row(rw_k_k[l]), row(rw_k_a[l]),
                          row(rw_r_k[l]), tri, hsum, 512)
        abar, rbar, btil, ktil, bhat, khat, v_rw, gam, g_rw, bonus = prep
        yn = _rwkv_scan(abar, rbar, btil, ktil, bhat, khat, v_rw, gam)

        def halves(cols):
            a = kv[:, cols * LANES:(cols + 1) * LANES].reshape(b, t, NSA_KV_GROUPS, HEAD_DIM)
            return a.transpose(0, 2, 1, 3).reshape(b * NSA_KV_GROUPS, n_half, CMP_STRIDE * HEAD_DIM)

        def w1_halves(w1):
            hw = CMP_STRIDE * HEAD_DIM
            return jnp.concatenate([w1[:hw], w1[hw:]], axis=1).astype(BF16)

        dup = lambda w2: jnp.concatenate([w2, w2], axis=1).astype(BF16)
        kc2, vc2 = _nsa_cmp(halves(0), halves(1), nsa_pe_k[l].reshape(1, -1), nsa_pe_v[l].reshape(1, -1),
                            w1_halves(nsa_ck1[l]), nsa_ck1[l].astype(BF16), dup(nsa_ck2[l]),
                            w1_halves(nsa_cv1[l]), nsa_cv1[l].astype(BF16), dup(nsa_cv2[l]), 4)
        o_nsa = _nsa_attn(q.reshape(b, t, NSA_WIDTH), gn.reshape(b, t, LANES),
                          kv.reshape(b, t, 6 * KV_WIDTH), kc2, vc2, cos, sin, ov, ex)

        h = _merge_out(h, yn.reshape(n, RW_WIDTH), bonus.reshape(n, RW_WIDTH), g_rw.reshape(n, RW_WIDTH),
                       row(rw_ln_g[l]), row(rw_ln_b[l]), o_nsa.reshape(n, NSA_WIDTH), gm,
                       w_up_rw[l].astype(BF16), w_up_nsa[l].astype(BF16), w_out[l].astype(BF16), 512)

        mkv = _mem_kv(mem, row(norm_mem_g[l]), xa_wkv[l].astype(BF16))
        h = _xattn(h.reshape(b, t, d), row(norm_xa_g[l]), xa_wq[l].astype(BF16), mkv,
                   xa_wo[l].astype(BF16), 512).reshape(n, d)

        last = l == w_in.shape[0] - 1
        gf = row(final_norm_g) if last else None
        assert last, "only a single layer is fused with the final norm"
        h = _ffn(h, row(norm_ffn_g[l]), ffn_w_gu[l].astype(BF16), ffn_w_down[l].astype(BF16), gf, 512)
    return h.reshape(b, t, d)
```

```python
import functools
import math

import numpy as np
import jax
import jax.numpy as jnp
from jax import lax
from jax.experimental import pallas as pl
from jax.experimental.pallas import tpu as pltpu

F32 = jnp.float32
BF16 = jnp.bfloat16

HEAD_DIM = 64
NORM_EPS = 1e-6
ROPE_THETA = 10000.0
RW_HEADS = 8
RW_WIDTH = RW_HEADS * HEAD_DIM
DECAY_LORA = 64
AAA_LORA = 64
GATE_LORA = 128
RW_GN_EPS = 64e-5
RW_IN = 3 * RW_WIDTH + DECAY_LORA + AAA_LORA + GATE_LORA
NSA_HEADS = 8
NSA_KV_GROUPS = 2
NSA_WIDTH = NSA_HEADS * HEAD_DIM
KV_WIDTH = NSA_KV_GROUPS * HEAD_DIM
CMP_BLOCK = 32
CMP_STRIDE = 16
CMP_HIDDEN = 128
SEL_BLOCK = 64
SEL_TOPK = 8
WINDOW = 512
XA_HEADS = 4

LANES = 128
VMEM_LIMIT = 56 * 1024 * 1024
RW_CHUNK = 64
NEG_BIG = -1e30


def _cparams(*sem):
    return pltpu.CompilerParams(dimension_semantics=sem, vmem_limit_bytes=VMEM_LIMIT)


def _mm(a, b):
    return jnp.dot(a.astype(BF16), b.astype(BF16), preferred_element_type=F32)


def _mm_nt(a, b):
    return lax.dot_general(a.astype(BF16), b.astype(BF16), (((1,), (1,)), ((), ())),
                           preferred_element_type=F32)


def _mm_tn(a, b):
    return lax.dot_general(a.astype(BF16), b.astype(BF16), (((0,), (0,)), ((), ())),
                           preferred_element_type=F32)


def _split3(x):
    hi = x.astype(BF16)
    r1 = x - hi.astype(F32)
    mid = r1.astype(BF16)
    lo = (r1 - mid.astype(F32)).astype(BF16)
    return hi, mid, lo


def _mm_exact_rhs(x, m_bf16):
    hi, mid, lo = _split3(x)
    f = lambda p: jnp.dot(p, m_bf16, preferred_element_type=F32)
    return f(hi) + f(mid) + f(lo)


def _head_sums(x, pair_ones):
    return jnp.concatenate([_mm_exact_rhs(x[:, c:c + LANES], pair_ones)
                            for c in range(0, x.shape[1], LANES)], axis=1)


def _mm_exact_lhs(m_bf16, x):
    hi, mid, lo = _split3(x)
    f = lambda p: jnp.dot(m_bf16, p, preferred_element_type=F32)
    return f(hi) + f(mid) + f(lo)


def _rms(x, g):
    return x * lax.rsqrt(jnp.mean(x * x, axis=-1, keepdims=True) + NORM_EPS) * g


def _sigmoid(x):
    return 1.0 / (1.0 + jnp.exp(-x))


def _const_spec(shape):
    nd = len(shape)
    return pl.BlockSpec(shape, lambda *_: (0,) * nd)


def _in_proj_kernel(x_ref, g_ref, w_ref, prw_ref, q_ref, kv_ref, gn_ref, gm_ref):
    hn = _rms(x_ref[...], g_ref[...]).astype(BF16)
    off = 0
    for o_ref in (prw_ref, q_ref, kv_ref, gn_ref, gm_ref):
        width = o_ref.shape[1]
        for c0 in range(0, width, 512):
            c1 = min(c0 + 512, width)
            o_ref[:, c0:c1] = jnp.dot(hn, w_ref[:, off + c0:off + c1],
                                      preferred_element_type=F32).astype(o_ref.dtype)
        off += width


def _in_proj(x2, g, w_pad, tm):
    n, d = x2.shape
    widths = (RW_IN, NSA_WIDTH, 6 * KV_WIDTH, LANES, 2 * d)
    assert sum(widths) == w_pad.shape[1] and n % tm == 0
    return pl.pallas_call(
        _in_proj_kernel,
        grid=(n // tm,),
        in_specs=[pl.BlockSpec((tm, d), lambda i: (i, 0)),
                  _const_spec((1, d)),
                  _const_spec(w_pad.shape)],
        out_specs=[pl.BlockSpec((tm, w), lambda i: (i, 0)) for w in widths],
        out_shape=[jax.ShapeDtypeStruct((n, w), dt) for w, dt in zip(widths, (F32, BF16, BF16, F32, BF16))],
        compiler_params=_cparams("parallel"),
        name="in_proj",
    )(x2, g, w_pad)


def _rwkv_prep_kernel(p_ref, prev_ref, mu_ref, wlora_ref, w0_ref, a0_ref, gup_ref, kk_ref, ka_ref,
                      rk_ref, tri_ref, hsum_ref,
                      abar_ref, rbar_ref, btil_ref, ktil_ref, bhat_ref, khat_ref, v_ref, gam_ref,
                      g_ref, bonus_ref):
    i = pl.program_id(1)
    tt = p_ref.shape[1]
    w = RW_WIDTH
    cur = p_ref[0]
    prev_row = jnp.where(i == 0, 0.0, prev_ref[0, 7:8, :])
    row = lax.broadcasted_iota(jnp.int32, (tt, 1), 0)
    prev = jnp.where(row == 0, prev_row, pltpu.roll(cur, 1, 0))
    p = cur + mu_ref[...] * (prev - cur)

    r = p[:, 0:w]
    k = p[:, w:2 * w]
    v = p[:, 2 * w:3 * w]
    x_wa = p[:, 3 * w:3 * w + LANES]
    x_g = p[:, 3 * w + LANES:3 * w + 2 * LANES]
    lane = lax.broadcasted_iota(jnp.int32, (1, LANES), 1)
    x_wa = jnp.where(lane < DECAY_LORA, jnp.tanh(x_wa), x_wa)
    lin = _mm(x_wa, wlora_ref[...])
    w_lin = w0_ref[...] + lin[:, 0:w]
    a = _sigmoid(a0_ref[...] + lin[:, w:2 * w])
    g_ref[0] = _mm(_sigmoid(x_g), gup_ref[...]).astype(BF16)

    z = -w_lin
    softplus = jnp.maximum(z, 0.0) + jnp.log(1.0 + jnp.exp(-jnp.abs(z)))
    logw = -jnp.exp(-softplus - 0.5)

    hsum = hsum_ref[...]
    kk = k * kk_ref[...]
    kk = kk * lax.rsqrt(jnp.maximum(_head_sums(kk * kk, hsum), 1e-12))
    k2 = k * (1.0 + (a - 1.0) * ka_ref[...])
    bonus_ref[0] = (_head_sums(r * k2 * rk_ref[...], hsum) * v).astype(BF16)
    v_ref[0] = v.astype(BF16)

    L = RW_CHUNK
    tri = tri_ref[...]
    kka = kk * a
    gam_rows = []
    for c in range(tt // L):
        sl = slice(c * L, (c + 1) * L)
        cs = _mm_exact_lhs(tri, logw[sl])
        cum, tot = cs[0:L], cs[L:2 * L]
        e_neg = jnp.exp(-cum)
        e_end = jnp.exp(tot - cum)
        abar_ref[0, sl, :] = (-kk[sl] * jnp.exp(cum - logw[sl])).astype(BF16)
        rbar_ref[0, sl, :] = (r[sl] * jnp.exp(cum)).astype(BF16)
        btil_ref[0, sl, :] = (kka[sl] * e_neg).astype(BF16)
        ktil_ref[0, sl, :] = (k2[sl] * e_neg).astype(BF16)
        bhat_ref[0, sl, :] = (kka[sl] * e_end).astype(BF16)
        khat_ref[0, sl, :] = (k2[sl] * e_end).astype(BF16)
        gam_rows.append(jnp.exp(tot[0:1]))
    gam_ref[0] = jnp.concatenate(gam_rows, axis=0)


def _rwkv_prep(p_rw, mu, wlora, w0, a0, gup, k_k, k_a, r_k, tri, hsum, tt):
    b, t, _ = p_rw.shape
    w = RW_WIDTH
    nck = tt // RW_CHUNK
    assert t % tt == 0 and nck % 8 == 0
    tile = lambda: pl.BlockSpec((1, tt, w), lambda bi, i: (bi, i, 0))
    seq_bf = jax.ShapeDtypeStruct((b, t, w), BF16)
    return pl.pallas_call(
        _rwkv_prep_kernel,
        grid=(b, t // tt),
        in_specs=[pl.BlockSpec((1, tt, RW_IN), lambda bi, i: (bi, i, 0)),
                  pl.BlockSpec((1, 8, RW_IN), lambda bi, i: (bi, jnp.maximum(i * (tt // 8) - 1, 0), 0)),
                  _const_spec(mu.shape), _const_spec(wlora.shape), _const_spec(w0.shape),
                  _const_spec(a0.shape), _const_spec(gup.shape), _const_spec(k_k.shape),
                  _const_spec(k_a.shape), _const_spec(r_k.shape), _const_spec(tri.shape),
                  _const_spec(hsum.shape)],
        out_specs=[tile(), tile(), tile(), tile(), tile(), tile(), tile(),
                   pl.BlockSpec((1, nck, w), lambda bi, i: (bi, i, 0)),
                   tile(), tile()],
        out_shape=[seq_bf] * 7 + [jax.ShapeDtypeStruct((b, t // RW_CHUNK, w), F32), seq_bf, seq_bf],
        compiler_params=_cparams("parallel", "parallel"),
        name="rwkv_prep",
    )(p_rw, p_rw, mu, wlora, w0, a0, gup, k_k, k_a, r_k, tri, hsum)


def _rwkv_scan_kernel(abar_ref, rbar_ref, btil_ref, ktil_ref, bhat_ref, khat_ref, v_ref, gam_ref,
                      y_ref, s_ref):
    c = pl.program_id(1)
    L = RW_CHUNK

    @pl.when(c == 0)
    def _():
        s_ref[...] = jnp.zeros_like(s_ref)

    lane = lax.broadcasted_iota(jnp.int32, (1, LANES), 1)
    m0 = lane < HEAD_DIM
    ri = lax.broadcasted_iota(jnp.int32, (2 * L, 2 * L), 0)
    ci = lax.broadcasted_iota(jnp.int32, (2 * L, 2 * L), 1)
    same = (ri // L) == (ci // L)
    low_strict = same & (ci < ri)
    low_incl = same & (ci <= ri)
    zero = jnp.zeros((), BF16)
    nb = abar_ref.shape[0]
    n_pair = RW_HEADS // 2

    def stack(ref, bi, ls):
        x = ref[bi, :, ls]
        return jnp.concatenate([jnp.where(m0, x, zero), jnp.where(m0, zero, x)], axis=0)

    chains = [(bi, pr) for bi in range(nb) for pr in range(n_pair)]
    nch = len(chains)
    lanes_of = lambda pr: slice(pr * LANES, (pr + 1) * LANES)

    x_all = [jnp.concatenate([stack(abar_ref, bi, lanes_of(pr)), stack(rbar_ref, bi, lanes_of(pr))], axis=0)
             for bi, pr in chains]
    vs = [stack(v_ref, bi, lanes_of(pr)) for bi, pr in chains]
    s0 = [s_ref[bi * n_pair + pr] for bi, pr in chains]
    amat = [_mm_nt(x_all[i], jnp.concatenate([stack(btil_ref, bi, lanes_of(pr)),
                                              stack(ktil_ref, bi, lanes_of(pr))], axis=0))
            for i, (bi, pr) in enumerate(chains)]
    xs = [_mm_nt(x_all[i], s0[i]) for i in range(nch)]

    u = [xs[i][0:2 * L] + _mm(jnp.where(low_strict, amat[i][0:2 * L, 2 * L:4 * L], 0.0), vs[i])
         for i in range(nch)]
    pw = [jnp.where(low_strict, a[0:2 * L, 0:2 * L], 0.0).astype(BF16) for a in amat]
    n_sq = int(math.log2(L))
    for step in range(n_sq):
        u = [u[i] + _mm(pw[i], u[i]) for i in range(nch)]
        if step + 1 < n_sq:
            pw = [_mm(q, q).astype(BF16) for q in pw]

    uv = [jnp.concatenate([u[i].astype(BF16), vs[i]], axis=0) for i in range(nch)]
    gam_rows = [gam_ref[bi, pl.ds(c % 8, 1), :] for bi in range(nb)]
    for i, (bi, pr) in enumerate(chains):
        gam = gam_rows[bi][:, lanes_of(pr)]
        bk_hat = jnp.concatenate([stack(bhat_ref, bi, lanes_of(pr)), stack(khat_ref, bi, lanes_of(pr))],
                                 axis=0)
        s_ref[bi * n_pair + pr] = s0[i] * gam + _mm_tn(uv[i], bk_hat)
    def head_mean(z):
        lo = jnp.sum(jnp.where(m0, z, 0.0), axis=-1, keepdims=True)
        hi = jnp.sum(jnp.where(m0, 0.0, z), axis=-1, keepdims=True)
        return jnp.where(m0, lo, hi) * (1.0 / HEAD_DIM)

    for i, (bi, pr) in enumerate(chains):
        a_r =jnp.concatenate([jnp.where(low_incl, amat[i][2 * L:4 * L, 0:2 * L], 0.0),
                               jnp.where(low_incl, amat[i][2 * L:4 * L, 2 * L:4 * L], 0.0)], axis=1)
        y2 = xs[i][2 * L:4 * L] + _mm(a_r, uv[i])
        y = y2[0:L] + y2[L:2 * L]
        d = y - head_mean(y)
        y_ref[bi, :, lanes_of(pr)] = (d * lax.rsqrt(head_mean(d * d) + RW_GN_EPS)).astype(BF16)


RW_SCAN_BATCH = 4


def _rwkv_scan(abar, rbar, btil, ktil, bhat, khat, v, gam):
    b, t, w = abar.shape
    L = RW_CHUNK
    nb = RW_SCAN_BATCH
    assert b % nb == 0
    tile = lambda: pl.BlockSpec((nb, L, w), lambda bi, c: (bi, c, 0))
    return pl.pallas_call(
        _rwkv_scan_kernel,
        grid=(b // nb, t // L),
        in_specs=[tile(), tile(), tile(), tile(), tile(), tile(), tile(),
                  pl.BlockSpec((nb, 8, w), lambda bi, c: (bi, c // 8, 0))],
        out_specs=tile(),
        out_shape=jax.ShapeDtypeStruct((b, t, w), BF16),
        scratch_shapes=[pltpu.VMEM((nb * (RW_HEADS // 2), LANES, LANES), F32)],
        compiler_params=_cparams("parallel", "arbitrary"),
        name="rwkv_scan",
    )(abar, rbar, btil, ktil, bhat, khat, v, gam)


def _gelu_tanh(x):
    return 0.5 * x * (1.0 + jnp.tanh(math.sqrt(2.0 / math.pi) * (x + 0.044715 * x * x * x)))


def _nsa_cmp_kernel(kh_ref, vh_ref, pek_ref, pev_ref, k1_ref, k1f_ref, k2_ref, v1_ref, v1f_ref, v2_ref,
                    ko_ref, vo_ref):
    r, nh, wid = kh_ref.shape
    for h_ref, pe_ref, w1_ref, w1f_ref, w2_ref, o_ref in (
            (kh_ref, pek_ref, k1_ref, k1f_ref, k2_ref, ko_ref),
            (vh_ref, pev_ref, v1_ref, v1f_ref, v2_ref, vo_ref)):
        halves = h_ref[...].reshape(r * nh, wid)
        z = _mm(halves, w1_ref[...])
        bias = _mm(jnp.broadcast_to(pe_ref[...], (8, pe_ref.shape[1])), w1f_ref[...])[0:1]
        pre = z[:, 0:CMP_HIDDEN] + pltpu.roll(z[:, CMP_HIDDEN:], r * nh - 1, 0) + bias
        o_ref[...] = _mm(_gelu_tanh(pre), w2_ref[...]).reshape(r, nh, LANES)


def _nsa_cmp(k_halves, v_halves, pe_k, pe_v, k1, k1f, k2d, v1, v1f, v2d, r):
    n, nh, wid = k_halves.shape
    assert n % r == 0
    tile_in = lambda: pl.BlockSpec((r, nh, wid), lambda i: (i, 0, 0))
    tile_out = lambda: pl.BlockSpec((r, nh, LANES), lambda i: (i, 0, 0))
    out = jax.ShapeDtypeStruct((n, nh, LANES), F32)
    consts = (pe_k, pe_v, k1, k1f, k2d, v1, v1f, v2d)
    return pl.pallas_call(
        _nsa_cmp_kernel,
        grid=(n // r,),
        in_specs=[tile_in(), tile_in()] + [_const_spec(c.shape) for c in consts],
        out_specs=[tile_out(), tile_out()],
        out_shape=[out, out],
        compiler_params=_cparams("parallel"),
        name="nsa_cmp",
    )(k_halves, v_halves, pe_k, pe_v, k1, k1f, k2d, v1, v1f, v2d)


NSA_TQ = 256
NSA_TK = 256
NSA_LOOKAHEAD = 1


def _rope_swap(x):
    lane = lax.broadcasted_iota(jnp.int32, (1, LANES), 1)
    first = (lane % HEAD_DIM) < (HEAD_DIM // 2)
    return jnp.where(first, pltpu.roll(x, LANES - HEAD_DIM // 2, 1), pltpu.roll(x, HEAD_DIM // 2, 1))


def _masked_softmax_parts(s, mask):
    s = jnp.where(mask, s, NEG_BIG)
    e = jnp.where(mask, jnp.exp(s - jnp.max(s, axis=-1, keepdims=True)), 0.0)
    return e / jnp.maximum(jnp.sum(e, axis=-1, keepdims=True), 1e-30)


NSA_ACC_ROWS = HEAD_DIM + 16
LOG2E = 1.4426950408889634


def _nsa_attn_kernel(q_ref, gn_ref, ks_ref, vs_ref, kw_ref, vw_ref, kc_ref, vc_ref,
                     cosq_ref, sinq_ref, cosk_ref, sink_ref, ovt_ref, ext_ref,
                     o_ref,
                     k2_sc, vt_sc, qrot_sc, sel_sc, m_sc, acc_sc):
    i = pl.program_id(1)
    tq, tk = NSA_TQ, NSA_TK
    t = ks_ref.shape[1]
    n_kt = t // tk
    n_sel = t // SEL_BLOCK
    lane = lax.broadcasted_iota(jnp.int32, (1, LANES), 1)
    lo_half = lane < HEAD_DIM

    @pl.when(i == 0)
    def _():
        ones = jnp.ones((NSA_ACC_ROWS - HEAD_DIM, tk), BF16)
        for jt in range(n_kt):
            rs = slice(jt * tk, (jt + 1) * tk)
            cos, sin = cosk_ref[rs, :], sink_ref[rs, :]
            for br, k_src, v_src in ((0, ks_ref, vs_ref), (1, kw_ref, vw_ref)):
                x = k_src[0, rs, :].astype(F32)
                x = x * cos + _rope_swap(x) * sin
                xr = pltpu.roll(x, HEAD_DIM, 1)
                k2_sc[br, 0, rs, :] = jnp.where(lo_half, x, xr).astype(BF16)
                k2_sc[br, 1, rs, :] = jnp.where(lo_half, xr, x).astype(BF16)
                v_t = v_src[0, rs, :].astype(F32).T.astype(BF16)
                for g in range(NSA_KV_GROUPS):
                    vt_sc[br, g, jt, 0:HEAD_DIM, :] = v_t[g * HEAD_DIM:(g + 1) * HEAD_DIM]
                    vt_sc[br, g, jt, HEAD_DIM:NSA_ACC_ROWS, :] = ones

    tq_row = i * tq + lax.broadcasted_iota(jnp.int32, (1, tq), 1)
    tq4_row = jnp.concatenate([tq_row] * 4, axis=1)
    gates_t = _sigmoid(gn_ref[0]).T
    cosq, sinq = cosq_ref[...], sinq_ref[...]
    qscale = HEAD_DIM ** -0.5 * LOG2E
    nt_dims = (((1,), (1,)), ((), ()))
    cols = lambda hh: slice(hh * tq, (hh + 1) * tq)

    def stack_heads(xa, xb):
        parts = [jnp.where(lo_half, xa, 0.0), jnp.where(lo_half, 0.0, xa),
                 jnp.where(lo_half, xb, 0.0), jnp.where(lo_half, 0.0, xb)]
        return (jnp.concatenate(parts, axis=0) * qscale).astype(BF16)

    j_last = (i * tq + tq - 1) // tk
    j_first_win = jnp.maximum(i * tq - (WINDOW - 1), 0) // tk
    blk = lax.broadcasted_iota(jnp.int32, (n_sel, 1), 0)
    cur = tq_row // SEL_BLOCK
    forced = (blk == 0) | (blk == cur) | (blk == cur - 1)
    allowed = blk <= cur
    n_valid = (t - CMP_BLOCK) // CMP_STRIDE + 1
    cblk = lax.broadcasted_iota(jnp.int32, (LANES, 1), 0)
    cmask = (cblk * CMP_STRIDE + (CMP_BLOCK - 1) <= tq4_row) & (cblk < n_valid)

    o_cmp = []
    for g in range(NSA_KV_GROUPS):
        qa = q_ref[0, :, (2 * g) * LANES:(2 * g + 1) * LANES].astype(F32)
        qb = q_ref[0, :, (2 * g + 1) * LANES:(2 * g + 2) * LANES].astype(F32)
        q_cmp = stack_heads(qa, qb)
        qrot_sc[g] = stack_heads(qa * cosq + _rope_swap(qa) * sinq, qb * cosq + _rope_swap(qb) * sinq)

        s_c = lax.dot_general(kc_ref[g].astype(BF16), q_cmp, nt_dims, preferred_element_type=F32)
        s_c = jnp.where(cmask, s_c, NEG_BIG)
        e_c = jnp.where(cmask, jnp.exp2(s_c - jnp.max(s_c, axis=0, keepdims=True)), 0.0)
        p_c = e_c / jnp.maximum(jnp.sum(e_c, axis=0, keepdims=True), 1e-30)
        o_cmp.append(_mm(vc_ref[g].T[0:HEAD_DIM], p_c))

        p_sum = p_c[:, cols(0)] + p_c[:, cols(1)] + p_c[:, cols(2)] + p_c[:, cols(3)]
        imp = sum(jnp.dot(ovt_ref[...], part, preferred_element_type=F32) for part in _split3(p_sum))
        imp = jnp.where(forced, 1e4, jnp.where(allowed, imp, -1.0))
        rank = jnp.zeros((n_sel, tq), F32)
        for b in range(n_sel):
            row_b = imp[b:b + 1, :]
            rank = rank + jnp.where((row_b > imp) | ((row_b == imp) & (b < blk)), 1.0, 0.0)
        sel_sc[g] = jnp.where(rank < min(SEL_TOPK, n_sel), 1.0, 0.0).astype(BF16)

    for ch in range(2 * NSA_KV_GROUPS):
        m_sc[ch] = jnp.full(m_sc.shape[1:], NEG_BIG, F32)
        acc_sc[ch] = jnp.zeros(acc_sc.shape[1:], F32)

    def step(j, chains):
        r0 = pl.multiple_of(j * tk, tk)
        kpos = r0 + lax.broadcasted_iota(jnp.int32, (tk, 1), 0)
        causal = kpos <= tq_row

        def scores(br, g):
            s = lax.dot_general(k2_sc[br, g, pl.ds(r0, tk), :], qrot_sc[g], nt_dims,
                                preferred_element_type=F32)
            if br == 0:
                hit = jnp.dot(ext_ref[pl.ds(r0, tk), :], sel_sc[g], preferred_element_type=F32)
                keep = causal & (hit > 0.5)
            else:
                keep = causal & (kpos > tq_row - WINDOW)
            return s + jnp.concatenate([jnp.where(keep, 0.0, NEG_BIG)] * 4, axis=1)

        pending = [scores(*c) for c in chains[:NSA_LOOKAHEAD]]
        for n, (br, g) in enumerate(chains):
            if n + NSA_LOOKAHEAD < len(chains):
                pending.append(scores(*chains[n + NSA_LOOKAHEAD]))
            s_cur = pending.pop(0)
            ch = br * NSA_KV_GROUPS + g
            m_old = m_sc[ch]
            m_new = jnp.maximum(m_old, jnp.max(s_cur, axis=0, keepdims=True))
            p = jnp.exp2(s_cur - m_new).astype(BF16)
            m_sc[ch] = m_new
            acc_sc[ch] = jnp.exp2(m_old - m_new) * acc_sc[ch] + jnp.dot(
                vt_sc[br, g, j], p, preferred_element_type=F32)

    sel_chains = [(0, g) for g in range(NSA_KV_GROUPS)]
    all_chains = [(0, 0), (1, 0), (0, 1), (1, 1)]

    def sel_only(j, carry):
        step(j, sel_chains)
        return carry

    def sel_and_win(j, carry):
        step(j, all_chains)
        return carry

    lax.fori_loop(0, j_first_win, sel_only, 0)
    lax.fori_loop(j_first_win, j_last + 1, sel_and_win, 0)

    def finish(ch):
        acc = acc_sc[ch]
        out = acc[0:HEAD_DIM] / jnp.maximum(acc[HEAD_DIM:HEAD_DIM + 1], 1e-30)
        return jnp.where(m_sc[ch] > 0.5 * NEG_BIG, out, 0.0)

    for g in range(NSA_KV_GROUPS):
        o_sel, o_win = finish(g), finish(NSA_KV_GROUPS + g)

        def head_out(hh):
            r = (4 * g + hh) * 3
            return (gates_t[r:r + 1] * o_cmp[g][:, cols(hh)] + gates_t[r + 1:r + 2] * o_sel[:, cols(hh)]
                    + gates_t[r + 2:r + 3] * o_win[:, cols(hh)])

        for pr in range(2):
            o_ref[0, :, (2 * g + pr) * LANES:(2 * g + pr + 1) * LANES] = jnp.concatenate(
                [head_out(2 * pr), head_out(2 * pr + 1)], axis=0).T.astype(BF16)


def _nsa_attn(q, gn, kv, kc2, vc2, cos, sin, ovt, ex):
    b, t, _ = q.shape
    tq, tk = NSA_TQ, NSA_TK
    g = NSA_KV_GROUPS
    assert t % tq == 0 and t % tk == 0
    full = lambda col: pl.BlockSpec((1, t, LANES), lambda bi, i: (bi, 0, col))
    return pl.pallas_call(
        _nsa_attn_kernel,
        grid=(b, t // tq),
        in_specs=[pl.BlockSpec((1, tq, NSA_WIDTH), lambda bi, i: (bi, i, 0)),
                  pl.BlockSpec((1, tq, LANES), lambda bi, i: (bi, i, 0)),
                  full(2), full(3), full(4), full(5),
                  pl.BlockSpec((g, kc2.shape[1], LANES), lambda bi, i: (bi, 0, 0)),
                  pl.BlockSpec((g, vc2.shape[1], LANES), lambda bi, i: (bi, 0, 0)),
                  pl.BlockSpec((tq, LANES), lambda bi, i: (i, 0)),
                  pl.BlockSpec((tq, LANES), lambda bi, i: (i, 0)),
                  _const_spec(cos.shape), _const_spec(sin.shape),
                  _const_spec(ovt.shape), _const_spec(ex.shape)],
        out_specs=pl.BlockSpec((1, tq, NSA_WIDTH), lambda bi, i: (bi, i, 0)),
        out_shape=jax.ShapeDtypeStruct((b, t, NSA_WIDTH), BF16),
        scratch_shapes=[pltpu.VMEM((2, g, t, LANES), BF16),
                        pltpu.VMEM((2, g, t // tk, NSA_ACC_ROWS, tk), BF16),
                        pltpu.VMEM((g, 4 * tq, LANES), BF16),
                        pltpu.VMEM((g, t // SEL_BLOCK, tq), BF16),
                        pltpu.VMEM((2 * g, 1, 4 * tq), F32),
                        pltpu.VMEM((2 * g, NSA_ACC_ROWS, 4 * tq), F32)],
        compiler_params=_cparams("parallel", "arbitrary"),
        name="nsa_attn",
    )(q, gn, kv, kv, kv, kv, kc2, vc2, cos, sin, cos, sin, ovt, ex)


def _merge_kernel(x_ref, yn_ref, bonus_ref, g_ref, lng_ref, lnb_ref, on_ref, gm_ref,
                  wrw_ref, wnsa_ref, wout_ref, o_ref):
    d = x_ref.shape[1]
    y_rw = ((yn_ref[...].astype(F32) * lng_ref[...] + lnb_ref[...] + bonus_ref[...].astype(F32))
            * g_ref[...].astype(F32))
    t_rw = _mm(y_rw, wrw_ref[...])
    t_ns = _mm(on_ref[...], wnsa_ref[...])
    gm = gm_ref[...].astype(F32)
    mix = _sigmoid(gm[:, 0:d]) * t_rw + _sigmoid(gm[:, d:2 * d]) * t_ns
    o_ref[...] = x_ref[...] + _mm(mix, wout_ref[...])


def _merge_out(x2, yn, bonus, g, ln_g, ln_b, o_nsa, gm, w_rw, w_nsa, w_out, tm):
    n, d = x2.shape
    row = lambda w: pl.BlockSpec((tm, w), lambda i: (i, 0))
    return pl.pallas_call(
        _merge_kernel,
        grid=(n // tm,),
        in_specs=[row(d), row(RW_WIDTH), row(RW_WIDTH), row(RW_WIDTH),
                  _const_spec(ln_g.shape), _const_spec(ln_b.shape),
                  row(NSA_WIDTH), row(2 * d),
                  _const_spec(w_rw.shape), _const_spec(w_nsa.shape), _const_spec(w_out.shape)],
        out_specs=row(d),
        out_shape=jax.ShapeDtypeStruct((n, d), F32),
        compiler_params=_cparams("parallel"),
        name="merge_out",
    )(x2, yn, bonus, g, ln_g, ln_b, o_nsa, gm, w_rw, w_nsa, w_out)


def _mem_kv_kernel(m_ref, g_ref, w_ref, o_ref):
    o_ref[0] = _mm(_rms(m_ref[0], g_ref[...]), w_ref[...]).astype(BF16)


def _mem_kv(mem, g, wkv):
    b, m, d = mem.shape
    return pl.pallas_call(
        _mem_kv_kernel,
        grid=(b,),
        in_specs=[pl.BlockSpec((1, m, d), lambda bi: (bi, 0, 0)), _const_spec(g.shape),
                  _const_spec(wkv.shape)],
        out_specs=pl.BlockSpec((1, m, 2 * d), lambda bi: (bi, 0, 0)),
        out_shape=jax.ShapeDtypeStruct((b, m, 2 * d), BF16),
        compiler_params=_cparams("parallel"),
        name="mem_kv",
    )(mem, g, wkv)


def _xattn_kernel(h_ref, g_ref, wq_ref, kv_ref, wo_ref, o_ref):
    d = h_ref.shape[2]
    hd = d // XA_HEADS
    h = h_ref[0]
    q = (_mm(_rms(h, g_ref[...]), wq_ref[...]) * hd ** -0.5).astype(BF16)
    outs = []
    for hh in range(XA_HEADS):
        cs = slice(hh * hd, (hh + 1) * hd)
        s = lax.dot_general(q[:, cs], kv_ref[0, :, cs], (((1,), (1,)), ((), ())),
                            preferred_element_type=F32)
        e = jnp.exp(s - jnp.max(s, axis=-1, keepdims=True))
        p = e / jnp.sum(e, axis=-1, keepdims=True)
        outs.append(jnp.dot(p.astype(BF16), kv_ref[0, :, d + hh * hd:d + (hh + 1) * hd],
                            preferred_element_type=F32))
    o_ref[0] = h + _mm(jnp.concatenate(outs, axis=1), wo_ref[...])


def _xattn(h3, g, wq, kv, wo, tm):
    b, t, d = h3.shape
    m = kv.shape[1]
    return pl.pallas_call(
        _xattn_kernel,
        grid=(b, t // tm),
        in_specs=[pl.BlockSpec((1, tm, d), lambda bi, i: (bi, i, 0)), _const_spec(g.shape),
                  _const_spec(wq.shape),
                  pl.BlockSpec((1, m, 2 * d), lambda bi, i: (bi, 0, 0)),
                  _const_spec(wo.shape)],
        out_specs=pl.BlockSpec((1, tm, d), lambda bi, i: (bi, i, 0)),
        out_shape=jax.ShapeDtypeStruct((b, t, d), F32),
        compiler_params=_cparams("parallel", "parallel"),
        name="xattn",
    )(h3, g, wq, kv, wo)


FFN_CHUNK = 256


def _ffn_kernel(h_ref, g_ref, wgu_ref, wd_ref, gf_ref, o_ref):
    h = h_ref[...]
    hn = _rms(h, g_ref[...]).astype(BF16)
    dff = wd_ref.shape[0]
    acc = jnp.zeros(h.shape, F32)
    for c0 in range(0, dff, FFN_CHUNK):
        gate = jnp.dot(hn, wgu_ref[:, c0:c0 + FFN_CHUNK], preferred_element_type=F32)
        up = jnp.dot(hn, wgu_ref[:, dff + c0:dff + c0 + FFN_CHUNK], preferred_element_type=F32)
        act = (gate * _sigmoid(gate) * up).astype(BF16)
        acc = acc + jnp.dot(act, wd_ref[c0:c0 + FFN_CHUNK, :], preferred_element_type=F32)
    o_ref[...] = _rms(h + acc, gf_ref[...])


def _ffn(h2, g, wgu, wd, gf, tm):
    n, d = h2.shape
    assert wd.shape[0] % FFN_CHUNK == 0
    return pl.pallas_call(
        _ffn_kernel,
        grid=(n // tm,),
        in_specs=[pl.BlockSpec((tm, d), lambda i: (i, 0)), _const_spec(g.shape),
                  _const_spec(wgu.shape), _const_spec(wd.shape), _const_spec(gf.shape)],
        out_specs=pl.BlockSpec((tm, d), lambda i: (i, 0)),
        out_shape=jax.ShapeDtypeStruct((n, d), F32),
        compiler_params=_cparams("parallel"),
        name="ffn",
    )(h2, g, wgu, wd, gf)


def _head_sum_matrix():
    idx = np.arange(LANES) // HEAD_DIM
    return jnp.asarray(idx[:, None] == idx[None, :], BF16)


def _prefix_matrix():
    L = RW_CHUNK
    tri = np.tril(np.ones((L, L), np.float32))
    return jnp.asarray(np.concatenate([tri, np.ones((L, L), np.float32)], axis=0), BF16)


def _rope_tables(t):
    half = HEAD_DIM // 2
    inv_freq = ROPE_THETA ** (-jnp.arange(half, dtype=F32) / half)
    ang = jnp.arange(t, dtype=F32)[:, None] * inv_freq[None, :]
    cos, sin = jnp.cos(ang), jnp.sin(ang)
    cos128 = jnp.tile(cos, (1, LANES // half))
    sin128 = jnp.tile(jnp.concatenate([-sin, sin], axis=1), (1, LANES // HEAD_DIM))
    return cos128, sin128


def _overlap_matrix(t):
    n_cmp = (t - CMP_BLOCK) // CMP_STRIDE + 1
    n_sel = t // SEL_BLOCK
    cs = np.arange(n_cmp) * CMP_STRIDE
    ss = np.arange(n_sel) * SEL_BLOCK
    ov = np.clip(np.minimum(cs[:, None] + CMP_BLOCK, ss[None, :] + SEL_BLOCK)
                 - np.maximum(cs[:, None], ss[None, :]), 0, None) / CMP_BLOCK
    out = np.zeros((n_sel, LANES), np.float32)
    out[:, :n_cmp] = ov.T
    return jnp.asarray(out, BF16)


def _expand_matrix(t):
    blk = np.arange(t) // SEL_BLOCK
    return jnp.asarray(blk[:, None] == np.arange(t // SEL_BLOCK)[None, :], BF16)


def kernel(x, mem, norm_mix_g, w_in, shift_mu, rw_w_up, rw_w0, rw_a_up, rw_a0, rw_g_up, rw_k_k, rw_k_a,
           rw_r_k, rw_ln_g, rw_ln_b, nsa_pe_k, nsa_pe_v, nsa_ck1, nsa_ck2, nsa_cv1, nsa_cv2, w_up_rw,
           w_up_nsa, w_out, norm_xa_g, norm_mem_g, xa_wq, xa_wkv, xa_wo, norm_ffn_g, ffn_w_gu,
           ffn_w_down, final_norm_g):
    b, t, d = x.shape
    n = b * t
    h = x.reshape(n, d)
    hsum = _head_sum_matrix()
    tri = _prefix_matrix()
    cos, sin = _rope_tables(t)
    ov = _overlap_matrix(t)
    ex = _expand_matrix(t)
    row = lambda a: a.reshape(1, -1)
    n_half = t // CMP_STRIDE
    assert t // SEL_BLOCK <= LANES and n_half <= LANES

    for l in range(w_in.shape[0]):
        c_g = RW_IN + NSA_WIDTH + 6 * KV_WIDTH
        n_gate = 3 * NSA_HEADS
        w_pad = jnp.concatenate(
            [w_in[l][:, :c_g + n_gate], jnp.zeros((d, LANES - n_gate), F32), w_in[l][:, c_g + n_gate:]],
            axis=1).astype(BF16)
        p_rw, q, kv, gn, gm = _in_proj(h, row(norm_mix_g[l]), w_pad, 512)

        z64 = jnp.zeros((DECAY_LORA, RW_WIDTH), F32)
        wlora = jnp.concatenate([jnp.concatenate([rw_w_up[l], z64], axis=1),
                                 jnp.concatenate([z64, rw_a_up[l]], axis=1)], axis=0).astype(BF16)
        prep = _rwkv_prep(p_rw.reshape(b, t, RW_IN), row(shift_mu[l]), wlora, row(rw_w0[l]),
                          row(rw_a0[l]), rw_g_up[l].astype(BF16), row(rw_k_k[l]), row(rw_k_a[l]),
                          row(rw_r_k[l]), tri, hsum, 512)
        abar, rbar, btil, ktil, bhat, khat, v_rw, gam, g_rw, bonus = prep
        yn = _rwkv_scan(abar, rbar, btil, ktil, bhat, khat, v_rw, gam)

        def halves(cols):
            a = kv[:, cols * LANES:(cols + 1) * LANES].reshape(b, t, NSA_KV_GROUPS, HEAD_DIM)
            return a.transpose(0, 2, 1, 3).reshape(b * NSA_KV_GROUPS, n_half, CMP_STRIDE * HEAD_DIM)

        def w1_halves(w1):
            hw = CMP_STRIDE * HEAD_DIM
            return jnp.concatenate([w1[:hw], w1[hw:]], axis=1).astype(BF16)

        dup = lambda w2: jnp.concatenate([w2, w2], axis=1).astype(BF16)
        kc2, vc2 = _nsa_cmp(halves(0), halves(1), nsa_pe_k[l].reshape(1, -1), nsa_pe_v[l].reshape(1, -1),
                            w1_halves(nsa_ck1[l]), nsa_ck1[l].astype(BF16), dup(nsa_ck2[l]),
                            w1_halves(nsa_cv1[l]), nsa_cv1[l].astype(BF16), dup(nsa_cv2[l]), 4)
        o_nsa = _nsa_attn(q.reshape(b, t, NSA_WIDTH), gn.reshape(b, t, LANES),
                          kv.reshape(b, t, 6 * KV_WIDTH), kc2, vc2, cos, sin, ov, ex)

        h = _merge_out(h, yn.reshape(n, RW_WIDTH), bonus.reshape(n, RW_WIDTH), g_rw.reshape(n, RW_WIDTH),
                       row(rw_ln_g[l]), row(rw_ln_b[l]), o_nsa.reshape(n, NSA_WIDTH), gm,
                       w_up_rw[l].astype(BF16), w_up_nsa[l].astype(BF16), w_out[l].astype(BF16), 512)

        mkv = _mem_kv(mem, row(norm_mem_g[l]), xa_wkv[l].astype(BF16))
        h = _xattn(h.reshape(b, t, d), row(norm_xa_g[l]), xa_wq[l].astype(BF16), mkv,
                   xa_wo[l].astype(BF16), 512).reshape(n, d)

        last = l == w_in.shape[0] - 1
        gf = row(final_norm_g) if last else None
        assert last, "only a single layer is fused with the final norm"
        h = _ffn(h, row(norm_ffn_g[l]), ffn_w_gu[l].astype(BF16), ffn_w_down[l].astype(BF16), gf, 512)
    return h.reshape(b, t, d)
```

```python
import functools
import math

import numpy as np
import jax
import jax.numpy as jnp
from jax import lax
from jax.experimental import pallas as pl
from jax.experimental.pallas import tpu as pltpu

F32 = jnp.float32
BF16 = jnp.bfloat16

HEAD_DIM = 64
NORM_EPS = 1e-6
ROPE_THETA = 10000.0
RW_HEADS = 8
RW_WIDTH = RW_HEADS * HEAD_DIM
DECAY_LORA = 64
AAA_LORA = 64
GATE_LORA = 128
RW_GN_EPS = 64e-5
RW_IN = 3 * RW_WIDTH + DECAY_LORA + AAA_LORA + GATE_LORA
NSA_HEADS = 8
NSA_KV_GROUPS = 2
NSA_WIDTH = NSA_HEADS * HEAD_DIM
KV_WIDTH = NSA_KV_GROUPS * HEAD_DIM
CMP_BLOCK = 32
CMP_STRIDE = 16
CMP_HIDDEN = 128
SEL_BLOCK = 64
SEL_TOPK = 8
WINDOW = 512
XA_HEADS = 4

LANES = 128
VMEM_LIMIT = 56 * 1024 * 1024
RW_CHUNK = 64
NEG_BIG = -1e30
LOG2E = 1.4426950408889634


def _cparams(*sem):
    return pltpu.CompilerParams(dimension_semantics=sem, vmem_limit_bytes=VMEM_LIMIT)


def _mm(a, b):
    return jnp.dot(a.astype(BF16), b.astype(BF16), preferred_element_type=F32)


def _mm_nt(a, b):
    return lax.dot_general(a.astype(BF16), b.astype(BF16), (((1,), (1,)), ((), ())),
                           preferred_element_type=F32)


def _mm_tn(a, b):
    return lax.dot_general(a.astype(BF16), b.astype(BF16), (((0,), (0,)), ((), ())),
                           preferred_element_type=F32)


def _split(x, parts):
    out = []
    for _ in range(parts - 1):
        piece = x.astype(BF16)
        out.append(piece)
        x = x - piece.astype(F32)
    return out + [x.astype(BF16)]


def _split3(x):
    return _split(x, 3)


def _mm_exact_rhs(x, m_bf16, parts=3):
    return sum(jnp.dot(p, m_bf16, preferred_element_type=F32) for p in _split(x, parts))


def _head_sums(x, pair_ones, parts):
    return jnp.concatenate([_mm_exact_rhs(x[:, c:c + LANES], pair_ones, parts)
                            for c in range(0, x.shape[1], LANES)], axis=1)


def _mm_exact_lhs(m_bf16, x, parts=3):
    return sum(jnp.dot(m_bf16, p, preferred_element_type=F32) for p in _split(x, parts))


def _rms(x, g):
    return x * lax.rsqrt(jnp.mean(x * x, axis=-1, keepdims=True) + NORM_EPS) * g


def _sigmoid(x):
    return 1.0 / (1.0 + jnp.exp(-x))


def _const_spec(shape):
    nd = len(shape)
    return pl.BlockSpec(shape, lambda *_: (0,) * nd)


def _in_proj_kernel(x_ref, g_ref, w_ref, prw_ref, q_ref, kv_ref, gn_ref, gm_ref):
    hn = _rms(x_ref[...], g_ref[...]).astype(BF16)
    off = 0
    for o_ref in (prw_ref, q_ref, kv_ref, gn_ref, gm_ref):
        width = o_ref.shape[1]
        for c0 in range(0, width, 512):
            c1 = min(c0 + 512, width)
            o_ref[:, c0:c1] = jnp.dot(hn, w_ref[:, off + c0:off + c1],
                                      preferred_element_type=F32).astype(o_ref.dtype)
        off += width


def _in_proj(x2, g, w_pad, tm):
    n, d = x2.shape
    widths = (RW_IN, NSA_WIDTH, 6 * KV_WIDTH, LANES, 2 * d)
    assert sum(widths) == w_pad.shape[1] and n % tm == 0
    return pl.pallas_call(
        _in_proj_kernel,
        grid=(n // tm,),
        in_specs=[pl.BlockSpec((tm, d), lambda i: (i, 0)),
                  _const_spec((1, d)),
                  _const_spec(w_pad.shape)],
        out_specs=[pl.BlockSpec((tm, w), lambda i: (i, 0)) for w in widths],
        out_shape=[jax.ShapeDtypeStruct((n, w), dt) for w, dt in zip(widths, (F32, BF16, BF16, F32, BF16))],
        compiler_params=_cparams("parallel"),
        name="in_proj",
    )(x2, g, w_pad)


def _rwkv_prep_kernel(p_ref, prev_ref, mu_ref, wlora_ref, w0_ref, a0_ref, gup_ref, kk_ref, ka_ref,
                      rk_ref, tri_ref, hsum_ref,
                      abar_ref, rbar_ref, btil_ref, ktil_ref, bhat_ref, khat_ref, v_ref, gam_ref,
                      g_ref, bonus_ref):
    i = pl.program_id(1)
    tt = p_ref.shape[1]
    w = RW_WIDTH
    cur = p_ref[0]
    prev_row = jnp.where(i == 0, 0.0, prev_ref[0, 7:8, :])
    row = lax.broadcasted_iota(jnp.int32, (tt, 1), 0)
    prev = jnp.where(row == 0, prev_row, pltpu.roll(cur, 1, 0))
    p = cur + mu_ref[...] * (prev - cur)

    r = p[:, 0:w]
    k = p[:, w:2 * w]
    v = p[:, 2 * w:3 * w]
    x_wa = p[:, 3 * w:3 * w + LANES]
    x_g = p[:, 3 * w + LANES:3 * w + 2 * LANES]
    lane = lax.broadcasted_iota(jnp.int32, (1, LANES), 1)
    x_wa = jnp.where(lane < DECAY_LORA, jnp.tanh(x_wa), x_wa)
    lin = _mm(x_wa, wlora_ref[...])
    w_lin = w0_ref[...] + lin[:, 0:w]
    a = _sigmoid(a0_ref[...] + lin[:, w:2 * w])
    g_ref[0] = _mm(_sigmoid(x_g), gup_ref[...]).astype(BF16)

    logw = _sigmoid(w_lin) * (-math.exp(-0.5) * LOG2E)

    hsum = hsum_ref[...]
    kk = k * kk_ref[...]
    kk = kk * lax.rsqrt(jnp.maximum(_head_sums(kk * kk, hsum, 1), 1e-12))
    k2 = k * (1.0 + (a - 1.0) * ka_ref[...])
    bonus_ref[0] = (_head_sums(r * k2 * rk_ref[...], hsum, 2) * v).astype(BF16)
    v_ref[0] = v.astype(BF16)

    L = RW_CHUNK
    tri = tri_ref[...]
    kka = kk * a
    gam_rows = []
    for c in range(tt // L):
        sl = slice(c * L, (c + 1) * L)
        cs = _mm_exact_lhs(tri, logw[sl], 2)
        cum, gam = cs[0:L], jnp.exp2(cs[L:L + 1])
        e_cum = jnp.exp2(cum)
        e_neg = 1.0 / e_cum
        b_til, k_til = kka[sl] * e_neg, k2[sl] * e_neg
        abar_ref[0, sl, :] = (-kk[sl] * jnp.exp2(cum - logw[sl])).astype(BF16)
        rbar_ref[0, sl, :] = (r[sl] * e_cum).astype(BF16)
        btil_ref[0, sl, :] = b_til.astype(BF16)
        ktil_ref[0, sl, :] = k_til.astype(BF16)
        bhat_ref[0, sl, :] = (b_til * gam).astype(BF16)
        khat_ref[0, sl, :] = (k_til * gam).astype(BF16)
        gam_rows.append(gam)
    gam_ref[0] = jnp.concatenate(gam_rows, axis=0)


def _rwkv_prep(p_rw, mu, wlora, w0, a0, gup, k_k, k_a, r_k, tri, hsum, tt):
    b, t, _ = p_rw.shape
    w = RW_WIDTH
    nck = tt // RW_CHUNK
    assert t % tt == 0 and nck % 8 == 0
    tile = lambda: pl.BlockSpec((1, tt, w), lambda bi, i: (bi, i, 0))
    seq_bf = jax.ShapeDtypeStruct((b, t, w), BF16)
    return pl.pallas_call(
        _rwkv_prep_kernel,
        grid=(b, t // tt),
        in_specs=[pl.BlockSpec((1, tt, RW_IN), lambda bi, i: (bi, i, 0)),
                  pl.BlockSpec((1, 8, RW_IN), lambda bi, i: (bi, jnp.maximum(i * (tt // 8) - 1, 0), 0)),
                  _const_spec(mu.shape), _const_spec(wlora.shape), _const_spec(w0.shape),
                  _const_spec(a0.shape), _const_spec(gup.shape), _const_spec(k_k.shape),
                  _const_spec(k_a.shape), _const_spec(r_k.shape), _const_spec(tri.shape),
                  _const_spec(hsum.shape)],
        out_specs=[tile(), tile(), tile(), tile(), tile(), tile(), tile(),
                   pl.BlockSpec((1, nck, w), lambda bi, i: (bi, i, 0)),
                   tile(), tile()],
        out_shape=[seq_bf] * 7 + [jax.ShapeDtypeStruct((b, t // RW_CHUNK, w), F32), seq_bf, seq_bf],
        compiler_params=_cparams("parallel", "parallel"),
        name="rwkv_prep",
    )(p_rw, p_rw, mu, wlora, w0, a0, gup, k_k, k_a, r_k, tri, hsum)


def _rwkv_scan_kernel(abar_ref, rbar_ref, btil_ref, ktil_ref, bhat_ref, khat_ref, v_ref, gam_ref,
                      y_ref, s_ref):
    c = pl.program_id(1)
    L = RW_CHUNK

    @pl.when(c == 0)
    def _():
        s_ref[...] = jnp.zeros_like(s_ref)

    lane = lax.broadcasted_iota(jnp.int32, (1, LANES), 1)
    m0 = lane < HEAD_DIM
    ri = lax.broadcasted_iota(jnp.int32, (2 * L, 2 * L), 0)
    ci = lax.broadcasted_iota(jnp.int32, (2 * L, 2 * L), 1)
    same = (ri // L) == (ci // L)
    low_strict = same & (ci < ri)
    low_incl = same & (ci <= ri)
    zero = jnp.zeros((), BF16)
    nb = abar_ref.shape[0]
    n_pair = RW_HEADS // 2

    def stack(ref, bi, ls):
        x = ref[bi, :, ls]
        return jnp.concatenate([jnp.where(m0, x, zero), jnp.where(m0, zero, x)], axis=0)

    chains = [(bi, pr) for bi in range(nb) for pr in range(n_pair)]
    nch = len(chains)
    lanes_of = lambda pr: slice(pr * LANES, (pr + 1) * LANES)

    x_all = [jnp.concatenate([stack(abar_ref, bi, lanes_of(pr)), stack(rbar_ref, bi, lanes_of(pr))], axis=0)
             for bi, pr in chains]
    vs = [stack(v_ref, bi, lanes_of(pr)) for bi, pr in chains]
    s0 = [s_ref[bi * n_pair + pr] for bi, pr in chains]
    amat = [_mm_nt(x_all[i], jnp.concatenate([stack(btil_ref, bi, lanes_of(pr)),
                                              stack(ktil_ref, bi, lanes_of(pr))], axis=0))
            for i, (bi, pr) in enumerate(chains)]
    xs = [_mm_nt(x_all[i], s0[i]) for i in range(nch)]

    u = [xs[i][0:2 * L] + _mm(jnp.where(low_strict, amat[i][0:2 * L, 2 * L:4 * L], 0.0), vs[i])
         for i in range(nch)]
    pw = [jnp.where(low_strict, a[0:2 * L, 0:2 * L], 0.0).astype(BF16) for a in amat]
    n_sq = int(math.log2(L))
    for step in range(n_sq):
        u = [u[i] + _mm(pw[i], u[i]) for i in range(nch)]
        if step + 1 < n_sq:
            pw = [_mm(q, q).astype(BF16) for q in pw]

    uv = [jnp.concatenate([u[i].astype(BF16), vs[i]], axis=0) for i in range(nch)]
    gam_rows = [gam_ref[bi, pl.ds(c % 8, 1), :] for bi in range(nb)]
    for i, (bi, pr) in enumerate(chains):
        gam = gam_rows[bi][:, lanes_of(pr)]
        bk_hat = jnp.concatenate([stack(bhat_ref, bi, lanes_of(pr)), stack(khat_ref, bi, lanes_of(pr))],
                                 axis=0)
        s_ref[bi * n_pair + pr] = s0[i] * gam + _mm_tn(uv[i], bk_hat)
    def head_mean(z):
        lo = jnp.sum(jnp.where(m0, z, 0.0), axis=-1, keepdims=True)
        hi = jnp.sum(jnp.where(m0, 0.0, z), axis=-1, keepdims=True)
        return jnp.where(m0, lo, hi) * (1.0 / HEAD_DIM)

    for i, (bi, pr) in enumerate(chains):
        a_r =jnp.concatenate([jnp.where(low_incl, amat[i][2 * L:4 * L, 0:2 * L], 0.0),
                               jnp.where(low_incl, amat[i][2 * L:4 * L, 2 * L:4 * L], 0.0)], axis=1)
        y2 = xs[i][2 * L:4 * L] + _mm(a_r, uv[i])
        y = y2[0:L] + y2[L:2 * L]
        d = y - head_mean(y)
        y_ref[bi, :, lanes_of(pr)] = (d * lax.rsqrt(head_mean(d * d) + RW_GN_EPS)).astype(BF16)


RW_SCAN_BATCH = 4


def _rwkv_scan(abar, rbar, btil, ktil, bhat, khat, v, gam):
    b, t, w = abar.shape
    L = RW_CHUNK
    nb = RW_SCAN_BATCH
    assert b % nb == 0
    tile = lambda: pl.BlockSpec((nb, L, w), lambda bi, c: (bi, c, 0))
    return pl.pallas_call(
        _rwkv_scan_kernel,
        grid=(b // nb, t // L),
        in_specs=[tile(), tile(), tile(), tile(), tile(), tile(), tile(),
                  pl.BlockSpec((nb, 8, w), lambda bi, c: (bi, c // 8, 0))],
        out_specs=tile(),
        out_shape=jax.ShapeDtypeStruct((b, t, w), BF16),
        scratch_shapes=[pltpu.VMEM((nb * (RW_HEADS // 2), LANES, LANES), F32)],
        compiler_params=_cparams("parallel", "arbitrary"),
        name="rwkv_scan",
    )(abar, rbar, btil, ktil, bhat, khat, v, gam)


def _gelu_tanh(x):
    return 0.5 * x * (1.0 + jnp.tanh(math.sqrt(2.0 / math.pi) * (x + 0.044715 * x * x * x)))


def _nsa_cmp_kernel(kh_ref, vh_ref, pek_ref, pev_ref, k1_ref, k1f_ref, k2_ref, v1_ref, v1f_ref, v2_ref,
                    ko_ref, vo_ref):
    r, nh, wid = kh_ref.shape
    for h_ref, pe_ref, w1_ref, w1f_ref, w2_ref, o_ref in (
            (kh_ref, pek_ref, k1_ref, k1f_ref, k2_ref, ko_ref),
            (vh_ref, pev_ref, v1_ref, v1f_ref, v2_ref, vo_ref)):
        halves = h_ref[...].reshape(r * nh, wid)
        z = _mm(halves, w1_ref[...])
        bias = _mm(jnp.broadcast_to(pe_ref[...], (8, pe_ref.shape[1])), w1f_ref[...])[0:1]
        pre = z[:, 0:CMP_HIDDEN] + pltpu.roll(z[:, CMP_HIDDEN:], r * nh - 1, 0) + bias
        o_ref[...] = _mm(_gelu_tanh(pre), w2_ref[...]).reshape(r, nh, LANES)


def _nsa_cmp(k_halves, v_halves, pe_k, pe_v, k1, k1f, k2d, v1, v1f, v2d, r):
    n, nh, wid = k_halves.shape
    assert n % r == 0
    tile_in = lambda: pl.BlockSpec((r, nh, wid), lambda i: (i, 0, 0))
    tile_out = lambda: pl.BlockSpec((r, nh, LANES), lambda i: (i, 0, 0))
    out = jax.ShapeDtypeStruct((n, nh, LANES), F32)
    consts = (pe_k, pe_v, k1, k1f, k2d, v1, v1f, v2d)
    return pl.pallas_call(
        _nsa_cmp_kernel,
        grid=(n // r,),
        in_specs=[tile_in(), tile_in()] + [_const_spec(c.shape) for c in consts],
        out_specs=[tile_out(), tile_out()],
        out_shape=[out, out],
        compiler_params=_cparams("parallel"),
        name="nsa_cmp",
    )(k_halves, v_halves, pe_k, pe_v, k1, k1f, k2d, v1, v1f, v2d)


NSA_TQ = 256
NSA_TK = 256
NSA_LOOKAHEAD = 1


def _rope_swap(x):
    lane = lax.broadcasted_iota(jnp.int32, (1, LANES), 1)
    first = (lane % HEAD_DIM) < (HEAD_DIM // 2)
    return jnp.where(first, pltpu.roll(x, LANES - HEAD_DIM // 2, 1), pltpu.roll(x, HEAD_DIM // 2, 1))


def _masked_softmax_parts(s, mask):
    s = jnp.where(mask, s, NEG_BIG)
    e = jnp.where(mask, jnp.exp(s - jnp.max(s, axis=-1, keepdims=True)), 0.0)
    return e / jnp.maximum(jnp.sum(e, axis=-1, keepdims=True), 1e-30)


NSA_ACC_ROWS = HEAD_DIM + 16


def _nsa_attn_kernel(q_ref, gn_ref, ks_ref, vs_ref, kw_ref, vw_ref, kc_ref, vc_ref,
                     cosq_ref, sinq_ref, cosk_ref, sink_ref, ovt_ref, ext_ref,
                     o_ref,
                     k2_sc, vt_sc, qrot_sc, sel_sc, m_sc, acc_sc):
    i = pl.program_id(1)
    tq, tk = NSA_TQ, NSA_TK
    t = ks_ref.shape[1]
    n_kt = t // tk
    n_sel = t // SEL_BLOCK
    lane = lax.broadcasted_iota(jnp.int32, (1, LANES), 1)
    lo_half = lane < HEAD_DIM

    @pl.when(i == 0)
    def _():
        ones = jnp.ones((NSA_ACC_ROWS - HEAD_DIM, tk), BF16)
        for jt in range(n_kt):
            rs = slice(jt * tk, (jt + 1) * tk)
            cos, sin = cosk_ref[rs, :], sink_ref[rs, :]
            for br, k_src, v_src in ((0, ks_ref, vs_ref), (1, kw_ref, vw_ref)):
                x = k_src[0, rs, :].astype(F32)
                x = x * cos + _rope_swap(x) * sin
                xr = pltpu.roll(x, HEAD_DIM, 1)
                k2_sc[br, 0, rs, :] = jnp.where(lo_half, x, xr).astype(BF16)
                k2_sc[br, 1, rs, :] = jnp.where(lo_half, xr, x).astype(BF16)
                v_t = v_src[0, rs, :].astype(F32).T.astype(BF16)
                for g in range(NSA_KV_GROUPS):
                    vt_sc[br, g, jt, 0:HEAD_DIM, :] = v_t[g * HEAD_DIM:(g + 1) * HEAD_DIM]
                    vt_sc[br, g, jt, HEAD_DIM:NSA_ACC_ROWS, :] = ones

    tq_row = i * tq + lax.broadcasted_iota(jnp.int32, (1, tq), 1)
    tq4_row = jnp.concatenate([tq_row] * 4, axis=1)
    gates_t = _sigmoid(gn_ref[0]).T
    cosq, sinq = cosq_ref[...], sinq_ref[...]
    qscale = HEAD_DIM ** -0.5 * LOG2E
    nt_dims = (((1,), (1,)), ((), ()))
    cols = lambda hh: slice(hh * tq, (hh + 1) * tq)

    def stack_heads(xa, xb):
        parts = [jnp.where(lo_half, xa, 0.0), jnp.where(lo_half, 0.0, xa),
                 jnp.where(lo_half, xb, 0.0), jnp.where(lo_half, 0.0, xb)]
        return (jnp.concatenate(parts, axis=0) * qscale).astype(BF16)

    j_last = (i * tq + tq - 1) // tk
    j_first_win = jnp.maximum(i * tq - (WINDOW - 1), 0) // tk
    blk = lax.broadcasted_iota(jnp.int32, (n_sel, 1), 0)
    cur = tq_row // SEL_BLOCK
    forced = (blk == 0) | (blk == cur) | (blk == cur - 1)
    allowed = blk <= cur
    n_valid = (t - CMP_BLOCK) // CMP_STRIDE + 1
    cblk = lax.broadcasted_iota(jnp.int32, (LANES, 1), 0)
    cmask = (cblk * CMP_STRIDE + (CMP_BLOCK - 1) <= tq4_row) & (cblk < n_valid)

    o_cmp = []
    for g in range(NSA_KV_GROUPS):
        qa = q_ref[0, :, (2 * g) * LANES:(2 * g + 1) * LANES].astype(F32)
        qb = q_ref[0, :, (2 * g + 1) * LANES:(2 * g + 2) * LANES].astype(F32)
        q_cmp = stack_heads(qa, qb)
        qrot_sc[g] = stack_heads(qa * cosq + _rope_swap(qa) * sinq, qb * cosq + _rope_swap(qb) * sinq)

        s_c = lax.dot_general(kc_ref[g].astype(BF16), q_cmp, nt_dims, preferred_element_type=F32)
        s_c = jnp.where(cmask, s_c, NEG_BIG)
        e_c = jnp.where(cmask, jnp.exp2(s_c - jnp.max(s_c, axis=0, keepdims=True)), 0.0)
        p_c = e_c / jnp.maximum(jnp.sum(e_c, axis=0, keepdims=True), 1e-30)
        o_cmp.append(_mm(vc_ref[g].T[0:HEAD_DIM], p_c))

        p_sum = p_c[:, cols(0)] + p_c[:, cols(1)] + p_c[:, cols(2)] + p_c[:, cols(3)]
        imp = sum(jnp.dot(ovt_ref[...], part, preferred_element_type=F32) for part in _split3(p_sum))
        imp = jnp.where(forced, 1e4, jnp.where(allowed, imp, -1.0))
        rank = jnp.zeros((n_sel, tq), F32)
        for b in range(n_sel):
            row_b = imp[b:b + 1, :]
            rank = rank + jnp.where((row_b > imp) | ((row_b == imp) & (b < blk)), 1.0, 0.0)
        sel_sc[g] = jnp.where(rank < min(SEL_TOPK, n_sel), 1.0, 0.0).astype(BF16)

    for ch in range(2 * NSA_KV_GROUPS):
        m_sc[ch] = jnp.full(m_sc.shape[1:], NEG_BIG, F32)
        acc_sc[ch] = jnp.zeros(acc_sc.shape[1:], F32)

    def run(items):
        def scores(j, pos0, br, g):
            r0 = pl.multiple_of(j * tk, tk)
            kpos = pos0 + lax.broadcasted_iota(jnp.int32, (tk, 1), 0)
            s = lax.dot_general(k2_sc[br, g, pl.ds(r0, tk), :], qrot_sc[g], nt_dims,
                                preferred_element_type=F32)
            if br == 0:
                hit = jnp.dot(ext_ref[pl.ds(r0, tk), :], sel_sc[g], preferred_element_type=F32)
                keep = (kpos <= tq_row) & (hit > 0.5)
            else:
                keep = (kpos <= tq_row) & (kpos > tq_row - WINDOW)
            return s + jnp.concatenate([jnp.where(keep, 0.0, NEG_BIG)] * 4, axis=1)

        pending = [scores(*it) for it in items[:NSA_LOOKAHEAD]]
        for n, (j, _, br, g) in enumerate(items):
            if n + NSA_LOOKAHEAD < len(items):
                pending.append(scores(*items[n + NSA_LOOKAHEAD]))
            s_cur = pending.pop(0)
            ch = br * NSA_KV_GROUPS + g
            m_old = m_sc[ch]
            m_new = jnp.maximum(m_old, jnp.max(s_cur, axis=0, keepdims=True))
            p = jnp.exp2(s_cur - m_new).astype(BF16)
            m_sc[ch] = m_new
            acc_sc[ch] = jnp.exp2(m_old - m_new) * acc_sc[ch] + jnp.dot(
                vt_sc[br, g, j], p, preferred_element_type=F32)

    def sel_only(j, carry):
        run([(j, j * tk, 0, g) for g in range(NSA_KV_GROUPS)])
        return carry

    lax.fori_loop(0, j_first_win, sel_only, 0)

    n_win = -(-(WINDOW - 1) // tk) + -(-tq // tk)
    far = jnp.int32(2 ** 30)
    items = []
    for d in range(n_win):
        j = j_last - (n_win - 1) + d
        j_load = jnp.maximum(j, 0)
        pos0 = jnp.where(j >= j_first_win, j_load * tk, far)
        items += [(j_load, pos0, br, g) for g in range(NSA_KV_GROUPS) for br in (0, 1)]
    run(items)

    def finish(ch):
        acc = acc_sc[ch]
        out = acc[0:HEAD_DIM] / jnp.maximum(acc[HEAD_DIM:HEAD_DIM + 1], 1e-30)
        return jnp.where(m_sc[ch] > 0.5 * NEG_BIG, out, 0.0)

    for g in range(NSA_KV_GROUPS):
        o_sel, o_win = finish(g), finish(NSA_KV_GROUPS + g)

        def head_out(hh):
            r = (4 * g + hh) * 3
            return (gates_t[r:r + 1] * o_cmp[g][:, cols(hh)] + gates_t[r + 1:r + 2] * o_sel[:, cols(hh)]
                    + gates_t[r + 2:r + 3] * o_win[:, cols(hh)])

        for pr in range(2):
            o_ref[0, :, (2 * g + pr) * LANES:(2 * g + pr + 1) * LANES] = jnp.concatenate(
                [head_out(2 * pr), head_out(2 * pr + 1)], axis=0).T.astype(BF16)


def _nsa_attn(q, gn, kv, kc2, vc2, cos, sin, ovt, ex):
    b, t, _ = q.shape
    tq, tk = NSA_TQ, NSA_TK
    g = NSA_KV_GROUPS
    assert t % tq == 0 and t % tk == 0
    full = lambda col: pl.BlockSpec((1, t, LANES), lambda bi, i: (bi, 0, col))
    return pl.pallas_call(
        _nsa_attn_kernel,
        grid=(b, t // tq),
        in_specs=[pl.BlockSpec((1, tq, NSA_WIDTH), lambda bi, i: (bi, i, 0)),
                  pl.BlockSpec((1, tq, LANES), lambda bi, i: (bi, i, 0)),
                  full(2), full(3), full(4), full(5),
                  pl.BlockSpec((g, kc2.shape[1], LANES), lambda bi, i: (bi, 0, 0)),
                  pl.BlockSpec((g, vc2.shape[1], LANES), lambda bi, i: (bi, 0, 0)),
                  pl.BlockSpec((tq, LANES), lambda bi, i: (i, 0)),
                  pl.BlockSpec((tq, LANES), lambda bi, i: (i, 0)),
                  _const_spec(cos.shape), _const_spec(sin.shape),
                  _const_spec(ovt.shape), _const_spec(ex.shape)],
        out_specs=pl.BlockSpec((1, tq, NSA_WIDTH), lambda bi, i: (bi, i, 0)),
        out_shape=jax.ShapeDtypeStruct((b, t, NSA_WIDTH), BF16),
        scratch_shapes=[pltpu.VMEM((2, g, t, LANES), BF16),
                        pltpu.VMEM((2, g, t // tk, NSA_ACC_ROWS, tk), BF16),
                        pltpu.VMEM((g, 4 * tq, LANES), BF16),
                        pltpu.VMEM((g, t // SEL_BLOCK, tq), BF16),
                        pltpu.VMEM((2 * g, 1, 4 * tq), F32),
                        pltpu.VMEM((2 * g, NSA_ACC_ROWS, 4 * tq), F32)],
        compiler_params=_cparams("parallel", "arbitrary"),
        name="nsa_attn",
    )(q, gn, kv, kv, kv, kv, kc2, vc2, cos, sin, cos, sin, ovt, ex)


def _merge_kernel(x_ref, yn_ref, bonus_ref, g_ref, lng_ref, lnb_ref, on_ref, gm_ref,
                  wrw_ref, wnsa_ref, wout_ref, o_ref):
    d = x_ref.shape[1]
    y_rw = ((yn_ref[...].astype(F32) * lng_ref[...] + lnb_ref[...] + bonus_ref[...].astype(F32))
            * g_ref[...].astype(F32))
    t_rw = _mm(y_rw, wrw_ref[...])
    t_ns = _mm(on_ref[...], wnsa_ref[...])
    gm = gm_ref[...].astype(F32)
    mix = _sigmoid(gm[:, 0:d]) * t_rw + _sigmoid(gm[:, d:2 * d]) * t_ns
    o_ref[...] = x_ref[...] + _mm(mix, wout_ref[...])


def _merge_out(x2, yn, bonus, g, ln_g, ln_b, o_nsa, gm, w_rw, w_nsa, w_out, tm):
    n, d = x2.shape
    row = lambda w: pl.BlockSpec((tm, w), lambda i: (i, 0))
    return pl.pallas_call(
        _merge_kernel,
        grid=(n // tm,),
        in_specs=[row(d), row(RW_WIDTH), row(RW_WIDTH), row(RW_WIDTH),
                  _const_spec(ln_g.shape), _const_spec(ln_b.shape),
                  row(NSA_WIDTH), row(2 * d),
                  _const_spec(w_rw.shape), _const_spec(w_nsa.shape), _const_spec(w_out.shape)],
        out_specs=row(d),
        out_shape=jax.ShapeDtypeStruct((n, d), F32),
        compiler_params=_cparams("parallel"),
        name="merge_out",
    )(x2, yn, bonus, g, ln_g, ln_b, o_nsa, gm, w_rw, w_nsa, w_out)


def _mem_kv_kernel(m_ref, g_ref, w_ref, o_ref):
    o_ref[0] = _mm(_rms(m_ref[0], g_ref[...]), w_ref[...]).astype(BF16)


def _mem_kv(mem, g, wkv):
    b, m, d = mem.shape
    return pl.pallas_call(
        _mem_kv_kernel,
        grid=(b,),
        in_specs=[pl.BlockSpec((1, m, d), lambda bi: (bi, 0, 0)), _const_spec(g.shape),
                  _const_spec(wkv.shape)],
        out_specs=pl.BlockSpec((1, m, 2 * d), lambda bi: (bi, 0, 0)),
        out_shape=jax.ShapeDtypeStruct((b, m, 2 * d), BF16),
        compiler_params=_cparams("parallel"),
        name="mem_kv",
    )(mem, g, wkv)


def _xattn_kernel(h_ref, g_ref, wq_ref, kv_ref, wo_ref, o_ref):
    d = h_ref.shape[2]
    hd = d // XA_HEADS
    h = h_ref[0]
    q = (_mm(_rms(h, g_ref[...]), wq_ref[...]) * hd ** -0.5).astype(BF16)
    outs = []
    for hh in range(XA_HEADS):
        cs = slice(hh * hd, (hh + 1) * hd)
        s = lax.dot_general(q[:, cs], kv_ref[0, :, cs], (((1,), (1,)), ((), ())),
                            preferred_element_type=F32)
        e = jnp.exp(s - jnp.max(s, axis=-1, keepdims=True))
        p = e / jnp.sum(e, axis=-1, keepdims=True)
        outs.append(jnp.dot(p.astype(BF16), kv_ref[0, :, d + hh * hd:d + (hh + 1) * hd],
                            preferred_element_type=F32))
    o_ref[0] = h + _mm(jnp.concatenate(outs, axis=1), wo_ref[...])


def _xattn(h3, g, wq, kv, wo, tm):
    b, t, d = h3.shape
    m = kv.shape[1]
    return pl.pallas_call(
        _xattn_kernel,
        grid=(b, t // tm),
        in_specs=[pl.BlockSpec((1, tm, d), lambda bi, i: (bi, i, 0)), _const_spec(g.shape),
                  _const_spec(wq.shape),
                  pl.BlockSpec((1, m, 2 * d), lambda bi, i: (bi, 0, 0)),
                  _const_spec(wo.shape)],
        out_specs=pl.BlockSpec((1, tm, d), lambda bi, i: (bi, i, 0)),
        out_shape=jax.ShapeDtypeStruct((b, t, d), F32),
        compiler_params=_cparams("parallel", "parallel"),
        name="xattn",
    )(h3, g, wq, kv, wo)


FFN_CHUNK = 256


def _ffn_kernel(h_ref, g_ref, wgu_ref, wd_ref, gf_ref, o_ref):
    h = h_ref[...]
    hn = _rms(h, g_ref[...]).astype(BF16)
    dff = wd_ref.shape[0]
    acc = jnp.zeros(h.shape, F32)
    for c0 in range(0, dff, FFN_CHUNK):
        gate = jnp.dot(hn, wgu_ref[:, c0:c0 + FFN_CHUNK], preferred_element_type=F32)
        up = jnp.dot(hn, wgu_ref[:, dff + c0:dff + c0 + FFN_CHUNK], preferred_element_type=F32)
        act = (gate * _sigmoid(gate) * up).astype(BF16)
        acc = acc + jnp.dot(act, wd_ref[c0:c0 + FFN_CHUNK, :], preferred_element_type=F32)
    o_ref[...] = _rms(h + acc, gf_ref[...])


def _ffn(h2, g, wgu, wd, gf, tm):
    n, d = h2.shape
    assert wd.shape[0] % FFN_CHUNK == 0
    return pl.pallas_call(
        _ffn_kernel,
        grid=(n // tm,),
        in_specs=[pl.BlockSpec((tm, d), lambda i: (i, 0)), _const_spec(g.shape),
                  _const_spec(wgu.shape), _const_spec(wd.shape), _const_spec(gf.shape)],
        out_specs=pl.BlockSpec((tm, d), lambda i: (i, 0)),
        out_shape=jax.ShapeDtypeStruct((n, d), F32),
        compiler_params=_cparams("parallel"),
        name="ffn",
    )(h2, g, wgu, wd, gf)


def _head_sum_matrix():
    idx = np.arange(LANES) // HEAD_DIM
    return jnp.asarray(idx[:, None] == idx[None, :], BF16)


def _prefix_matrix():
    L = RW_CHUNK
    tri = np.tril(np.ones((L, L), np.float32))
    return jnp.asarray(np.concatenate([tri, np.ones((16, L), np.float32)], axis=0), BF16)


def _rope_tables(t):
    half = HEAD_DIM // 2
    inv_freq = ROPE_THETA ** (-jnp.arange(half, dtype=F32) / half)
    ang = jnp.arange(t, dtype=F32)[:, None] * inv_freq[None, :]
    cos, sin = jnp.cos(ang), jnp.sin(ang)
    cos128 = jnp.tile(cos, (1, LANES // half))
    sin128 = jnp.tile(jnp.concatenate([-sin, sin], axis=1), (1, LANES // HEAD_DIM))
    return cos128, sin128


def _overlap_matrix(t):
    n_cmp = (t - CMP_BLOCK) // CMP_STRIDE + 1
    n_sel = t // SEL_BLOCK
    cs = np.arange(n_cmp) * CMP_STRIDE
    ss = np.arange(n_sel) * SEL_BLOCK
    ov = np.clip(np.minimum(cs[:, None] + CMP_BLOCK, ss[None, :] + SEL_BLOCK)
                 - np.maximum(cs[:, None], ss[None, :]), 0, None) / CMP_BLOCK
    out = np.zeros((n_sel, LANES), np.float32)
    out[:, :n_cmp] = ov.T
    return jnp.asarray(out, BF16)


def _expand_matrix(t):
    blk = np.arange(t) // SEL_BLOCK
    return jnp.asarray(blk[:, None] == np.arange(t // SEL_BLOCK)[None, :], BF16)


def kernel(x, mem, norm_mix_g, w_in, shift_mu, rw_w_up, rw_w0, rw_a_up, rw_a0, rw_g_up, rw_k_k, rw_k_a,
           rw_r_k, rw_ln_g, rw_ln_b, nsa_pe_k, nsa_pe_v, nsa_ck1, nsa_ck2, nsa_cv1, nsa_cv2, w_up_rw,
           w_up_nsa, w_out, norm_xa_g, norm_mem_g, xa_wq, xa_wkv, xa_wo, norm_ffn_g, ffn_w_gu,
           ffn_w_down, final_norm_g):
    b, t, d = x.shape
    n = b * t
    h = x.reshape(n, d)
    hsum = _head_sum_matrix()
    tri = _prefix_matrix()
    cos, sin = _rope_tables(t)
    ov = _overlap_matrix(t)
    ex = _expand_matrix(t)
    row = lambda a: a.reshape(1, -1)
    n_half = t // CMP_STRIDE
    assert t // SEL_BLOCK <= LANES and n_half <= LANES

    for l in range(w_in.shape[0]):
        c_g = RW_IN + NSA_WIDTH + 6 * KV_WIDTH
        n_gate = 3 * NSA_HEADS
        w_pad = jnp.concatenate(
            [w_in[l][:, :c_g + n_gate], jnp.zeros((d, LANES - n_gate), F32), w_in[l][:, c_g + n_gate:]],
            axis=1).astype(BF16)
        p_rw, q, kv, gn, gm = _in_proj(h, row(norm_mix_g[l]), w_pad, 512)

        z64 = jnp.zeros((DECAY_LORA, RW_WIDTH), F32)
        wlora = jnp.concatenate([jnp.concatenate([rw_w_up[l], z64], axis=1),
                                 jnp.concatenate([z64, rw_a_up[l]], axis=1)], axis=0).astype(BF16)
        prep = _rwkv_prep(p_rw.reshape(b, t, RW_IN), row(shift_mu[l]), wlora, row(rw_w0[l]),
                          row(rw_a0[l]), rw_g_up[l].astype(BF16), row(rw_k_k[l]), row(rw_k_a[l]),
                          row(rw_r_k[l]), tri, hsum, 512)
        abar, rbar, btil, ktil, bhat, khat, v_rw, gam, g_rw, bonus = prep
        yn = _rwkv_scan(abar, rbar, btil, ktil, bhat, khat, v_rw, gam)

        def halves(cols):
            a = kv[:, cols * LANES:(cols + 1) * LANES].reshape(b, t, NSA_KV_GROUPS, HEAD_DIM)
            return a.transpose(0, 2, 1, 3).reshape(b * NSA_KV_GROUPS, n_half, CMP_STRIDE * HEAD_DIM)

        def w1_halves(w1):
            hw = CMP_STRIDE * HEAD_DIM
            return jnp.concatenate([w1[:hw], w1[hw:]], axis=1).astype(BF16)

        dup = lambda w2: jnp.concatenate([w2, w2], axis=1).astype(BF16)
        kc2, vc2 = _nsa_cmp(halves(0), halves(1), nsa_pe_k[l].reshape(1, -1), nsa_pe_v[l].reshape(1, -1),
                            w1_halves(nsa_ck1[l]), nsa_ck1[l].astype(BF16), dup(nsa_ck2[l]),
                            w1_halves(nsa_cv1[l]), nsa_cv1[l].astype(BF16), dup(nsa_cv2[l]), 4)
        o_nsa = _nsa_attn(q.reshape(b, t, NSA_WIDTH), gn.reshape(b, t, LANES),
                          kv.reshape(b, t, 6 * KV_WIDTH), kc2, vc2, cos, sin, ov, ex)

        h = _merge_out(h, yn.reshape(n, RW_WIDTH), bonus.reshape(n, RW_WIDTH), g_rw.reshape(n, RW_WIDTH),
                       row(rw_ln_g[l]), row(rw_ln_b[l]), o_nsa.reshape(n, NSA_WIDTH), gm,
                       w_up_rw[l].astype(BF16), w_up_nsa[l].astype(BF16), w_out[l].astype(BF16), 512)

        mkv = _mem_kv(mem, row(norm_mem_g[l]), xa_wkv[l].astype(BF16))
        h = _xattn(h.reshape(b, t, d), row(norm_xa_g[l]), xa_wq[l].astype(BF16), mkv,
                   xa_wo[l].astype(BF16), 512).reshape(n, d)

        last = l == w_in.shape[0] - 1
        gf = row(final_norm_g) if last else None
        assert last, "only a single layer is fused with the final norm"
        h = _ffn(h, row(norm_ffn_g[l]), ffn_w_gu[l].astype(BF16), ffn_w_down[l].astype(BF16), gf, 512)
    return h.reshape(b, t, d)
```

```python
import functools
import math

import numpy as np
import jax
import jax.numpy as jnp
from jax import lax
from jax.experimental import pallas as pl
from jax.experimental.pallas import tpu as pltpu

F32 = jnp.float32
BF16 = jnp.bfloat16

HEAD_DIM = 64
NORM_EPS = 1e-6
ROPE_THETA = 10000.0
RW_HEADS = 8
RW_WIDTH = RW_HEADS * HEAD_DIM
DECAY_LORA = 64
AAA_LORA = 64
GATE_LORA = 128
RW_GN_EPS = 64e-5
RW_IN = 3 * RW_WIDTH + DECAY_LORA + AAA_LORA + GATE_LORA
NSA_HEADS = 8
NSA_KV_GROUPS = 2
NSA_WIDTH = NSA_HEADS * HEAD_DIM
KV_WIDTH = NSA_KV_GROUPS * HEAD_DIM
CMP_BLOCK = 32
CMP_STRIDE = 16
CMP_HIDDEN = 128
SEL_BLOCK = 64
SEL_TOPK = 8
WINDOW = 512
XA_HEADS = 4

LANES = 128
VMEM_LIMIT = 56 * 1024 * 1024
RW_CHUNK = 64
NEG_BIG = -1e30
LOG2E = 1.4426950408889634


def _cparams(*sem):
    return pltpu.CompilerParams(dimension_semantics=sem, vmem_limit_bytes=VMEM_LIMIT)


def _mm(a, b):
    return jnp.dot(a.astype(BF16), b.astype(BF16), preferred_element_type=F32)


def _mm_nt(a, b):
    return lax.dot_general(a.astype(BF16), b.astype(BF16), (((1,), (1,)), ((), ())),
                           preferred_element_type=F32)


def _mm_tn(a, b):
    return lax.dot_general(a.astype(BF16), b.astype(BF16), (((0,), (0,)), ((), ())),
                           preferred_element_type=F32)


def _split(x, parts):
    out = []
    for _ in range(parts - 1):
        piece = x.astype(BF16)
        out.append(piece)
        x = x - piece.astype(F32)
    return out + [x.astype(BF16)]


def _split3(x):
    return _split(x, 3)


def _mm_exact_rhs(x, m_bf16, parts=3):
    return sum(jnp.dot(p, m_bf16, preferred_element_type=F32) for p in _split(x, parts))


def _head_sums(x, pair_ones, parts):
    return jnp.concatenate([_mm_exact_rhs(x[:, c:c + LANES], pair_ones, parts)
                            for c in range(0, x.shape[1], LANES)], axis=1)


def _mm_exact_lhs(m_bf16, x, parts=3):
    return sum(jnp.dot(m_bf16, p, preferred_element_type=F32) for p in _split(x, parts))


def _rms(x, g):
    return x * lax.rsqrt(jnp.mean(x * x, axis=-1, keepdims=True) + NORM_EPS) * g


def _sigmoid(x):
    return 1.0 / (1.0 + jnp.exp(-x))


def _const_spec(shape):
    nd = len(shape)
    return pl.BlockSpec(shape, lambda *_: (0,) * nd)


def _in_proj_kernel(x_ref, g_ref, w_main_ref, w_gate_ref, w_merge_ref, *out_refs):
    hn = _rms(x_ref[...], g_ref[...]).astype(BF16)
    w_of = [w_main_ref] * (len(out_refs) - 2) + [w_gate_ref, w_merge_ref]
    off = 0
    for o_ref, w_ref in zip(out_refs, w_of):
        width = o_ref.shape[1]
        base = off if w_ref is w_main_ref else 0
        for c0 in range(0, width, 512):
            c1 = min(c0 + 512, width)
            o_ref[:, c0:c1] = jnp.dot(hn, w_ref[:, base + c0:base + c1],
                                      preferred_element_type=F32).astype(o_ref.dtype)
        off += width


def _in_proj(x2, g, w_main, w_gate, w_merge, tm):
    n, d = x2.shape
    widths = (RW_IN, NSA_WIDTH, KV_WIDTH, KV_WIDTH, 4 * KV_WIDTH, LANES, 2 * d)
    dtypes = (F32, BF16, F32, F32, BF16, F32, BF16)
    assert sum(widths[:-2]) == w_main.shape[1] and widths[-2:] == (w_gate.shape[1], w_merge.shape[1])
    assert n % tm == 0
    return pl.pallas_call(
        _in_proj_kernel,
        grid=(n // tm,),
        in_specs=[pl.BlockSpec((tm, d), lambda i: (i, 0)),
                  _const_spec((1, d)),
                  _const_spec(w_main.shape), _const_spec(w_gate.shape), _const_spec(w_merge.shape)],
        out_specs=[pl.BlockSpec((tm, w), lambda i: (i, 0)) for w in widths],
        out_shape=[jax.ShapeDtypeStruct((n, w), dt) for w, dt in zip(widths, dtypes)],
        compiler_params=_cparams("parallel"),
        name="in_proj",
    )(x2, g, w_main, w_gate, w_merge)


def _rwkv_prep_kernel(p_ref, prev_ref, mu_ref, wlora_ref, w0_ref, a0_ref, gup_ref, kk_ref, ka_ref,
                      rk_ref, tri_ref, hsum_ref,
                      abar_ref, rbar_ref, btil_ref, ktil_ref, bhat_ref, khat_ref, v_ref, gam_ref,
                      g_ref, bonus_ref):
    i = pl.program_id(1)
    tt = p_ref.shape[1]
    w = RW_WIDTH
    cur = p_ref[0]
    prev_row = jnp.where(i == 0, 0.0, prev_ref[0, 7:8, :])
    row = lax.broadcasted_iota(jnp.int32, (tt, 1), 0)
    prev = jnp.where(row == 0, prev_row, pltpu.roll(cur, 1, 0))
    p = cur + mu_ref[...] * (prev - cur)

    r = p[:, 0:w]
    k = p[:, w:2 * w]
    v = p[:, 2 * w:3 * w]
    x_wa = p[:, 3 * w:3 * w + LANES]
    x_g = p[:, 3 * w + LANES:3 * w + 2 * LANES]
    lane = lax.broadcasted_iota(jnp.int32, (1, LANES), 1)
    x_wa = jnp.where(lane < DECAY_LORA, jnp.tanh(x_wa), x_wa)
    lin = _mm(x_wa, wlora_ref[...])
    w_lin = w0_ref[...] + lin[:, 0:w]
    a = _sigmoid(a0_ref[...] + lin[:, w:2 * w])
    g_ref[0] = _mm(_sigmoid(x_g), gup_ref[...]).astype(BF16)

    logw = _sigmoid(w_lin) * (-math.exp(-0.5) * LOG2E)

    hsum = hsum_ref[...]
    kk = k * kk_ref[...]
    kk = kk * lax.rsqrt(jnp.maximum(_head_sums(kk * kk, hsum, 1), 1e-12))
    k2 = k * (1.0 + (a - 1.0) * ka_ref[...])
    bonus_ref[0] = (_head_sums(r * k2 * rk_ref[...], hsum, 2) * v).astype(BF16)
    v_ref[0] = v.astype(BF16)

    L = RW_CHUNK
    tri = tri_ref[...]
    kka = kk * a
    gam_rows = []
    for c in range(tt // L):
        sl = slice(c * L, (c + 1) * L)
        cs = _mm_exact_lhs(tri, logw[sl], 2)
        cum, gam = cs[0:L], jnp.exp2(cs[L:L + 1])
        e_cum = jnp.exp2(cum)
        e_neg = 1.0 / e_cum
        b_til, k_til = kka[sl] * e_neg, k2[sl] * e_neg
        abar_ref[0, sl, :] = (-kk[sl] * jnp.exp2(cum - logw[sl])).astype(BF16)
        rbar_ref[0, sl, :] = (r[sl] * e_cum).astype(BF16)
        btil_ref[0, sl, :] = b_til.astype(BF16)
        ktil_ref[0, sl, :] = k_til.astype(BF16)
        bhat_ref[0, sl, :] = (b_til * gam).astype(BF16)
        khat_ref[0, sl, :] = (k_til * gam).astype(BF16)
        gam_rows.append(gam)
    gam_ref[0] = jnp.concatenate(gam_rows, axis=0)


def _rwkv_prep(p_rw, mu, wlora, w0, a0, gup, k_k, k_a, r_k, tri, hsum, tt):
    b, t, _ = p_rw.shape
    w = RW_WIDTH
    nck = tt // RW_CHUNK
    assert t % tt == 0 and nck % 8 == 0
    tile = lambda: pl.BlockSpec((1, tt, w), lambda bi, i: (bi, i, 0))
    seq_bf = jax.ShapeDtypeStruct((b, t, w), BF16)
    return pl.pallas_call(
        _rwkv_prep_kernel,
        grid=(b, t // tt),
        in_specs=[pl.BlockSpec((1, tt, RW_IN), lambda bi, i: (bi, i, 0)),
                  pl.BlockSpec((1, 8, RW_IN), lambda bi, i: (bi, jnp.maximum(i * (tt // 8) - 1, 0), 0)),
                  _const_spec(mu.shape), _const_spec(wlora.shape), _const_spec(w0.shape),
                  _const_spec(a0.shape), _const_spec(gup.shape), _const_spec(k_k.shape),
                  _const_spec(k_a.shape), _const_spec(r_k.shape), _const_spec(tri.shape),
                  _const_spec(hsum.shape)],
        out_specs=[tile(), tile(), tile(), tile(), tile(), tile(), tile(),
                   pl.BlockSpec((1, nck, w), lambda bi, i: (bi, i, 0)),
                   tile(), tile()],
        out_shape=[seq_bf] * 7 + [jax.ShapeDtypeStruct((b, t // RW_CHUNK, w), F32), seq_bf, seq_bf],
        compiler_params=_cparams("parallel", "parallel"),
        name="rwkv_prep",
    )(p_rw, p_rw, mu, wlora, w0, a0, gup, k_k, k_a, r_k, tri, hsum)


def _rwkv_scan_kernel(abar_ref, rbar_ref, btil_ref, ktil_ref, bhat_ref, khat_ref, v_ref, gam_ref,
                      y_ref, s_ref):
    c = pl.program_id(1)
    L = RW_CHUNK

    @pl.when(c == 0)
    def _():
        s_ref[...] = jnp.zeros_like(s_ref)

    lane = lax.broadcasted_iota(jnp.int32, (1, LANES), 1)
    m0 = lane < HEAD_DIM
    ri = lax.broadcasted_iota(jnp.int32, (2 * L, 2 * L), 0)
    ci = lax.broadcasted_iota(jnp.int32, (2 * L, 2 * L), 1)
    same = (ri // L) == (ci // L)
    low_strict = same & (ci < ri)
    low_incl = same & (ci <= ri)
    zero = jnp.zeros((), BF16)
    nb = abar_ref.shape[0]
    n_pair = RW_HEADS // 2

    def stack(ref, bi, ls):
        x = ref[bi, :, ls]
        return jnp.concatenate([jnp.where(m0, x, zero), jnp.where(m0, zero, x)], axis=0)

    chains = [(bi, pr) for bi in range(nb) for pr in range(n_pair)]
    nch = len(chains)
    lanes_of = lambda pr: slice(pr * LANES, (pr + 1) * LANES)

    x_all = [jnp.concatenate([stack(abar_ref, bi, lanes_of(pr)), stack(rbar_ref, bi, lanes_of(pr))], axis=0)
             for bi, pr in chains]
    vs = [stack(v_ref, bi, lanes_of(pr)) for bi, pr in chains]
    s0 = [s_ref[bi * n_pair + pr] for bi, pr in chains]
    amat = [_mm_nt(x_all[i], jnp.concatenate([stack(btil_ref, bi, lanes_of(pr)),
                                              stack(ktil_ref, bi, lanes_of(pr))], axis=0))
            for i, (bi, pr) in enumerate(chains)]
    xs = [_mm_nt(x_all[i], s0[i]) for i in range(nch)]

    u = [xs[i][0:2 * L] + _mm(jnp.where(low_strict, amat[i][0:2 * L, 2 * L:4 * L], 0.0), vs[i])
         for i in range(nch)]
    pw = [jnp.where(low_strict, a[0:2 * L, 0:2 * L], 0.0).astype(BF16) for a in amat]
    n_sq = int(math.log2(L))
    for step in range(n_sq):
        u = [u[i] + _mm(pw[i], u[i]) for i in range(nch)]
        if step + 1 < n_sq:
            pw = [_mm(q, q).astype(BF16) for q in pw]

    uv = [jnp.concatenate([u[i].astype(BF16), vs[i]], axis=0) for i in range(nch)]
    gam_rows = [gam_ref[bi, pl.ds(c % 8, 1), :] for bi in range(nb)]
    for i, (bi, pr) in enumerate(chains):
        gam = gam_rows[bi][:, lanes_of(pr)]
        bk_hat = jnp.concatenate([stack(bhat_ref, bi, lanes_of(pr)), stack(khat_ref, bi, lanes_of(pr))],
                                 axis=0)
        s_ref[bi * n_pair + pr] = s0[i] * gam + _mm_tn(uv[i], bk_hat)
    def head_mean(z):
        lo = jnp.sum(jnp.where(m0, z, 0.0), axis=-1, keepdims=True)
        hi = jnp.sum(jnp.where(m0, 0.0, z), axis=-1, keepdims=True)
        return jnp.where(m0, lo, hi) * (1.0 / HEAD_DIM)

    for i, (bi, pr) in enumerate(chains):
        a_r =jnp.concatenate([jnp.where(low_incl, amat[i][2 * L:4 * L, 0:2 * L], 0.0),
                               jnp.where(low_incl, amat[i][2 * L:4 * L, 2 * L:4 * L], 0.0)], axis=1)
        y2 = xs[i][2 * L:4 * L] + _mm(a_r, uv[i])
        y = y2[0:L] + y2[L:2 * L]
        d = y - head_mean(y)
        y_ref[bi, :, lanes_of(pr)] = (d * lax.rsqrt(head_mean(d * d) + RW_GN_EPS)).astype(BF16)


RW_SCAN_BATCH = 4


def _rwkv_scan(abar, rbar, btil, ktil, bhat, khat, v, gam):
    b, t, w = abar.shape
    L = RW_CHUNK
    nb = RW_SCAN_BATCH
    assert b % nb == 0
    tile = lambda: pl.BlockSpec((nb, L, w), lambda bi, c: (bi, c, 0))
    return pl.pallas_call(
        _rwkv_scan_kernel,
        grid=(b // nb, t // L),
        in_specs=[tile(), tile(), tile(), tile(), tile(), tile(), tile(),
                  pl.BlockSpec((nb, 8, w), lambda bi, c: (bi, c // 8, 0))],
        out_specs=tile(),
        out_shape=jax.ShapeDtypeStruct((b, t, w), BF16),
        scratch_shapes=[pltpu.VMEM((nb * (RW_HEADS // 2), LANES, LANES), F32)],
        compiler_params=_cparams("parallel", "arbitrary"),
        name="rwkv_scan",
    )(abar, rbar, btil, ktil, bhat, khat, v, gam)


def _gelu_tanh(x):
    return 0.5 * x * (1.0 + jnp.tanh(math.sqrt(2.0 / math.pi) * (x + 0.044715 * x * x * x)))


def _nsa_cmp_kernel(kc_ref, vc_ref, pek_ref, pev_ref, k1_ref, k1f_ref, k2_ref, v1_ref, v1f_ref, v2_ref,
                    ko_ref, vo_ref):
    nh = ko_ref.shape[1]
    hid = CMP_HIDDEN
    for x_ref, pe_ref, w1_ref, w1f_ref, w2_ref, o_ref in (
            (kc_ref, pek_ref, k1_ref, k1f_ref, k2_ref, ko_ref),
            (vc_ref, pev_ref, v1_ref, v1f_ref, v2_ref, vo_ref)):
        z = jnp.zeros((nh, 2 * NSA_KV_GROUPS * hid), F32)
        for tok in range(CMP_STRIDE):
            z = z + _mm(x_ref[0, pl.ds(tok, nh, stride=CMP_STRIDE), :], w1_ref[tok])
        bias = _mm(jnp.broadcast_to(pe_ref[...], (8, pe_ref.shape[1])), w1f_ref[...])[0:1]
        for g in range(NSA_KV_GROUPS):
            zg = z[:, 2 * g * hid:2 * (g + 1) * hid]
            pre = zg[:, 0:hid] + pltpu.roll(zg[:, hid:2 * hid], nh - 1, 0) + bias
            o_ref[g] = _mm(_gelu_tanh(pre), w2_ref[...])


def _nsa_cmp(kc, vc, pe_k, pe_v, k1, k1f, k2d, v1, v1f, v2d):
    b, t, _ = kc.shape
    nh = t // CMP_STRIDE
    tile_in = lambda: pl.BlockSpec((1, t, LANES), lambda i: (i, 0, 0))
    tile_out = lambda: pl.BlockSpec((NSA_KV_GROUPS, nh, LANES), lambda i: (i, 0, 0))
    out = jax.ShapeDtypeStruct((b * NSA_KV_GROUPS, nh, LANES), F32)
    consts = (pe_k, pe_v, k1, k1f, k2d, v1, v1f, v2d)
    return pl.pallas_call(
        _nsa_cmp_kernel,
        grid=(b,),
        in_specs=[tile_in(), tile_in()] + [_const_spec(c.shape) for c in consts],
        out_specs=[tile_out(), tile_out()],
        out_shape=[out, out],
        compiler_params=_cparams("parallel"),
        name="nsa_cmp",
    )(kc, vc, pe_k, pe_v, k1, k1f, k2d, v1, v1f, v2d)


NSA_TQ = 256
NSA_TK = 256
NSA_LOOKAHEAD = 1


def _rope_swap(x):
    lane = lax.broadcasted_iota(jnp.int32, (1, LANES), 1)
    first = (lane % HEAD_DIM) < (HEAD_DIM // 2)
    return jnp.where(first, pltpu.roll(x, LANES - HEAD_DIM // 2, 1), pltpu.roll(x, HEAD_DIM // 2, 1))


def _masked_softmax_parts(s, mask):
    s = jnp.where(mask, s, NEG_BIG)
    e = jnp.where(mask, jnp.exp(s - jnp.max(s, axis=-1, keepdims=True)), 0.0)
    return e / jnp.maximum(jnp.sum(e, axis=-1, keepdims=True), 1e-30)


NSA_ACC_ROWS = HEAD_DIM + 16


def _nsa_attn_kernel(q_ref, gn_ref, ks_ref, vs_ref, kw_ref, vw_ref, kc_ref, vc_ref,
                     cosq_ref, sinq_ref, cosk_ref, sink_ref, ovt_ref, ext_ref,
                     o_ref,
                     k2_sc, vt_sc, qrot_sc, sel_sc, m_sc, acc_sc):
    i = pl.program_id(1)
    tq, tk = NSA_TQ, NSA_TK
    t = ks_ref.shape[1]
    n_kt = t // tk
    n_sel = t // SEL_BLOCK
    lane = lax.broadcasted_iota(jnp.int32, (1, LANES), 1)
    lo_half = lane < HEAD_DIM

    @pl.when(i == 0)
    def _():
        ones = jnp.ones((NSA_ACC_ROWS - HEAD_DIM, tk), BF16)
        for jt in range(n_kt):
            rs = slice(jt * tk, (jt + 1) * tk)
            cos, sin = cosk_ref[rs, :], sink_ref[rs, :]
            for br, k_src, v_src in ((0, ks_ref, vs_ref), (1, kw_ref, vw_ref)):
                x = k_src[0, rs, :].astype(F32)
                x = x * cos + _rope_swap(x) * sin
                xr = pltpu.roll(x, HEAD_DIM, 1)
                k2_sc[br, 0, rs, :] = jnp.where(lo_half, x, xr).astype(BF16)
                k2_sc[br, 1, rs, :] = jnp.where(lo_half, xr, x).astype(BF16)
                v_t = v_src[0, rs, :].astype(F32).T.astype(BF16)
                for g in range(NSA_KV_GROUPS):
                    vt_sc[br, g, jt, 0:HEAD_DIM, :] = v_t[g * HEAD_DIM:(g + 1) * HEAD_DIM]
                    vt_sc[br, g, jt, HEAD_DIM:NSA_ACC_ROWS, :] = ones

    tq_row = i * tq + lax.broadcasted_iota(jnp.int32, (1, tq), 1)
    tq4_row = jnp.concatenate([tq_row] * 4, axis=1)
    gates_t = _sigmoid(gn_ref[0]).T
    cosq, sinq = cosq_ref[...], sinq_ref[...]
    qscale = HEAD_DIM ** -0.5 * LOG2E
    nt_dims = (((1,), (1,)), ((), ()))
    cols = lambda hh: slice(hh * tq, (hh + 1) * tq)

    def stack_heads(xa, xb):
        parts = [jnp.where(lo_half, xa, 0.0), jnp.where(lo_half, 0.0, xa),
                 jnp.where(lo_half, xb, 0.0), jnp.where(lo_half, 0.0, xb)]
        return (jnp.concatenate(parts, axis=0) * qscale).astype(BF16)

    j_last = (i * tq + tq - 1) // tk
    j_first_win = jnp.maximum(i * tq - (WINDOW - 1), 0) // tk
    blk = lax.broadcasted_iota(jnp.int32, (n_sel, 1), 0)
    cur = tq_row // SEL_BLOCK
    forced = (blk == 0) | (blk == cur) | (blk == cur - 1)
    allowed = blk <= cur
    n_valid = (t - CMP_BLOCK) // CMP_STRIDE + 1
    cblk = lax.broadcasted_iota(jnp.int32, (LANES, 1), 0)
    cmask = (cblk * CMP_STRIDE + (CMP_BLOCK - 1) <= tq4_row) & (cblk < n_valid)

    o_cmp = []
    for g in range(NSA_KV_GROUPS):
        qa = q_ref[0, :, (2 * g) * LANES:(2 * g + 1) * LANES].astype(F32)
        qb = q_ref[0, :, (2 * g + 1) * LANES:(2 * g + 2) * LANES].astype(F32)
        q_cmp = stack_heads(qa, qb)
        qrot_sc[g] = stack_heads(qa * cosq + _rope_swap(qa) * sinq, qb * cosq + _rope_swap(qb) * sinq)

        s_c = lax.dot_general(kc_ref[g].astype(BF16), q_cmp, nt_dims, preferred_element_type=F32)
        s_c = jnp.where(cmask, s_c, NEG_BIG)
        e_c = jnp.where(cmask, jnp.exp2(s_c - jnp.max(s_c, axis=0, keepdims=True)), 0.0)
        p_c = e_c / jnp.maximum(jnp.sum(e_c, axis=0, keepdims=True), 1e-30)
        o_cmp.append(_mm(vc_ref[g].T[0:HEAD_DIM], p_c))

        p_sum = p_c[:, cols(0)] + p_c[:, cols(1)] + p_c[:, cols(2)] + p_c[:, cols(3)]
        imp = sum(jnp.dot(ovt_ref[...], part, preferred_element_type=F32) for part in _split3(p_sum))
        imp = jnp.where(forced, 1e4, jnp.where(allowed, imp, -1.0))
        rank = jnp.zeros((n_sel, tq), F32)
        for b in range(n_sel):
            row_b = imp[b:b + 1, :]
            rank = rank + jnp.where((row_b > imp) | ((row_b == imp) & (b < blk)), 1.0, 0.0)
        sel_sc[g] = jnp.where(rank < min(SEL_TOPK, n_sel), 1.0, 0.0).astype(BF16)

    for ch in range(2 * NSA_KV_GROUPS):
        m_sc[ch] = jnp.full(m_sc.shape[1:], NEG_BIG, F32)
        acc_sc[ch] = jnp.zeros(acc_sc.shape[1:], F32)

    def run(items):
        def scores(j, pos0, br, g):
            r0 = pl.multiple_of(j * tk, tk)
            kpos = pos0 + lax.broadcasted_iota(jnp.int32, (tk, 1), 0)
            s = lax.dot_general(k2_sc[br, g, pl.ds(r0, tk), :], qrot_sc[g], nt_dims,
                                preferred_element_type=F32)
            if br == 0:
                hit = jnp.dot(ext_ref[pl.ds(r0, tk), :], sel_sc[g], preferred_element_type=F32)
                keep = (kpos <= tq_row) & (hit > 0.5)
            else:
                keep = (kpos <= tq_row) & (kpos > tq_row - WINDOW)
            return s + jnp.concatenate([jnp.where(keep, 0.0, NEG_BIG)] * 4, axis=1)

        pending = [scores(*it) for it in items[:NSA_LOOKAHEAD]]
        for n, (j, _, br, g) in enumerate(items):
            if n + NSA_LOOKAHEAD < len(items):
                pending.append(scores(*items[n + NSA_LOOKAHEAD]))
            s_cur = pending.pop(0)
            ch = br * NSA_KV_GROUPS + g
            m_old = m_sc[ch]
            m_new = jnp.maximum(m_old, jnp.max(s_cur, axis=0, keepdims=True))
            p = jnp.exp2(s_cur - m_new).astype(BF16)
            m_sc[ch] = m_new
            acc_sc[ch] = jnp.exp2(m_old - m_new) * acc_sc[ch] + jnp.dot(
                vt_sc[br, g, j], p, preferred_element_type=F32)

    def sel_only(j, carry):
        run([(j, j * tk, 0, g) for g in range(NSA_KV_GROUPS)])
        return carry

    def sel_and_win(j, carry):
        run([(j, j * tk, br, g) for g in range(NSA_KV_GROUPS) for br in (0, 1)])
        return carry

    lax.fori_loop(0, j_first_win, sel_only, 0)
    lax.fori_loop(j_first_win, j_last + 1, sel_and_win, 0)

    def finish(ch):
        acc = acc_sc[ch]
        out = acc[0:HEAD_DIM] / jnp.maximum(acc[HEAD_DIM:HEAD_DIM + 1], 1e-30)
        return jnp.where(m_sc[ch] > 0.5 * NEG_BIG, out, 0.0)

    for g in range(NSA_KV_GROUPS):
        o_sel, o_win = finish(g), finish(NSA_KV_GROUPS + g)

        def head_out(hh):
            r = (4 * g + hh) * 3
            return (gates_t[r:r + 1] * o_cmp[g][:, cols(hh)] + gates_t[r + 1:r + 2] * o_sel[:, cols(hh)]
                    + gates_t[r + 2:r + 3] * o_win[:, cols(hh)])

        for pr in range(2):
            o_ref[0, :, (2 * g + pr) * LANES:(2 * g + pr + 1) * LANES] = jnp.concatenate(
                [head_out(2 * pr), head_out(2 * pr + 1)], axis=0).T.astype(BF16)


def _nsa_attn(q, gn, kv, kc2, vc2, cos, sin, ovt, ex):
    b, t, _ = q.shape
    tq, tk = NSA_TQ, NSA_TK
    g = NSA_KV_GROUPS
    assert t % tq == 0 and t % tk == 0
    full = lambda col: pl.BlockSpec((1, t, LANES), lambda bi, i: (bi, 0, col))
    return pl.pallas_call(
        _nsa_attn_kernel,
        grid=(b, t // tq),
        in_specs=[pl.BlockSpec((1, tq, NSA_WIDTH), lambda bi, i: (bi, i, 0)),
                  pl.BlockSpec((1, tq, LANES), lambda bi, i: (bi, i, 0)),
                  full(0), full(1), full(2), full(3),
                  pl.BlockSpec((g, kc2.shape[1], LANES), lambda bi, i: (bi, 0, 0)),
                  pl.BlockSpec((g, vc2.shape[1], LANES), lambda bi, i: (bi, 0, 0)),
                  pl.BlockSpec((tq, LANES), lambda bi, i: (i, 0)),
                  pl.BlockSpec((tq, LANES), lambda bi, i: (i, 0)),
                  _const_spec(cos.shape), _const_spec(sin.shape),
                  _const_spec(ovt.shape), _const_spec(ex.shape)],
        out_specs=pl.BlockSpec((1, tq, NSA_WIDTH), lambda bi, i: (bi, i, 0)),
        out_shape=jax.ShapeDtypeStruct((b, t, NSA_WIDTH), BF16),
        scratch_shapes=[pltpu.VMEM((2, g, t, LANES), BF16),
                        pltpu.VMEM((2, g, t // tk, NSA_ACC_ROWS, tk), BF16),
                        pltpu.VMEM((g, 4 * tq, LANES), BF16),
                        pltpu.VMEM((g, t // SEL_BLOCK, tq), BF16),
                        pltpu.VMEM((2 * g, 1, 4 * tq), F32),
                        pltpu.VMEM((2 * g, NSA_ACC_ROWS, 4 * tq), F32)],
        compiler_params=_cparams("parallel", "arbitrary"),
        name="nsa_attn",
    )(q, gn, kv, kv, kv, kv, kc2, vc2, cos, sin, cos, sin, ovt, ex)


def _merge_kernel(x_ref, yn_ref, bonus_ref, g_ref, lng_ref, lnb_ref, on_ref, gm_ref,
                  wrw_ref, wnsa_ref, wout_ref, o_ref):
    d = x_ref.shape[1]
    y_rw = ((yn_ref[...].astype(F32) * lng_ref[...] + lnb_ref[...] + bonus_ref[...].astype(F32))
            * g_ref[...].astype(F32))
    t_rw = _mm(y_rw, wrw_ref[...])
    t_ns = _mm(on_ref[...], wnsa_ref[...])
    gm = gm_ref[...].astype(F32)
    mix = _sigmoid(gm[:, 0:d]) * t_rw + _sigmoid(gm[:, d:2 * d]) * t_ns
    o_ref[...] = x_ref[...] + _mm(mix, wout_ref[...])


def _merge_out(x2, yn, bonus, g, ln_g, ln_b, o_nsa, gm, w_rw, w_nsa, w_out, tm):
    n, d = x2.shape
    row = lambda w: pl.BlockSpec((tm, w), lambda i: (i, 0))
    return pl.pallas_call(
        _merge_kernel,
        grid=(n // tm,),
        in_specs=[row(d), row(RW_WIDTH), row(RW_WIDTH), row(RW_WIDTH),
                  _const_spec(ln_g.shape), _const_spec(ln_b.shape),
                  row(NSA_WIDTH), row(2 * d),
                  _const_spec(w_rw.shape), _const_spec(w_nsa.shape), _const_spec(w_out.shape)],
        out_specs=row(d),
        out_shape=jax.ShapeDtypeStruct((n, d), F32),
        compiler_params=_cparams("parallel"),
        name="merge_out",
    )(x2, yn, bonus, g, ln_g, ln_b, o_nsa, gm, w_rw, w_nsa, w_out)


def _xattn_kernel(h_ref, g_ref, wq_ref, mem_ref, gm_ref, wkv_ref, wo_ref, o_ref, kv_sc):
    d = h_ref.shape[2]
    hd = d // XA_HEADS

    @pl.when(pl.program_id(1) == 0)
    def _():
        kv_sc[...] = _mm(_rms(mem_ref[0], gm_ref[...]), wkv_ref[...]).astype(BF16)

    h = h_ref[0]
    q = (_mm(_rms(h, g_ref[...]), wq_ref[...]) * hd ** -0.5).astype(BF16)
    outs = []
    for hh in range(XA_HEADS):
        cs = slice(hh * hd, (hh + 1) * hd)
        s = lax.dot_general(q[:, cs], kv_sc[:, cs], (((1,), (1,)), ((), ())),
                            preferred_element_type=F32)
        e = jnp.exp(s - jnp.max(s, axis=-1, keepdims=True))
        p = e / jnp.sum(e, axis=-1, keepdims=True)
        outs.append(jnp.dot(p.astype(BF16), kv_sc[:, d + hh * hd:d + (hh + 1) * hd],
                            preferred_element_type=F32))
    o_ref[0] = h + _mm(jnp.concatenate(outs, axis=1), wo_ref[...])


def _xattn(h3, g, wq, mem, g_mem, wkv, wo, tm):
    b, t, d = h3.shape
    m = mem.shape[1]
    return pl.pallas_call(
        _xattn_kernel,
        grid=(b, t // tm),
        in_specs=[pl.BlockSpec((1, tm, d), lambda bi, i: (bi, i, 0)), _const_spec(g.shape),
                  _const_spec(wq.shape),
                  pl.BlockSpec((1, m, d), lambda bi, i: (bi, 0, 0)), _const_spec(g_mem.shape),
                  _const_spec(wkv.shape), _const_spec(wo.shape)],
        out_specs=pl.BlockSpec((1, tm, d), lambda bi, i: (bi, i, 0)),
        out_shape=jax.ShapeDtypeStruct((b, t, d), F32),
        scratch_shapes=[pltpu.VMEM((m, 2 * d), BF16)],
        compiler_params=_cparams("parallel", "arbitrary"),
        name="xattn",
    )(h3, g, wq, mem, g_mem, wkv, wo)


FFN_CHUNK = 256


def _ffn_kernel(h_ref, g_ref, wgu_ref, wd_ref, gf_ref, o_ref):
    h = h_ref[...]
    hn = _rms(h, g_ref[...]).astype(BF16)
    dff = wd_ref.shape[0]
    acc = jnp.zeros(h.shape, F32)
    for c0 in range(0, dff, FFN_CHUNK):
        gate = jnp.dot(hn, wgu_ref[:, c0:c0 + FFN_CHUNK], preferred_element_type=F32)
        up = jnp.dot(hn, wgu_ref[:, dff + c0:dff + c0 + FFN_CHUNK], preferred_element_type=F32)
        act = (gate * _sigmoid(gate) * up).astype(BF16)
        acc = acc + jnp.dot(act, wd_ref[c0:c0 + FFN_CHUNK, :], preferred_element_type=F32)
    o_ref[...] = _rms(h + acc, gf_ref[...])


def _ffn(h2, g, wgu, wd, gf, tm):
    n, d = h2.shape
    assert wd.shape[0] % FFN_CHUNK == 0
    return pl.pallas_call(
        _ffn_kernel,
        grid=(n // tm,),
        in_specs=[pl.BlockSpec((tm, d), lambda i: (i, 0)), _const_spec(g.shape),
                  _const_spec(wgu.shape), _const_spec(wd.shape), _const_spec(gf.shape)],
        out_specs=pl.BlockSpec((tm, d), lambda i: (i, 0)),
        out_shape=jax.ShapeDtypeStruct((n, d), F32),
        compiler_params=_cparams("parallel"),
        name="ffn",
    )(h2, g, wgu, wd, gf)


def _head_sum_matrix():
    idx = np.arange(LANES) // HEAD_DIM
    return jnp.asarray(idx[:, None] == idx[None, :], BF16)


def _prefix_matrix():
    L = RW_CHUNK
    tri = np.tril(np.ones((L, L), np.float32))
    return jnp.asarray(np.concatenate([tri, np.ones((16, L), np.float32)], axis=0), BF16)


def _rope_tables(t):
    half = HEAD_DIM // 2
    inv_freq = ROPE_THETA ** (-np.arange(half, dtype=np.float64) / half)
    ang = np.arange(t, dtype=np.float64)[:, None] * inv_freq[None, :]
    cos, sin = np.cos(ang), np.sin(ang)
    cos128 = np.tile(cos, (1, LANES // half))
    sin128 = np.tile(np.concatenate([-sin, sin], axis=1), (1, LANES // HEAD_DIM))
    return jnp.asarray(cos128, F32), jnp.asarray(sin128, F32)


def _overlap_matrix(t):
    n_cmp = (t - CMP_BLOCK) // CMP_STRIDE + 1
    n_sel = t // SEL_BLOCK
    cs = np.arange(n_cmp) * CMP_STRIDE
    ss = np.arange(n_sel) * SEL_BLOCK
    ov = np.clip(np.minimum(cs[:, None] + CMP_BLOCK, ss[None, :] + SEL_BLOCK)
                 - np.maximum(cs[:, None], ss[None, :]), 0, None) / CMP_BLOCK
    out = np.zeros((n_sel, LANES), np.float32)
    out[:, :n_cmp] = ov.T
    return jnp.asarray(out, BF16)


def _expand_matrix(t):
    blk = np.arange(t) // SEL_BLOCK
    return jnp.asarray(blk[:, None] == np.arange(t // SEL_BLOCK)[None, :], BF16)


def kernel(x, mem, norm_mix_g, w_in, shift_mu, rw_w_up, rw_w0, rw_a_up, rw_a0, rw_g_up, rw_k_k, rw_k_a,
           rw_r_k, rw_ln_g, rw_ln_b, nsa_pe_k, nsa_pe_v, nsa_ck1, nsa_ck2, nsa_cv1, nsa_cv2, w_up_rw,
           w_up_nsa, w_out, norm_xa_g, norm_mem_g, xa_wq, xa_wkv, xa_wo, norm_ffn_g, ffn_w_gu,
           ffn_w_down, final_norm_g):
    b, t, d = x.shape
    n = b * t
    h = x.reshape(n, d)
    hsum = _head_sum_matrix()
    tri = _prefix_matrix()
    cos, sin = _rope_tables(t)
    ov = _overlap_matrix(t)
    ex = _expand_matrix(t)
    row = lambda a: a.reshape(1, -1)
    n_half = t // CMP_STRIDE
    assert t // SEL_BLOCK <= LANES and n_half <= LANES

    for l in range(w_in.shape[0]):
        c_g = RW_IN + NSA_WIDTH + 6 * KV_WIDTH
        n_gate = 3 * NSA_HEADS
        w_gate = jnp.pad(w_in[l][:, c_g:c_g + n_gate], ((0, 0), (0, LANES - n_gate))).astype(BF16)
        p_rw, q, kc, vc, kv, gn, gm = _in_proj(h, row(norm_mix_g[l]), w_in[l][:, :c_g].astype(BF16), w_gate,
                                               w_in[l][:, c_g + n_gate:].astype(BF16), 512)

        z64 = jnp.zeros((DECAY_LORA, RW_WIDTH), F32)
        wlora = jnp.concatenate([jnp.concatenate([rw_w_up[l], z64], axis=1),
                                 jnp.concatenate([z64, rw_a_up[l]], axis=1)], axis=0).astype(BF16)
        prep = _rwkv_prep(p_rw.reshape(b, t, RW_IN), row(shift_mu[l]), wlora, row(rw_w0[l]),
                          row(rw_a0[l]), rw_g_up[l].astype(BF16), row(rw_k_k[l]), row(rw_k_a[l]),
                          row(rw_r_k[l]), tri, hsum, 512)
        abar, rbar, btil, ktil, bhat, khat, v_rw, gam, g_rw, bonus = prep
        yn = _rwkv_scan(abar, rbar, btil, ktil, bhat, khat, v_rw, gam)

        def w1_tokens(w1):
            hw = CMP_STRIDE * HEAD_DIM
            per_tok = jnp.concatenate([w1[:hw].reshape(CMP_STRIDE, HEAD_DIM, -1),
                                       w1[hw:].reshape(CMP_STRIDE, HEAD_DIM, -1)], axis=2)
            zero = jnp.zeros_like(per_tok)
            return jnp.concatenate([jnp.concatenate([per_tok, zero], axis=2),
                                    jnp.concatenate([zero, per_tok], axis=2)], axis=1).astype(BF16)

        dup = lambda w2: jnp.concatenate([w2, w2], axis=1).astype(BF16)
        kc2, vc2 = _nsa_cmp(kc.reshape(b, t, KV_WIDTH), vc.reshape(b, t, KV_WIDTH),
                            nsa_pe_k[l].reshape(1, -1), nsa_pe_v[l].reshape(1, -1),
                            w1_tokens(nsa_ck1[l]), nsa_ck1[l].astype(BF16), dup(nsa_ck2[l]),
                            w1_tokens(nsa_cv1[l]), nsa_cv1[l].astype(BF16), dup(nsa_cv2[l]))
        o_nsa = _nsa_attn(q.reshape(b, t, NSA_WIDTH), gn.reshape(b, t, LANES),
                          kv.reshape(b, t, 4 * KV_WIDTH), kc2, vc2, cos, sin, ov, ex)

        h = _merge_out(h, yn.reshape(n, RW_WIDTH), bonus.reshape(n, RW_WIDTH), g_rw.reshape(n, RW_WIDTH),
                       row(rw_ln_g[l]), row(rw_ln_b[l]), o_nsa.reshape(n, NSA_WIDTH), gm,
                       w_up_rw[l].astype(BF16), w_up_nsa[l].astype(BF16), w_out[l].astype(BF16), 512)

        h = _xattn(h.reshape(b, t, d), row(norm_xa_g[l]), xa_wq[l].astype(BF16), mem, row(norm_mem_g[l]),
                   xa_wkv[l].astype(BF16), xa_wo[l].astype(BF16), 512).reshape(n, d)

        last = l == w_in.shape[0] - 1
        gf = row(final_norm_g) if last else None
        assert last, "only a single layer is fused with the final norm"
        h = _ffn(h, row(norm_ffn_g[l]), ffn_w_gu[l].astype(BF16), ffn_w_down[l].astype(BF16), gf, 512)
    return h.reshape(b, t, d)
```

```python
import functools
import math

import numpy as np
import jax
import jax.numpy as jnp
from jax import lax
from jax.experimental import pallas as pl
from jax.experimental.pallas import tpu as pltpu

F32 = jnp.float32
BF16 = jnp.bfloat16

HEAD_DIM = 64
NORM_EPS = 1e-6
ROPE_THETA = 10000.0
RW_HEADS = 8
RW_WIDTH = RW_HEADS * HEAD_DIM
DECAY_LORA = 64
AAA_LORA = 64
GATE_LORA = 128
RW_GN_EPS = 64e-5
RW_IN = 3 * RW_WIDTH + DECAY_LORA + AAA_LORA + GATE_LORA
NSA_HEADS = 8
NSA_KV_GROUPS = 2
NSA_WIDTH = NSA_HEADS * HEAD_DIM
KV_WIDTH = NSA_KV_GROUPS * HEAD_DIM
CMP_BLOCK = 32
CMP_STRIDE = 16
CMP_HIDDEN = 128
SEL_BLOCK = 64
SEL_TOPK = 8
WINDOW = 512
XA_HEADS = 4

LANES = 128
VMEM_LIMIT = 56 * 1024 * 1024
RW_CHUNK = 64
NEG_BIG = -1e30
LOG2E = 1.4426950408889634


def _cparams(*sem):
    return pltpu.CompilerParams(dimension_semantics=sem, vmem_limit_bytes=VMEM_LIMIT)


def _mm(a, b):
    return jnp.dot(a.astype(BF16), b.astype(BF16), preferred_element_type=F32)


def _mm_nt(a, b):
    return lax.dot_general(a.astype(BF16), b.astype(BF16), (((1,), (1,)), ((), ())),
                           preferred_element_type=F32)


def _mm_tn(a, b):
    return lax.dot_general(a.astype(BF16), b.astype(BF16), (((0,), (0,)), ((), ())),
                           preferred_element_type=F32)


def _split(x, parts):
    out = []
    for _ in range(parts - 1):
        piece = x.astype(BF16)
        out.append(piece)
        x = x - piece.astype(F32)
    return out + [x.astype(BF16)]


def _split3(x):
    return _split(x, 3)


def _mm_exact_rhs(x, m_bf16, parts=3):
    return sum(jnp.dot(p, m_bf16, preferred_element_type=F32) for p in _split(x, parts))


def _head_sums(x, pair_ones, parts):
    return jnp.concatenate([_mm_exact_rhs(x[:, c:c + LANES], pair_ones, parts)
                            for c in range(0, x.shape[1], LANES)], axis=1)


def _mm_exact_lhs(m_bf16, x, parts=3):
    return sum(jnp.dot(m_bf16, p, preferred_element_type=F32) for p in _split(x, parts))


def _rms(x, g):
    return x * lax.rsqrt(jnp.mean(x * x, axis=-1, keepdims=True) + NORM_EPS) * g


def _sigmoid(x):
    return 1.0 / (1.0 + jnp.exp(-x))


def _const_spec(shape):
    nd = len(shape)
    return pl.BlockSpec(shape, lambda *_: (0,) * nd)


def _in_proj_kernel(x_ref, g_ref, w_main_ref, w_gate_ref, w_merge_ref, *out_refs):
    hn = _rms(x_ref[...], g_ref[...]).astype(BF16)
    w_of = [w_main_ref] * (len(out_refs) - 2) + [w_gate_ref, w_merge_ref]
    off = 0
    for o_ref, w_ref in zip(out_refs, w_of):
        width = o_ref.shape[1]
        base = off if w_ref is w_main_ref else 0
        for c0 in range(0, width, 512):
            c1 = min(c0 + 512, width)
            o_ref[:, c0:c1] = jnp.dot(hn, w_ref[:, base + c0:base + c1],
                                      preferred_element_type=F32).astype(o_ref.dtype)
        off += width


def _in_proj(x2, g, w_main, w_gate, w_merge, tm):
    n, d = x2.shape
    widths = (RW_IN, NSA_WIDTH, KV_WIDTH, KV_WIDTH, 4 * KV_WIDTH, LANES, 2 * d)
    dtypes = (F32, BF16, F32, F32, BF16, F32, BF16)
    assert sum(widths[:-2]) == w_main.shape[1] and widths[-2:] == (w_gate.shape[1], w_merge.shape[1])
    assert n % tm == 0
    return pl.pallas_call(
        _in_proj_kernel,
        grid=(n // tm,),
        in_specs=[pl.BlockSpec((tm, d), lambda i: (i, 0)),
                  _const_spec((1, d)),
                  _const_spec(w_main.shape), _const_spec(w_gate.shape), _const_spec(w_merge.shape)],
        out_specs=[pl.BlockSpec((tm, w), lambda i: (i, 0)) for w in widths],
        out_shape=[jax.ShapeDtypeStruct((n, w), dt) for w, dt in zip(widths, dtypes)],
        compiler_params=_cparams("parallel"),
        name="in_proj",
    )(x2, g, w_main, w_gate, w_merge)


def _rwkv_prep_kernel(p_ref, prev_ref, mu_ref, wlora_ref, w0_ref, a0_ref, gup_ref, kk_ref, ka_ref,
                      rk_ref, tri_ref, hsum_ref,
                      abar_ref, rbar_ref, btil_ref, ktil_ref, v_ref, gam_ref,
                      g_ref, bonus_ref):
    i = pl.program_id(1)
    tt = p_ref.shape[1]
    w = RW_WIDTH
    cur = p_ref[0]
    prev_row = jnp.where(i == 0, 0.0, prev_ref[0, 7:8, :])
    row = lax.broadcasted_iota(jnp.int32, (tt, 1), 0)
    prev = jnp.where(row == 0, prev_row, pltpu.roll(cur, 1, 0))
    p = cur + mu_ref[...] * (prev - cur)

    r = p[:, 0:w]
    k = p[:, w:2 * w]
    v = p[:, 2 * w:3 * w]
    x_wa = p[:, 3 * w:3 * w + LANES]
    x_g = p[:, 3 * w + LANES:3 * w + 2 * LANES]
    lane = lax.broadcasted_iota(jnp.int32, (1, LANES), 1)
    x_wa = jnp.where(lane < DECAY_LORA, jnp.tanh(x_wa), x_wa)
    lin = _mm(x_wa, wlora_ref[...])
    w_lin = w0_ref[...] + lin[:, 0:w]
    a = _sigmoid(a0_ref[...] + lin[:, w:2 * w])
    g_ref[0] = _mm(_sigmoid(x_g), gup_ref[...]).astype(BF16)

    logw = _sigmoid(w_lin) * (-math.exp(-0.5) * LOG2E)

    hsum = hsum_ref[...]
    kk = k * kk_ref[...]
    kk = kk * lax.rsqrt(jnp.maximum(_head_sums(kk * kk, hsum, 1), 1e-12))
    k2 = k * (1.0 + (a - 1.0) * ka_ref[...])
    bonus_ref[0] = (_head_sums(r * k2 * rk_ref[...], hsum, 1) * v).astype(BF16)
    v_ref[0] = v.astype(BF16)

    L = RW_CHUNK
    tri = tri_ref[...]
    kka = kk * a
    gam_rows = []
    for c in range(tt // L):
        sl = slice(c * L, (c + 1) * L)
        cs = _mm_exact_lhs(tri, logw[sl], 2)
        cum, gam = cs[0:L], jnp.exp2(cs[L:L + 1])
        e_cum = jnp.exp2(cum)
        e_neg = 1.0 / e_cum
        b_til, k_til = kka[sl] * e_neg, k2[sl] * e_neg
        abar_ref[0, sl, :] = (-kk[sl] * jnp.exp2(cum - logw[sl])).astype(BF16)
        rbar_ref[0, sl, :] = (r[sl] * e_cum).astype(BF16)
        btil_ref[0, sl, :] = b_til.astype(BF16)
        ktil_ref[0, sl, :] = k_til.astype(BF16)
        gam_rows.append(gam)
    gam_ref[0] = jnp.concatenate(gam_rows, axis=0)


def _rwkv_prep(p_rw, mu, wlora, w0, a0, gup, k_k, k_a, r_k, tri, hsum, tt):
    b, t, _ = p_rw.shape
    w = RW_WIDTH
    nck = tt // RW_CHUNK
    assert t % tt == 0 and nck % 8 == 0
    tile = lambda: pl.BlockSpec((1, tt, w), lambda bi, i: (bi, i, 0))
    seq_bf = jax.ShapeDtypeStruct((b, t, w), BF16)
    return pl.pallas_call(
        _rwkv_prep_kernel,
        grid=(b, t // tt),
        in_specs=[pl.BlockSpec((1, tt, RW_IN), lambda bi, i: (bi, i, 0)),
                  pl.BlockSpec((1, 8, RW_IN), lambda bi, i: (bi, jnp.maximum(i * (tt // 8) - 1, 0), 0)),
                  _const_spec(mu.shape), _const_spec(wlora.shape), _const_spec(w0.shape),
                  _const_spec(a0.shape), _const_spec(gup.shape), _const_spec(k_k.shape),
                  _const_spec(k_a.shape), _const_spec(r_k.shape), _const_spec(tri.shape),
                  _const_spec(hsum.shape)],
        out_specs=[tile(), tile(), tile(), tile(), tile(),
                   pl.BlockSpec((1, nck, w), lambda bi, i: (bi, i, 0)),
                   tile(), tile()],
        out_shape=[seq_bf] * 5 + [jax.ShapeDtypeStruct((b, t // RW_CHUNK, w), F32), seq_bf, seq_bf],
        compiler_params=_cparams("parallel", "parallel"),
        name="rwkv_prep",
    )(p_rw, p_rw, mu, wlora, w0, a0, gup, k_k, k_a, r_k, tri, hsum)


def _rwkv_scan_kernel(abar_ref, rbar_ref, btil_ref, ktil_ref, v_ref, gam_ref,
                      y_ref, s_ref):
    c = pl.program_id(1)
    L = RW_CHUNK

    @pl.when(c == 0)
    def _():
        s_ref[...] = jnp.zeros_like(s_ref)

    lane = lax.broadcasted_iota(jnp.int32, (1, LANES), 1)
    m0 = lane < HEAD_DIM
    ri = lax.broadcasted_iota(jnp.int32, (2 * L, 2 * L), 0)
    ci = lax.broadcasted_iota(jnp.int32, (2 * L, 2 * L), 1)
    same = (ri // L) == (ci // L)
    low_strict = same & (ci < ri)
    low_incl = same & (ci <= ri)
    zero = jnp.zeros((), BF16)
    nb = abar_ref.shape[0]
    n_pair = RW_HEADS // 2

    def stack(ref, bi, ls):
        x = ref[bi, :, ls]
        return jnp.concatenate([jnp.where(m0, x, zero), jnp.where(m0, zero, x)], axis=0)

    chains = [(bi, pr) for bi in range(nb) for pr in range(n_pair)]
    nch = len(chains)
    lanes_of = lambda pr: slice(pr * LANES, (pr + 1) * LANES)

    x_all = [jnp.concatenate([stack(abar_ref, bi, lanes_of(pr)), stack(rbar_ref, bi, lanes_of(pr))], axis=0)
             for bi, pr in chains]
    vs = [stack(v_ref, bi, lanes_of(pr)) for bi, pr in chains]
    s0 = [s_ref[bi * n_pair + pr] for bi, pr in chains]
    bk = [jnp.concatenate([stack(btil_ref, bi, lanes_of(pr)), stack(ktil_ref, bi, lanes_of(pr))], axis=0)
          for bi, pr in chains]
    amat = [_mm_nt(x_all[i], bk[i]) for i in range(nch)]
    xs = [_mm_nt(x_all[i], s0[i]) for i in range(nch)]

    u = [xs[i][0:2 * L] + _mm(jnp.where(low_strict, amat[i][0:2 * L, 2 * L:4 * L], 0.0), vs[i])
         for i in range(nch)]
    pw = [jnp.where(low_strict, a[0:2 * L, 0:2 * L], 0.0).astype(BF16) for a in amat]
    n_sq = int(math.log2(L))
    for step in range(n_sq):
        u = [u[i] + _mm(pw[i], u[i]) for i in range(nch)]
        if step + 1 < n_sq:
            pw = [_mm(q, q).astype(BF16) for q in pw]

    uv = [jnp.concatenate([u[i].astype(BF16), vs[i]], axis=0) for i in range(nch)]
    gam_rows = [gam_ref[bi, pl.ds(c % 8, 1), :] for bi in range(nb)]
    for i, (bi, pr) in enumerate(chains):
        gam = gam_rows[bi][:, lanes_of(pr)]
        bk_hat = bk[i].astype(F32) * gam
        s_ref[bi * n_pair + pr] = s0[i] * gam + _mm_tn(uv[i], bk_hat)
    def head_mean(z):
        lo = jnp.sum(jnp.where(m0, z, 0.0), axis=-1, keepdims=True)
        hi = jnp.sum(jnp.where(m0, 0.0, z), axis=-1, keepdims=True)
        return jnp.where(m0, lo, hi) * (1.0 / HEAD_DIM)

    for i, (bi, pr) in enumerate(chains):
        a_r =jnp.concatenate([jnp.where(low_incl, amat[i][2 * L:4 * L, 0:2 * L], 0.0),
                               jnp.where(low_incl, amat[i][2 * L:4 * L, 2 * L:4 * L], 0.0)], axis=1)
        y2 = xs[i][2 * L:4 * L] + _mm(a_r, uv[i])
        y = y2[0:L] + y2[L:2 * L]
        d = y - head_mean(y)
        y_ref[bi, :, lanes_of(pr)] = (d * lax.rsqrt(head_mean(d * d) + RW_GN_EPS)).astype(BF16)


RW_SCAN_BATCH = 8


def _rwkv_scan(abar, rbar, btil, ktil, v, gam):
    b, t, w = abar.shape
    L = RW_CHUNK
    nb = RW_SCAN_BATCH
    assert b % nb == 0
    tile = lambda: pl.BlockSpec((nb, L, w), lambda bi, c: (bi, c, 0))
    return pl.pallas_call(
        _rwkv_scan_kernel,
        grid=(b // nb, t // L),
        in_specs=[tile(), tile(), tile(), tile(), tile(),
                  pl.BlockSpec((nb, 8, w), lambda bi, c: (bi, c // 8, 0))],
        out_specs=tile(),
        out_shape=jax.ShapeDtypeStruct((b, t, w), BF16),
        scratch_shapes=[pltpu.VMEM((nb * (RW_HEADS // 2), LANES, LANES), F32)],
        compiler_params=_cparams("parallel", "arbitrary"),
        name="rwkv_scan",
    )(abar, rbar, btil, ktil, v, gam)


def _gelu_tanh(x):
    return 0.5 * x * (1.0 + jnp.tanh(math.sqrt(2.0 / math.pi) * (x + 0.044715 * x * x * x)))


def _nsa_cmp_kernel(kc_ref, vc_ref, pek_ref, pev_ref, k1_ref, k1f_ref, k2_ref, v1_ref, v1f_ref, v2_ref,
                    ko_ref, vo_ref):
    nh = ko_ref.shape[1]
    hid = CMP_HIDDEN
    for x_ref, pe_ref, w1_ref, w1f_ref, w2_ref, o_ref in (
            (kc_ref, pek_ref, k1_ref, k1f_ref, k2_ref, ko_ref),
            (vc_ref, pev_ref, v1_ref, v1f_ref, v2_ref, vo_ref)):
        z = jnp.zeros((nh, 2 * NSA_KV_GROUPS * hid), F32)
        for tok in range(CMP_STRIDE):
            z = z + _mm(x_ref[0, pl.ds(tok, nh, stride=CMP_STRIDE), :], w1_ref[tok])
        bias = _mm(jnp.broadcast_to(pe_ref[...], (8, pe_ref.shape[1])), w1f_ref[...])[0:1]
        for g in range(NSA_KV_GROUPS):
            zg = z[:, 2 * g * hid:2 * (g + 1) * hid]
            pre = zg[:, 0:hid] + pltpu.roll(zg[:, hid:2 * hid], nh - 1, 0) + bias
            o_ref[g] = _mm(_gelu_tanh(pre), w2_ref[...])


def _nsa_cmp(kc, vc, pe_k, pe_v, k1, k1f, k2d, v1, v1f, v2d):
    b, t, _ = kc.shape
    nh = t // CMP_STRIDE
    tile_in = lambda: pl.BlockSpec((1, t, LANES), lambda i: (i, 0, 0))
    tile_out = lambda: pl.BlockSpec((NSA_KV_GROUPS, nh, LANES), lambda i: (i, 0, 0))
    out = jax.ShapeDtypeStruct((b * NSA_KV_GROUPS, nh, LANES), F32)
    consts = (pe_k, pe_v, k1, k1f, k2d, v1, v1f, v2d)
    return pl.pallas_call(
        _nsa_cmp_kernel,
        grid=(b,),
        in_specs=[tile_in(), tile_in()] + [_const_spec(c.shape) for c in consts],
        out_specs=[tile_out(), tile_out()],
        out_shape=[out, out],
        compiler_params=_cparams("parallel"),
        name="nsa_cmp",
    )(kc, vc, pe_k, pe_v, k1, k1f, k2d, v1, v1f, v2d)


NSA_TQ = 256
NSA_TK = 256
NSA_LOOKAHEAD = 1


def _rope_swap(x):
    lane = lax.broadcasted_iota(jnp.int32, (1, LANES), 1)
    first = (lane % HEAD_DIM) < (HEAD_DIM // 2)
    return jnp.where(first, pltpu.roll(x, LANES - HEAD_DIM // 2, 1), pltpu.roll(x, HEAD_DIM // 2, 1))


def _masked_softmax_parts(s, mask):
    s = jnp.where(mask, s, NEG_BIG)
    e = jnp.where(mask, jnp.exp(s - jnp.max(s, axis=-1, keepdims=True)), 0.0)
    return e / jnp.maximum(jnp.sum(e, axis=-1, keepdims=True), 1e-30)


NSA_ACC_ROWS = HEAD_DIM + 16


def _nsa_attn_kernel(q_ref, gn_ref, ks_ref, vs_ref, kw_ref, vw_ref, kc_ref, vc_ref,
                     cosq_ref, sinq_ref, cosk_ref, sink_ref, ovt_ref, ext_ref,
                     o_ref,
                     k2_sc, vt_sc, qrot_sc, sel_sc, m_sc, acc_sc):
    i = pl.program_id(1)
    tq, tk = NSA_TQ, NSA_TK
    t = ks_ref.shape[1]
    n_kt = t // tk
    n_sel = t // SEL_BLOCK
    lane = lax.broadcasted_iota(jnp.int32, (1, LANES), 1)
    lo_half = lane < HEAD_DIM

    @pl.when(i == 0)
    def _():
        ones = jnp.ones((NSA_ACC_ROWS - HEAD_DIM, tk), BF16)
        for jt in range(n_kt):
            rs = slice(jt * tk, (jt + 1) * tk)
            cos, sin = cosk_ref[rs, :], sink_ref[rs, :]
            for br, k_src, v_src in ((0, ks_ref, vs_ref), (1, kw_ref, vw_ref)):
                x = k_src[0, rs, :].astype(F32)
                x = x * cos + _rope_swap(x) * sin
                xr = pltpu.roll(x, HEAD_DIM, 1)
                k2_sc[br, 0, rs, :] = jnp.where(lo_half, x, xr).astype(BF16)
                k2_sc[br, 1, rs, :] = jnp.where(lo_half, xr, x).astype(BF16)
                v_t = v_src[0, rs, :].astype(F32).T.astype(BF16)
                for g in range(NSA_KV_GROUPS):
                    vt_sc[br, g, jt, 0:HEAD_DIM, :] = v_t[g * HEAD_DIM:(g + 1) * HEAD_DIM]
                    vt_sc[br, g, jt, HEAD_DIM:NSA_ACC_ROWS, :] = ones

    tq_row = i * tq + lax.broadcasted_iota(jnp.int32, (1, tq), 1)
    tq4_row = jnp.concatenate([tq_row] * 4, axis=1)
    gates_t = _sigmoid(gn_ref[0]).T
    cosq, sinq = cosq_ref[...], sinq_ref[...]
    qscale = HEAD_DIM ** -0.5 * LOG2E
    nt_dims = (((1,), (1,)), ((), ()))
    cols = lambda hh: slice(hh * tq, (hh + 1) * tq)

    def stack_heads(xa, xb):
        parts = [jnp.where(lo_half, xa, 0.0), jnp.where(lo_half, 0.0, xa),
                 jnp.where(lo_half, xb, 0.0), jnp.where(lo_half, 0.0, xb)]
        return (jnp.concatenate(parts, axis=0) * qscale).astype(BF16)

    j_last = (i * tq + tq - 1) // tk
    j_first_win = jnp.maximum(i * tq - (WINDOW - 1), 0) // tk
    blk = lax.broadcasted_iota(jnp.int32, (n_sel, 1), 0)
    cur = tq_row // SEL_BLOCK
    forced = (blk == 0) | (blk == cur) | (blk == cur - 1)
    allowed = blk <= cur
    n_valid = (t - CMP_BLOCK) // CMP_STRIDE + 1
    cblk = lax.broadcasted_iota(jnp.int32, (LANES, 1), 0)
    cmask = (cblk * CMP_STRIDE + (CMP_BLOCK - 1) <= tq4_row) & (cblk < n_valid)

    o_cmp = []
    for g in range(NSA_KV_GROUPS):
        qa = q_ref[0, :, (2 * g) * LANES:(2 * g + 1) * LANES].astype(F32)
        qb = q_ref[0, :, (2 * g + 1) * LANES:(2 * g + 2) * LANES].astype(F32)
        q_cmp = stack_heads(qa, qb)
        qrot_sc[g] = stack_heads(qa * cosq + _rope_swap(qa) * sinq, qb * cosq + _rope_swap(qb) * sinq)

        s_c = lax.dot_general(kc_ref[g].astype(BF16), q_cmp, nt_dims, preferred_element_type=F32)
        s_c = jnp.where(cmask, s_c, NEG_BIG)
        e_c = jnp.where(cmask, jnp.exp2(s_c - jnp.max(s_c, axis=0, keepdims=True)), 0.0)
        p_c = e_c / jnp.maximum(jnp.sum(e_c, axis=0, keepdims=True), 1e-30)
        o_cmp.append(_mm(vc_ref[g].T[0:HEAD_DIM], p_c))

        p_sum = p_c[:, cols(0)] + p_c[:, cols(1)] + p_c[:, cols(2)] + p_c[:, cols(3)]
        imp = sum(jnp.dot(ovt_ref[...], part, preferred_element_type=F32) for part in _split3(p_sum))
        imp = jnp.where(forced, 1e4, jnp.where(allowed, imp, -1.0))
        rank = jnp.zeros((n_sel, tq), F32)
        for b in range(n_sel):
            row_b = imp[b:b + 1, :]
            rank = rank + jnp.where((row_b > imp) | ((row_b == imp) & (b < blk)), 1.0, 0.0)
        sel_sc[g] = jnp.where(rank < min(SEL_TOPK, n_sel), 1.0, 0.0).astype(BF16)

    for ch in range(2 * NSA_KV_GROUPS):
        m_sc[ch] = jnp.full(m_sc.shape[1:], NEG_BIG, F32)
        acc_sc[ch] = jnp.zeros(acc_sc.shape[1:], F32)

    def run(items):
        def scores(j, pos0, br, g):
            r0 = pl.multiple_of(j * tk, tk)
            kpos = pos0 + lax.broadcasted_iota(jnp.int32, (tk, 1), 0)
            s = lax.dot_general(k2_sc[br, g, pl.ds(r0, tk), :], qrot_sc[g], nt_dims,
                                preferred_element_type=F32)
            if br == 0:
                hit = jnp.dot(ext_ref[pl.ds(r0, tk), :], sel_sc[g], preferred_element_type=F32)
                keep = (kpos <= tq_row) & (hit > 0.5)
            else:
                keep = (kpos <= tq_row) & (kpos > tq_row - WINDOW)
            return s + jnp.concatenate([jnp.where(keep, 0.0, NEG_BIG)] * 4, axis=1)

        pending = [scores(*it) for it in items[:NSA_LOOKAHEAD]]
        for n, (j, _, br, g) in enumerate(items):
            if n + NSA_LOOKAHEAD < len(items):
                pending.append(scores(*items[n + NSA_LOOKAHEAD]))
            s_cur = pending.pop(0)
            ch = br * NSA_KV_GROUPS + g
            m_old = m_sc[ch]
            m_new = jnp.maximum(m_old, jnp.max(s_cur, axis=0, keepdims=True))
            p = jnp.exp2(s_cur - m_new).astype(BF16)
            m_sc[ch] = m_new
            acc_sc[ch] = jnp.exp2(m_old - m_new) * acc_sc[ch] + jnp.dot(
                vt_sc[br, g, j], p, preferred_element_type=F32)

    def sel_only(j, carry):
        run([(j, j * tk, 0, g) for g in range(NSA_KV_GROUPS)])
        return carry

    def sel_and_win(j, carry):
        run([(j, j * tk, br, g) for g in range(NSA_KV_GROUPS) for br in (0, 1)])
        return carry

    lax.fori_loop(0, j_first_win, sel_only, 0)
    lax.fori_loop(j_first_win, j_last + 1, sel_and_win, 0)

    def finish(ch):
        acc = acc_sc[ch]
        out = acc[0:HEAD_DIM] / jnp.maximum(acc[HEAD_DIM:HEAD_DIM + 1], 1e-30)
        return jnp.where(m_sc[ch] > 0.5 * NEG_BIG, out, 0.0)

    for g in range(NSA_KV_GROUPS):
        o_sel, o_win = finish(g), finish(NSA_KV_GROUPS + g)

        def head_out(hh):
            r = (4 * g + hh) * 3
            return (gates_t[r:r + 1] * o_cmp[g][:, cols(hh)] + gates_t[r + 1:r + 2] * o_sel[:, cols(hh)]
                    + gates_t[r + 2:r + 3] * o_win[:, cols(hh)])

        for pr in range(2):
            o_ref[0, :, (2 * g + pr) * LANES:(2 * g + pr + 1) * LANES] = jnp.concatenate(
                [head_out(2 * pr), head_out(2 * pr + 1)], axis=0).T.astype(BF16)


def _nsa_attn(q, gn, kv, kc2, vc2, cos, sin, ovt, ex):
    b, t, _ = q.shape
    tq, tk = NSA_TQ, NSA_TK
    g = NSA_KV_GROUPS
    assert t % tq == 0 and t % tk == 0
    full = lambda col: pl.BlockSpec((1, t, LANES), lambda bi, i: (bi, 0, col))
    return pl.pallas_call(
        _nsa_attn_kernel,
        grid=(b, t // tq),
        in_specs=[pl.BlockSpec((1, tq, NSA_WIDTH), lambda bi, i: (bi, i, 0)),
                  pl.BlockSpec((1, tq, LANES), lambda bi, i: (bi, i, 0)),
                  full(0), full(1), full(2), full(3),
                  pl.BlockSpec((g, kc2.shape[1], LANES), lambda bi, i: (bi, 0, 0)),
                  pl.BlockSpec((g, vc2.shape[1], LANES), lambda bi, i: (bi, 0, 0)),
                  pl.BlockSpec((tq, LANES), lambda bi, i: (i, 0)),
                  pl.BlockSpec((tq, LANES), lambda bi, i: (i, 0)),
                  _const_spec(cos.shape), _const_spec(sin.shape),
                  _const_spec(ovt.shape), _const_spec(ex.shape)],
        out_specs=pl.BlockSpec((1, tq, NSA_WIDTH), lambda bi, i: (bi, i, 0)),
        out_shape=jax.ShapeDtypeStruct((b, t, NSA_WIDTH), BF16),
        scratch_shapes=[pltpu.VMEM((2, g, t, LANES), BF16),
                        pltpu.VMEM((2, g, t // tk, NSA_ACC_ROWS, tk), BF16),
                        pltpu.VMEM((g, 4 * tq, LANES), BF16),
                        pltpu.VMEM((g, t // SEL_BLOCK, tq), BF16),
                        pltpu.VMEM((2 * g, 1, 4 * tq), F32),
                        pltpu.VMEM((2 * g, NSA_ACC_ROWS, 4 * tq), F32)],
        compiler_params=_cparams("parallel", "arbitrary"),
        name="nsa_attn",
    )(q, gn, kv, kv, kv, kv, kc2, vc2, cos, sin, cos, sin, ovt, ex)


def _merge_kernel(x_ref, yn_ref, bonus_ref, g_ref, lng_ref, lnb_ref, on_ref, gm_ref,
                  wrw_ref, wnsa_ref, wout_ref, o_ref):
    d = x_ref.shape[1]
    y_rw = ((yn_ref[...].astype(F32) * lng_ref[...] + lnb_ref[...] + bonus_ref[...].astype(F32))
            * g_ref[...].astype(F32))
    t_rw = _mm(y_rw, wrw_ref[...])
    t_ns = _mm(on_ref[...], wnsa_ref[...])
    gm = gm_ref[...].astype(F32)
    mix = _sigmoid(gm[:, 0:d]) * t_rw + _sigmoid(gm[:, d:2 * d]) * t_ns
    o_ref[...] = x_ref[...] + _mm(mix, wout_ref[...])


def _merge_out(x2, yn, bonus, g, ln_g, ln_b, o_nsa, gm, w_rw, w_nsa, w_out, tm):
    n, d = x2.shape
    row = lambda w: pl.BlockSpec((tm, w), lambda i: (i, 0))
    return pl.pallas_call(
        _merge_kernel,
        grid=(n // tm,),
        in_specs=[row(d), row(RW_WIDTH), row(RW_WIDTH), row(RW_WIDTH),
                  _const_spec(ln_g.shape), _const_spec(ln_b.shape),
                  row(NSA_WIDTH), row(2 * d),
                  _const_spec(w_rw.shape), _const_spec(w_nsa.shape), _const_spec(w_out.shape)],
        out_specs=row(d),
        out_shape=jax.ShapeDtypeStruct((n, d), F32),
        compiler_params=_cparams("parallel"),
        name="merge_out",
    )(x2, yn, bonus, g, ln_g, ln_b, o_nsa, gm, w_rw, w_nsa, w_out)


def _xattn_kernel(h_ref, g_ref, wq_ref, mem_ref, gm_ref, wkv_ref, wo_ref, o_ref, kv_sc):
    d = h_ref.shape[2]
    hd = d // XA_HEADS

    @pl.when(pl.program_id(1) == 0)
    def _():
        kv_sc[...] = _mm(_rms(mem_ref[0], gm_ref[...]), wkv_ref[...]).astype(BF16)

    h = h_ref[0]
    q = (_mm(_rms(h, g_ref[...]), wq_ref[...]) * hd ** -0.5).astype(BF16)
    def scores(hh):
        cs = slice(hh * hd, (hh + 1) * hd)
        return lax.dot_general(q[:, cs], kv_sc[:, cs], (((1,), (1,)), ((), ())), preferred_element_type=F32)

    outs = []
    s_next = scores(0)
    for hh in range(XA_HEADS):
        s, s_next = s_next, (scores(hh + 1) if hh + 1 < XA_HEADS else None)
        e = jnp.exp(s - jnp.max(s, axis=-1, keepdims=True))
        p = e * (1.0 / jnp.sum(e, axis=-1, keepdims=True))
        outs.append(jnp.dot(p.astype(BF16), kv_sc[:, d + hh * hd:d + (hh + 1) * hd],
                            preferred_element_type=F32))
    o_ref[0] = h + _mm(jnp.concatenate(outs, axis=1), wo_ref[...])


def _xattn(h3, g, wq, mem, g_mem, wkv, wo, tm):
    b, t, d = h3.shape
    m = mem.shape[1]
    return pl.pallas_call(
        _xattn_kernel,
        grid=(b, t // tm),
        in_specs=[pl.BlockSpec((1, tm, d), lambda bi, i: (bi, i, 0)), _const_spec(g.shape),
                  _const_spec(wq.shape),
                  pl.BlockSpec((1, m, d), lambda bi, i: (bi, 0, 0)), _const_spec(g_mem.shape),
                  _const_spec(wkv.shape), _const_spec(wo.shape)],
        out_specs=pl.BlockSpec((1, tm, d), lambda bi, i: (bi, i, 0)),
        out_shape=jax.ShapeDtypeStruct((b, t, d), F32),
        scratch_shapes=[pltpu.VMEM((m, 2 * d), BF16)],
        compiler_params=_cparams("parallel", "arbitrary"),
        name="xattn",
    )(h3, g, wq, mem, g_mem, wkv, wo)


FFN_CHUNK = 256


def _ffn_kernel(h_ref, g_ref, wgu_ref, wd_ref, gf_ref, o_ref):
    h = h_ref[...]
    hn = _rms(h, g_ref[...]).astype(BF16)
    dff = wd_ref.shape[0]
    acc = jnp.zeros(h.shape, F32)
    for c0 in range(0, dff, FFN_CHUNK):
        gate = jnp.dot(hn, wgu_ref[:, c0:c0 + FFN_CHUNK], preferred_element_type=F32)
        up = jnp.dot(hn, wgu_ref[:, dff + c0:dff + c0 + FFN_CHUNK], preferred_element_type=F32)
        act = (gate * _sigmoid(gate) * up).astype(BF16)
        acc = acc + jnp.dot(act, wd_ref[c0:c0 + FFN_CHUNK, :], preferred_element_type=F32)
    o_ref[...] = _rms(h + acc, gf_ref[...])


def _ffn(h2, g, wgu, wd, gf, tm):
    n, d = h2.shape
    assert wd.shape[0] % FFN_CHUNK == 0
    return pl.pallas_call(
        _ffn_kernel,
        grid=(n // tm,),
        in_specs=[pl.BlockSpec((tm, d), lambda i: (i, 0)), _const_spec(g.shape),
                  _const_spec(wgu.shape), _const_spec(wd.shape), _const_spec(gf.shape)],
        out_specs=pl.BlockSpec((tm, d), lambda i: (i, 0)),
        out_shape=jax.ShapeDtypeStruct((n, d), F32),
        compiler_params=_cparams("parallel"),
        name="ffn",
    )(h2, g, wgu, wd, gf)


def _head_sum_matrix():
    idx = np.arange(LANES) // HEAD_DIM
    return jnp.asarray(idx[:, None] == idx[None, :], BF16)


def _prefix_matrix():
    L = RW_CHUNK
    tri = np.tril(np.ones((L, L), np.float32))
    return jnp.asarray(np.concatenate([tri, np.ones((16, L), np.float32)], axis=0), BF16)


def _rope_tables(t):
    half = HEAD_DIM // 2
    inv_freq = ROPE_THETA ** (-np.arange(half, dtype=np.float64) / half)
    ang = np.arange(t, dtype=np.float64)[:, None] * inv_freq[None, :]
    cos, sin = np.cos(ang), np.sin(ang)
    cos128 = np.tile(cos, (1, LANES // half))
    sin128 = np.tile(np.concatenate([-sin, sin], axis=1), (1, LANES // HEAD_DIM))
    return jnp.asarray(cos128, F32), jnp.asarray(sin128, F32)


def _overlap_matrix(t):
    n_cmp = (t - CMP_BLOCK) // CMP_STRIDE + 1
    n_sel = t // SEL_BLOCK
    cs = np.arange(n_cmp) * CMP_STRIDE
    ss = np.arange(n_sel) * SEL_BLOCK
    ov = np.clip(np.minimum(cs[:, None] + CMP_BLOCK, ss[None, :] + SEL_BLOCK)
                 - np.maximum(cs[:, None], ss[None, :]), 0, None) / CMP_BLOCK
    out = np.zeros((n_sel, LANES), np.float32)
    out[:, :n_cmp] = ov.T
    return jnp.asarray(out, BF16)


def _expand_matrix(t):
    blk = np.arange(t) // SEL_BLOCK
    return jnp.asarray(blk[:, None] == np.arange(t // SEL_BLOCK)[None, :], BF16)


def kernel(x, mem, norm_mix_g, w_in, shift_mu, rw_w_up, rw_w0, rw_a_up, rw_a0, rw_g_up, rw_k_k, rw_k_a,
           rw_r_k, rw_ln_g, rw_ln_b, nsa_pe_k, nsa_pe_v, nsa_ck1, nsa_ck2, nsa_cv1, nsa_cv2, w_up_rw,
           w_up_nsa, w_out, norm_xa_g, norm_mem_g, xa_wq, xa_wkv, xa_wo, norm_ffn_g, ffn_w_gu,
           ffn_w_down, final_norm_g):
    b, t, d = x.shape
    n = b * t
    h = x.reshape(n, d)
    hsum = _head_sum_matrix()
    tri = _prefix_matrix()
    cos, sin = _rope_tables(t)
    ov = _overlap_matrix(t)
    ex = _expand_matrix(t)
    row = lambda a: a.reshape(1, -1)
    n_half = t // CMP_STRIDE
    assert t // SEL_BLOCK <= LANES and n_half <= LANES

    for l in range(w_in.shape[0]):
        c_g = RW_IN + NSA_WIDTH + 6 * KV_WIDTH
        n_gate = 3 * NSA_HEADS
        w_gate = jnp.pad(w_in[l][:, c_g:c_g + n_gate], ((0, 0), (0, LANES - n_gate))).astype(BF16)
        p_rw, q, kc, vc, kv, gn, gm = _in_proj(h, row(norm_mix_g[l]), w_in[l][:, :c_g].astype(BF16), w_gate,
                                               w_in[l][:, c_g + n_gate:].astype(BF16), 512)

        z64 = jnp.zeros((DECAY_LORA, RW_WIDTH), F32)
        wlora = jnp.concatenate([jnp.concatenate([rw_w_up[l], z64], axis=1),
                                 jnp.concatenate([z64, rw_a_up[l]], axis=1)], axis=0).astype(BF16)
        prep = _rwkv_prep(p_rw.reshape(b, t, RW_IN), row(shift_mu[l]), wlora, row(rw_w0[l]),
                          row(rw_a0[l]), rw_g_up[l].astype(BF16), row(rw_k_k[l]), row(rw_k_a[l]),
                          row(rw_r_k[l]), tri, hsum, 512)
        abar, rbar, btil, ktil, v_rw, gam, g_rw, bonus = prep
        yn = _rwkv_scan(abar, rbar, btil, ktil, v_rw, gam)

        def w1_tokens(w1):
            hw = CMP_STRIDE * HEAD_DIM
            per_tok = jnp.concatenate([w1[:hw].reshape(CMP_STRIDE, HEAD_DIM, -1),
                                       w1[hw:].reshape(CMP_STRIDE, HEAD_DIM, -1)], axis=2)
            zero = jnp.zeros_like(per_tok)
            return jnp.concatenate([jnp.concatenate([per_tok, zero], axis=2),
                                    jnp.concatenate([zero, per_tok], axis=2)], axis=1).astype(BF16)

        dup = lambda w2: jnp.concatenate([w2, w2], axis=1).astype(BF16)
        kc2, vc2 = _nsa_cmp(kc.reshape(b, t, KV_WIDTH), vc.reshape(b, t, KV_WIDTH),
                            nsa_pe_k[l].reshape(1, -1), nsa_pe_v[l].reshape(1, -1),
                            w1_tokens(nsa_ck1[l]), nsa_ck1[l].astype(BF16), dup(nsa_ck2[l]),
                            w1_tokens(nsa_cv1[l]), nsa_cv1[l].astype(BF16), dup(nsa_cv2[l]))
        o_nsa = _nsa_attn(q.reshape(b, t, NSA_WIDTH), gn.reshape(b, t, LANES),
                          kv.reshape(b, t, 4 * KV_WIDTH), kc2, vc2, cos, sin, ov, ex)

        h = _merge_out(h, yn.reshape(n, RW_WIDTH), bonus.reshape(n, RW_WIDTH), g_rw.reshape(n, RW_WIDTH),
                       row(rw_ln_g[l]), row(rw_ln_b[l]), o_nsa.reshape(n, NSA_WIDTH), gm,
                       w_up_rw[l].astype(BF16), w_up_nsa[l].astype(BF16), w_out[l].astype(BF16), 512)

        h = _xattn(h.reshape(b, t, d), row(norm_xa_g[l]), xa_wq[l].astype(BF16), mem, row(norm_mem_g[l]),
                   xa_wkv[l].astype(BF16), xa_wo[l].astype(BF16), 512).reshape(n, d)

        last = l == w_in.shape[0] - 1
        gf = row(final_norm_g) if last else None
        assert last, "only a single layer is fused with the final norm"
        h = _ffn(h, row(norm_ffn_g[l]), ffn_w_gu[l].astype(BF16), ffn_w_down[l].astype(BF16), gf, 512)
    return h.reshape(b, t, d)
```

```python
import functools
import math

import numpy as np
import jax
import jax.numpy as jnp
from jax import lax
from jax.experimental import pallas as pl
from jax.experimental.pallas import tpu as pltpu

F32 = jnp.float32
BF16 = jnp.bfloat16

HEAD_DIM = 64
NORM_EPS = 1e-6
ROPE_THETA = 10000.0
RW_HEADS = 8
RW_WIDTH = RW_HEADS * HEAD_DIM
DECAY_LORA = 64
AAA_LORA = 64
GATE_LORA = 128
RW_GN_EPS = 64e-5
RW_IN = 3 * RW_WIDTH + DECAY_LORA + AAA_LORA + GATE_LORA
NSA_HEADS = 8
NSA_KV_GROUPS = 2
NSA_WIDTH = NSA_HEADS * HEAD_DIM
KV_WIDTH = NSA_KV_GROUPS * HEAD_DIM
CMP_BLOCK = 32
CMP_STRIDE = 16
CMP_HIDDEN = 128
SEL_BLOCK = 64
SEL_TOPK = 8
WINDOW = 512
XA_HEADS = 4

LANES = 128
VMEM_LIMIT = 56 * 1024 * 1024
RW_CHUNK = 64
NEG_BIG = -1e30
LOG2E = 1.4426950408889634


def _cparams(*sem):
    return pltpu.CompilerParams(dimension_semantics=sem, vmem_limit_bytes=VMEM_LIMIT)


def _mm(a, b):
    return jnp.dot(a.astype(BF16), b.astype(BF16), preferred_element_type=F32)


def _mm_nt(a, b):
    return lax.dot_general(a.astype(BF16), b.astype(BF16), (((1,), (1,)), ((), ())),
                           preferred_element_type=F32)


def _mm_tn(a, b):
    return lax.dot_general(a.astype(BF16), b.astype(BF16), (((0,), (0,)), ((), ())),
                           preferred_element_type=F32)


def _split(x, parts):
    out = []
    for _ in range(parts - 1):
        piece = x.astype(BF16)
        out.append(piece)
        x = x - piece.astype(F32)
    return out + [x.astype(BF16)]


def _split3(x):
    return _split(x, 3)


def _mm_exact_rhs(x, m_bf16, parts=3):
    return sum(jnp.dot(p, m_bf16, preferred_element_type=F32) for p in _split(x, parts))


def _head_sums(x, pair_ones, parts):
    return jnp.concatenate([_mm_exact_rhs(x[:, c:c + LANES], pair_ones, parts)
                            for c in range(0, x.shape[1], LANES)], axis=1)


def _mm_exact_lhs(m_bf16, x, parts=3):
    return sum(jnp.dot(m_bf16, p, preferred_element_type=F32) for p in _split(x, parts))


def _rms(x, g):
    return x * lax.rsqrt(jnp.mean(x * x, axis=-1, keepdims=True) + NORM_EPS) * g


def _sigmoid(x):
    return 1.0 / (1.0 + jnp.exp(-x))


def _const_spec(shape):
    nd = len(shape)
    return pl.BlockSpec(shape, lambda *_: (0,) * nd)


def _in_proj_kernel(x_ref, g_ref, w_main_ref, w_gate_ref, w_merge_ref,
                    mu_ref, wlora_ref, w0_ref, a0_ref, gup_ref, kk_ref, ka_ref, rk_ref, tri_ref, hsum_ref,
                    q_ref, kc_ref, vc_ref, kv_ref, gn_ref, gm_ref,
                    abar_ref, rbar_ref, btil_ref, ktil_ref, v_ref, gam_ref, gate_ref, bonus_ref,
                    p_sc, last_sc, *, rows_per_seq):
    i = pl.program_id(0)
    tm = x_ref.shape[0]
    w = RW_WIDTH
    hn = _rms(x_ref[...], g_ref[...]).astype(BF16)

    def project(o_ref, w_ref, base, c0, c1):
        o_ref[:, c0:c1] = jnp.dot(hn, w_ref[:, base + c0:base + c1],
                                  preferred_element_type=F32).astype(o_ref.dtype)

    def project_all(o_ref, w_ref, base):
        width = o_ref.shape[1]
        for c0 in range(0, width, 512):
            project(o_ref, w_ref, base, c0, min(c0 + 512, width))

    project_all(p_sc, w_main_ref, 0)
    off = RW_IN
    for o_ref in (q_ref, kc_ref, vc_ref, kv_ref):
        project_all(o_ref, w_main_ref, off)
        off += o_ref.shape[1]
    project_all(gn_ref, w_gate_ref, 0)

    cur = p_sc[...]
    prev_row = jnp.where((i * tm) % rows_per_seq == 0, 0.0, last_sc[7:8, :])
    row = lax.broadcasted_iota(jnp.int32, (tm, 1), 0)
    prev = jnp.where(row == 0, prev_row, pltpu.roll(cur, 1, 0))
    p = cur + mu_ref[...] * (prev - cur)
    last_sc[7:8, :] = cur[tm - 1:tm, :]

    r = p[:, 0:w]
    k = p[:, w:2 * w]
    v = p[:, 2 * w:3 * w]
    x_wa = p[:, 3 * w:3 * w + LANES]
    x_g = p[:, 3 * w + LANES:3 * w + 2 * LANES]
    lane = lax.broadcasted_iota(jnp.int32, (1, LANES), 1)
    x_wa = jnp.where(lane < DECAY_LORA, jnp.tanh(x_wa), x_wa)
    lin = _mm(x_wa, wlora_ref[...])
    w_lin = w0_ref[...] + lin[:, 0:w]
    a = _sigmoid(a0_ref[...] + lin[:, w:2 * w])
    gate_ref[...] = _mm(_sigmoid(x_g), gup_ref[...]).astype(BF16)

    logw = _sigmoid(w_lin) * (-math.exp(-0.5) * LOG2E)

    hsum = hsum_ref[...]
    kk = k * kk_ref[...]
    kk = kk * lax.rsqrt(jnp.maximum(_head_sums(kk * kk, hsum, 1), 1e-12))
    k2 = k * (1.0 + (a - 1.0) * ka_ref[...])
    bonus_ref[...] = (_head_sums(r * k2 * rk_ref[...], hsum, 1) * v).astype(BF16)
    v_ref[...] = v.astype(BF16)

    half = gm_ref.shape[1] // 2
    for c0 in range(0, half, 512):
        project(gm_ref, w_merge_ref, 0, c0, c0 + 512)

    L = RW_CHUNK
    tri = tri_ref[...]
    kka = kk * a
    gam_rows = []
    for c in range(tm // L):
        sl = slice(c * L, (c + 1) * L)
        cs = _mm_exact_lhs(tri, logw[sl], 2)
        cum, gam = cs[0:L], jnp.exp2(cs[L:L + 1])
        e_cum = jnp.exp2(cum)
        e_neg = 1.0 / e_cum
        abar_ref[sl, :] = (-kk[sl] * jnp.exp2(cum - logw[sl])).astype(BF16)
        rbar_ref[sl, :] = (r[sl] * e_cum).astype(BF16)
        btil_ref[sl, :] = (kka[sl] * e_neg).astype(BF16)
        ktil_ref[sl, :] = (k2[sl] * e_neg).astype(BF16)
        gam_rows.append(gam)
    gam_ref[...] = jnp.concatenate(gam_rows, axis=0)

    for c0 in range(half, 2 * half, 512):
        project(gm_ref, w_merge_ref, 0, c0, c0 + 512)


def _in_proj(x2, g, w_main, w_gate, w_merge, prep_consts, tm, rows_per_seq):
    n, d = x2.shape
    w = RW_WIDTH
    widths = (NSA_WIDTH, KV_WIDTH, KV_WIDTH, 4 * KV_WIDTH, LANES, 2 * d)
    dtypes = (BF16, F32, F32, BF16, F32, BF16)
    assert RW_IN + sum(widths[:-2]) == w_main.shape[1] and widths[-2:] == (w_gate.shape[1], w_merge.shape[1])
    assert n % tm == 0 and rows_per_seq % tm == 0 and tm % (8 * RW_CHUNK) == 0 and (2 * d) % 1024 == 0
    rows = lambda width: pl.BlockSpec((tm, width), lambda i: (i, 0))
    seq_bf = jax.ShapeDtypeStruct((n, w), BF16)
    return pl.pallas_call(
        functools.partial(_in_proj_kernel, rows_per_seq=rows_per_seq),
        grid=(n // tm,),
        in_specs=[rows(d), _const_spec((1, d)),
                  _const_spec(w_main.shape), _const_spec(w_gate.shape), _const_spec(w_merge.shape)]
        + [_const_spec(c.shape) for c in prep_consts],
        out_specs=[rows(wd) for wd in widths] + [rows(w)] * 5
        + [pl.BlockSpec((tm // RW_CHUNK, w), lambda i: (i, 0)), rows(w), rows(w)],
        out_shape=[jax.ShapeDtypeStruct((n, wd), dt) for wd, dt in zip(widths, dtypes)] + [seq_bf] * 5
        + [jax.ShapeDtypeStruct((n // RW_CHUNK, w), F32), seq_bf, seq_bf],
        scratch_shapes=[pltpu.VMEM((tm, RW_IN), F32), pltpu.VMEM((8, RW_IN), F32)],
        compiler_params=_cparams("arbitrary"),
        name="in_proj",
    )(x2, g, w_main, w_gate, w_merge, *prep_consts)


def _rwkv_scan_kernel(abar_ref, rbar_ref, btil_ref, ktil_ref, v_ref, gam_ref,
                      y_ref, s_ref):
    c = pl.program_id(1)
    L = RW_CHUNK

    @pl.when(c == 0)
    def _():
        s_ref[...] = jnp.zeros_like(s_ref)

    lane = lax.broadcasted_iota(jnp.int32, (1, LANES), 1)
    m0 = lane < HEAD_DIM
    ri = lax.broadcasted_iota(jnp.int32, (2 * L, 2 * L), 0)
    ci = lax.broadcasted_iota(jnp.int32, (2 * L, 2 * L), 1)
    same = (ri // L) == (ci // L)
    low_strict = same & (ci < ri)
    low_incl = same & (ci <= ri)
    zero = jnp.zeros((), BF16)
    nb = abar_ref.shape[0]
    n_pair = RW_HEADS // 2

    def stack(ref, bi, ls):
        x = ref[bi, :, ls]
        return jnp.concatenate([jnp.where(m0, x, zero), jnp.where(m0, zero, x)], axis=0)

    chains = [(bi, pr) for bi in range(nb) for pr in range(n_pair)]
    nch = len(chains)
    lanes_of = lambda pr: slice(pr * LANES, (pr + 1) * LANES)

    x_all = [jnp.concatenate([stack(abar_ref, bi, lanes_of(pr)), stack(rbar_ref, bi, lanes_of(pr))], axis=0)
             for bi, pr in chains]
    vs = [stack(v_ref, bi, lanes_of(pr)) for bi, pr in chains]
    s0 = [s_ref[bi * n_pair + pr] for bi, pr in chains]
    bk = [jnp.concatenate([stack(btil_ref, bi, lanes_of(pr)), stack(ktil_ref, bi, lanes_of(pr))], axis=0)
          for bi, pr in chains]
    amat = [_mm_nt(x_all[i], bk[i]) for i in range(nch)]
    xs = [_mm_nt(x_all[i], s0[i]) for i in range(nch)]

    u = [xs[i][0:2 * L] + _mm(jnp.where(low_strict, amat[i][0:2 * L, 2 * L:4 * L], 0.0), vs[i])
         for i in range(nch)]
    pw = [jnp.where(low_strict, a[0:2 * L, 0:2 * L], 0.0).astype(BF16) for a in amat]
    n_sq = int(math.log2(L))
    for step in range(n_sq):
        u = [u[i] + _mm(pw[i], u[i]) for i in range(nch)]
        if step + 1 < n_sq:
            pw = [_mm(q, q).astype(BF16) for q in pw]

    uv = [jnp.concatenate([u[i].astype(BF16), vs[i]], axis=0) for i in range(nch)]
    gam_rows = [gam_ref[bi, pl.ds(c % 8, 1), :] for bi in range(nb)]
    for i, (bi, pr) in enumerate(chains):
        gam = gam_rows[bi][:, lanes_of(pr)]
        bk_hat = bk[i].astype(F32) * gam
        s_ref[bi * n_pair + pr] = s0[i] * gam + _mm_tn(uv[i], bk_hat)
    def head_mean(z):
        lo = jnp.sum(jnp.where(m0, z, 0.0), axis=-1, keepdims=True)
        hi = jnp.sum(jnp.where(m0, 0.0, z), axis=-1, keepdims=True)
        return jnp.where(m0, lo, hi) * (1.0 / HEAD_DIM)

    for i, (bi, pr) in enumerate(chains):
        a_r =jnp.concatenate([jnp.where(low_incl, amat[i][2 * L:4 * L, 0:2 * L], 0.0),
                               jnp.where(low_incl, amat[i][2 * L:4 * L, 2 * L:4 * L], 0.0)], axis=1)
        y2 = xs[i][2 * L:4 * L] + _mm(a_r, uv[i])
        y = y2[0:L] + y2[L:2 * L]
        d = y - head_mean(y)
        y_ref[bi, :, lanes_of(pr)] = (d * lax.rsqrt(head_mean(d * d) + RW_GN_EPS)).astype(BF16)


RW_SCAN_BATCH = 8


def _rwkv_scan(abar, rbar, btil, ktil, v, gam):
    b, t, w = abar.shape
    L = RW_CHUNK
    nb = RW_SCAN_BATCH
    assert b % nb == 0
    tile = lambda: pl.BlockSpec((nb, L, w), lambda bi, c: (bi, c, 0))
    return pl.pallas_call(
        _rwkv_scan_kernel,
        grid=(b // nb, t // L),
        in_specs=[tile(), tile(), tile(), tile(), tile(),
                  pl.BlockSpec((nb, 8, w), lambda bi, c: (bi, c // 8, 0))],
        out_specs=tile(),
        out_shape=jax.ShapeDtypeStruct((b, t, w), BF16),
        scratch_shapes=[pltpu.VMEM((nb * (RW_HEADS // 2), LANES, LANES), F32)],
        compiler_params=_cparams("parallel", "arbitrary"),
        name="rwkv_scan",
    )(abar, rbar, btil, ktil, v, gam)


def _gelu_tanh(x):
    return 0.5 * x * (1.0 + jnp.tanh(math.sqrt(2.0 / math.pi) * (x + 0.044715 * x * x * x)))


def _nsa_cmp_kernel(kc_ref, vc_ref, pek_ref, pev_ref, k1_ref, k1f_ref, k2_ref, v1_ref, v1f_ref, v2_ref,
                    ko_ref, vo_ref):
    nh = ko_ref.shape[1]
    hid = CMP_HIDDEN
    for x_ref, pe_ref, w1_ref, w1f_ref, w2_ref, o_ref in (
            (kc_ref, pek_ref, k1_ref, k1f_ref, k2_ref, ko_ref),
            (vc_ref, pev_ref, v1_ref, v1f_ref, v2_ref, vo_ref)):
        z = jnp.zeros((nh, 2 * NSA_KV_GROUPS * hid), F32)
        for tok in range(CMP_STRIDE):
            z = z + _mm(x_ref[0, pl.ds(tok, nh, stride=CMP_STRIDE), :], w1_ref[tok])
        bias = _mm(jnp.broadcast_to(pe_ref[...], (8, pe_ref.shape[1])), w1f_ref[...])[0:1]
        for g in range(NSA_KV_GROUPS):
            zg = z[:, 2 * g * hid:2 * (g + 1) * hid]
            pre = zg[:, 0:hid] + pltpu.roll(zg[:, hid:2 * hid], nh - 1, 0) + bias
            o_ref[g] = _mm(_gelu_tanh(pre), w2_ref[...])


def _nsa_cmp(kc, vc, pe_k, pe_v, k1, k1f, k2d, v1, v1f, v2d):
    b, t, _ = kc.shape
    nh = t // CMP_STRIDE
    tile_in = lambda: pl.BlockSpec((1, t, LANES), lambda i: (i, 0, 0))
    tile_out = lambda: pl.BlockSpec((NSA_KV_GROUPS, nh, LANES), lambda i: (i, 0, 0))
    out = jax.ShapeDtypeStruct((b * NSA_KV_GROUPS, nh, LANES), F32)
    consts = (pe_k, pe_v, k1, k1f, k2d, v1, v1f, v2d)
    return pl.pallas_call(
        _nsa_cmp_kernel,
        grid=(b,),
        in_specs=[tile_in(), tile_in()] + [_const_spec(c.shape) for c in consts],
        out_specs=[tile_out(), tile_out()],
        out_shape=[out, out],
        compiler_params=_cparams("parallel"),
        name="nsa_cmp",
    )(kc, vc, pe_k, pe_v, k1, k1f, k2d, v1, v1f, v2d)


NSA_TQ = 256
NSA_TK = 256
NSA_LOOKAHEAD = 1


def _rope_swap(x):
    lane = lax.broadcasted_iota(jnp.int32, (1, LANES), 1)
    first = (lane % HEAD_DIM) < (HEAD_DIM // 2)
    return jnp.where(first, pltpu.roll(x, LANES - HEAD_DIM // 2, 1), pltpu.roll(x, HEAD_DIM // 2, 1))


def _masked_softmax_parts(s, mask):
    s = jnp.where(mask, s, NEG_BIG)
    e = jnp.where(mask, jnp.exp(s - jnp.max(s, axis=-1, keepdims=True)), 0.0)
    return e / jnp.maximum(jnp.sum(e, axis=-1, keepdims=True), 1e-30)


NSA_ACC_ROWS = HEAD_DIM + 16


def _nsa_attn_kernel(q_ref, gn_ref, ks_ref, vs_ref, kw_ref, vw_ref, kc_ref, vc_ref,
                     cosq_ref, sinq_ref, cosk_ref, sink_ref, ovt_ref, ext_ref,
                     o_ref,
                     k2_sc, vt_sc, qrot_sc, sel_sc, m_sc, acc_sc):
    i = pl.program_id(1)
    tq, tk = NSA_TQ, NSA_TK
    t = ks_ref.shape[1]
    n_kt = t // tk
    n_sel = t // SEL_BLOCK
    lane = lax.broadcasted_iota(jnp.int32, (1, LANES), 1)
    lo_half = lane < HEAD_DIM

    @pl.when(i == 0)
    def _():
        ones = jnp.ones((NSA_ACC_ROWS - HEAD_DIM, tk), BF16)
        for jt in range(n_kt):
            rs = slice(jt * tk, (jt + 1) * tk)
            cos, sin = cosk_ref[rs, :], sink_ref[rs, :]
            for br, k_src, v_src in ((0, ks_ref, vs_ref), (1, kw_ref, vw_ref)):
                x = k_src[0, rs, :].astype(F32)
                x = x * cos + _rope_swap(x) * sin
                xr = pltpu.roll(x, HEAD_DIM, 1)
                k2_sc[br, 0, rs, :] = jnp.where(lo_half, x, xr).astype(BF16)
                k2_sc[br, 1, rs, :] = jnp.where(lo_half, xr, x).astype(BF16)
                v_t = v_src[0, rs, :].astype(F32).T.astype(BF16)
                for g in range(NSA_KV_GROUPS):
                    vt_sc[br, g, jt, 0:HEAD_DIM, :] = v_t[g * HEAD_DIM:(g + 1) * HEAD_DIM]
                    vt_sc[br, g, jt, HEAD_DIM:NSA_ACC_ROWS, :] = ones

    tq_row = i * tq + lax.broadcasted_iota(jnp.int32, (1, tq), 1)
    tq4_row = jnp.concatenate([tq_row] * 4, axis=1)
    gates_t = _sigmoid(gn_ref[0]).T
    cosq, sinq = cosq_ref[...], sinq_ref[...]
    qscale = HEAD_DIM ** -0.5 * LOG2E
    nt_dims = (((1,), (1,)), ((), ()))
    cols = lambda hh: slice(hh * tq, (hh + 1) * tq)

    def stack_heads(xa, xb):
        parts = [jnp.where(lo_half, xa, 0.0), jnp.where(lo_half, 0.0, xa),
                 jnp.where(lo_half, xb, 0.0), jnp.where(lo_half, 0.0, xb)]
        return (jnp.concatenate(parts, axis=0) * qscale).astype(BF16)

    j_last = (i * tq + tq - 1) // tk
    j_first_win = jnp.maximum(i * tq - (WINDOW - 1), 0) // tk
    blk = lax.broadcasted_iota(jnp.int32, (n_sel, 1), 0)
    cur = tq_row // SEL_BLOCK
    forced = (blk == 0) | (blk == cur) | (blk == cur - 1)
    allowed = blk <= cur
    n_valid = (t - CMP_BLOCK) // CMP_STRIDE + 1
    cblk = lax.broadcasted_iota(jnp.int32, (LANES, 1), 0)
    cmask = (cblk * CMP_STRIDE + (CMP_BLOCK - 1) <= tq4_row) & (cblk < n_valid)

    o_cmp = []
    for g in range(NSA_KV_GROUPS):
        qa = q_ref[0, :, (2 * g) * LANES:(2 * g + 1) * LANES].astype(F32)
        qb = q_ref[0, :, (2 * g + 1) * LANES:(2 * g + 2) * LANES].astype(F32)
        q_cmp = stack_heads(qa, qb)
        qrot_sc[g] = stack_heads(qa * cosq + _rope_swap(qa) * sinq, qb * cosq + _rope_swap(qb) * sinq)

        s_c = lax.dot_general(kc_ref[g].astype(BF16), q_cmp, nt_dims, preferred_element_type=F32)
        s_c = jnp.where(cmask, s_c, NEG_BIG)
        e_c = jnp.where(cmask, jnp.exp2(s_c - jnp.max(s_c, axis=0, keepdims=True)), 0.0)
        p_c = e_c / jnp.maximum(jnp.sum(e_c, axis=0, keepdims=True), 1e-30)
        o_cmp.append(_mm(vc_ref[g].T[0:HEAD_DIM], p_c))

        p_sum = p_c[:, cols(0)] + p_c[:, cols(1)] + p_c[:, cols(2)] + p_c[:, cols(3)]
        imp = sum(jnp.dot(ovt_ref[...], part, preferred_element_type=F32) for part in _split3(p_sum))
        imp = jnp.where(forced, 1e4, jnp.where(allowed, imp, -1.0))
        rank = jnp.zeros((n_sel, tq), F32)
        for b in range(n_sel):
            row_b = imp[b:b + 1, :]
            rank = rank + jnp.where((row_b > imp) | ((row_b == imp) & (b < blk)), 1.0, 0.0)
        sel_sc[g] = jnp.where(rank < min(SEL_TOPK, n_sel), 1.0, 0.0).astype(BF16)

    for ch in range(2 * NSA_KV_GROUPS):
        m_sc[ch] = jnp.full(m_sc.shape[1:], NEG_BIG, F32)
        acc_sc[ch] = jnp.zeros(acc_sc.shape[1:], F32)

    def run(items):
        def scores(j, pos0, br, g):
            r0 = pl.multiple_of(j * tk, tk)
            kpos = pos0 + lax.broadcasted_iota(jnp.int32, (tk, 1), 0)
            s = lax.dot_general(k2_sc[br, g, pl.ds(r0, tk), :], qrot_sc[g], nt_dims,
                                preferred_element_type=F32)
            if br == 0:
                hit = jnp.dot(ext_ref[pl.ds(r0, tk), :], sel_sc[g], preferred_element_type=F32)
                keep = (kpos <= tq_row) & (hit > 0.5)
            else:
                keep = (kpos <= tq_row) & (kpos > tq_row - WINDOW)
            return s + jnp.concatenate([jnp.where(keep, 0.0, NEG_BIG)] * 4, axis=1)

        pending = [scores(*it) for it in items[:NSA_LOOKAHEAD]]
        for n, (j, _, br, g) in enumerate(items):
            if n + NSA_LOOKAHEAD < len(items):
                pending.append(scores(*items[n + NSA_LOOKAHEAD]))
            s_cur = pending.pop(0)
            ch = br * NSA_KV_GROUPS + g
            m_old = m_sc[ch]
            m_new = jnp.maximum(m_old, jnp.max(s_cur, axis=0, keepdims=True))
            p = jnp.exp2(s_cur - m_new).astype(BF16)
            m_sc[ch] = m_new
            acc_sc[ch] = jnp.exp2(m_old - m_new) * acc_sc[ch] + jnp.dot(
                vt_sc[br, g, j], p, preferred_element_type=F32)

    def sel_only(j, carry):
        run([(j, j * tk, 0, g) for g in range(NSA_KV_GROUPS)])
        return carry

    def sel_and_win(j, carry):
        run([(j, j * tk, br, g) for g in range(NSA_KV_GROUPS) for br in (0, 1)])
        return carry

    lax.fori_loop(0, j_first_win, sel_only, 0)
    lax.fori_loop(j_first_win, j_last + 1, sel_and_win, 0)

    def finish(ch):
        acc = acc_sc[ch]
        out = acc[0:HEAD_DIM] / jnp.maximum(acc[HEAD_DIM:HEAD_DIM + 1], 1e-30)
        return jnp.where(m_sc[ch] > 0.5 * NEG_BIG, out, 0.0)

    for g in range(NSA_KV_GROUPS):
        o_sel, o_win = finish(g), finish(NSA_KV_GROUPS + g)

        def head_out(hh):
            r = (4 * g + hh) * 3
            return (gates_t[r:r + 1] * o_cmp[g][:, cols(hh)] + gates_t[r + 1:r + 2] * o_sel[:, cols(hh)]
                    + gates_t[r + 2:r + 3] * o_win[:, cols(hh)])

        for pr in range(2):
            o_ref[0, :, (2 * g + pr) * LANES:(2 * g + pr + 1) * LANES] = jnp.concatenate(
                [head_out(2 * pr), head_out(2 * pr + 1)], axis=0).T.astype(BF16)


def _nsa_attn(q, gn, kv, kc2, vc2, cos, sin, ovt, ex):
    b, t, _ = q.shape
    tq, tk = NSA_TQ, NSA_TK
    g = NSA_KV_GROUPS
    assert t % tq == 0 and t % tk == 0
    full = lambda col: pl.BlockSpec((1, t, LANES), lambda bi, i: (bi, 0, col))
    return pl.pallas_call(
        _nsa_attn_kernel,
        grid=(b, t // tq),
        in_specs=[pl.BlockSpec((1, tq, NSA_WIDTH), lambda bi, i: (bi, i, 0)),
                  pl.BlockSpec((1, tq, LANES), lambda bi, i: (bi, i, 0)),
                  full(0), full(1), full(2), full(3),
                  pl.BlockSpec((g, kc2.shape[1], LANES), lambda bi, i: (bi, 0, 0)),
                  pl.BlockSpec((g, vc2.shape[1], LANES), lambda bi, i: (bi, 0, 0)),
                  pl.BlockSpec((tq, LANES), lambda bi, i: (i, 0)),
                  pl.BlockSpec((tq, LANES), lambda bi, i: (i, 0)),
                  _const_spec(cos.shape), _const_spec(sin.shape),
                  _const_spec(ovt.shape), _const_spec(ex.shape)],
        out_specs=pl.BlockSpec((1, tq, NSA_WIDTH), lambda bi, i: (bi, i, 0)),
        out_shape=jax.ShapeDtypeStruct((b, t, NSA_WIDTH), BF16),
        scratch_shapes=[pltpu.VMEM((2, g, t, LANES), BF16),
                        pltpu.VMEM((2, g, t // tk, NSA_ACC_ROWS, tk), BF16),
                        pltpu.VMEM((g, 4 * tq, LANES), BF16),
                        pltpu.VMEM((g, t // SEL_BLOCK, tq), BF16),
                        pltpu.VMEM((2 * g, 1, 4 * tq), F32),
                        pltpu.VMEM((2 * g, NSA_ACC_ROWS, 4 * tq), F32)],
        compiler_params=_cparams("parallel", "arbitrary"),
        name="nsa_attn",
    )(q, gn, kv, kv, kv, kv, kc2, vc2, cos, sin, cos, sin, ovt, ex)


def _merge_kernel(x_ref, yn_ref, bonus_ref, g_ref, lng_ref, lnb_ref, on_ref, gm_ref,
                  wrw_ref, wnsa_ref, wout_ref, o_ref):
    d = x_ref.shape[1]
    y_rw = ((yn_ref[...].astype(F32) * lng_ref[...] + lnb_ref[...] + bonus_ref[...].astype(F32))
            * g_ref[...].astype(F32))
    t_rw = _mm(y_rw, wrw_ref[...])
    t_ns = _mm(on_ref[...], wnsa_ref[...])
    gm = gm_ref[...].astype(F32)
    mix = _sigmoid(gm[:, 0:d]) * t_rw + _sigmoid(gm[:, d:2 * d]) * t_ns
    o_ref[...] = x_ref[...] + _mm(mix, wout_ref[...])


def _merge_out(x2, yn, bonus, g, ln_g, ln_b, o_nsa, gm, w_rw, w_nsa, w_out, tm):
    n, d = x2.shape
    row = lambda w: pl.BlockSpec((tm, w), lambda i: (i, 0))
    return pl.pallas_call(
        _merge_kernel,
        grid=(n // tm,),
        in_specs=[row(d), row(RW_WIDTH), row(RW_WIDTH), row(RW_WIDTH),
                  _const_spec(ln_g.shape), _const_spec(ln_b.shape),
                  row(NSA_WIDTH), row(2 * d),
                  _const_spec(w_rw.shape), _const_spec(w_nsa.shape), _const_spec(w_out.shape)],
        out_specs=row(d),
        out_shape=jax.ShapeDtypeStruct((n, d), F32),
        compiler_params=_cparams("parallel"),
        name="merge_out",
    )(x2, yn, bonus, g, ln_g, ln_b, o_nsa, gm, w_rw, w_nsa, w_out)


def _xattn_kernel(h_ref, g_ref, wq_ref, mem_ref, gm_ref, wkv_ref, wo_ref, o_ref, kv_sc):
    d = h_ref.shape[2]
    hd = d // XA_HEADS

    @pl.when(pl.program_id(1) == 0)
    def _():
        kv_sc[...] = _mm(_rms(mem_ref[0], gm_ref[...]), wkv_ref[...]).astype(BF16)

    h = h_ref[0]
    q = (_mm(_rms(h, g_ref[...]), wq_ref[...]) * hd ** -0.5).astype(BF16)
    def scores(hh):
        cs = slice(hh * hd, (hh + 1) * hd)
        return lax.dot_general(q[:, cs], kv_sc[:, cs], (((1,), (1,)), ((), ())), preferred_element_type=F32)

    outs = []
    s_next = scores(0)
    for hh in range(XA_HEADS):
        s, s_next = s_next, (scores(hh + 1) if hh + 1 < XA_HEADS else None)
        e = jnp.exp(s - jnp.max(s, axis=-1, keepdims=True))
        p = e * (1.0 / jnp.sum(e, axis=-1, keepdims=True))
        outs.append(jnp.dot(p.astype(BF16), kv_sc[:, d + hh * hd:d + (hh + 1) * hd],
                            preferred_element_type=F32))
    o_ref[0] = h + _mm(jnp.concatenate(outs, axis=1), wo_ref[...])


def _xattn(h3, g, wq, mem, g_mem, wkv, wo, tm):
    b, t, d = h3.shape
    m = mem.shape[1]
    return pl.pallas_call(
        _xattn_kernel,
        grid=(b, t // tm),
        in_specs=[pl.BlockSpec((1, tm, d), lambda bi, i: (bi, i, 0)), _const_spec(g.shape),
                  _const_spec(wq.shape),
                  pl.BlockSpec((1, m, d), lambda bi, i: (bi, 0, 0)), _const_spec(g_mem.shape),
                  _const_spec(wkv.shape), _const_spec(wo.shape)],
        out_specs=pl.BlockSpec((1, tm, d), lambda bi, i: (bi, i, 0)),
        out_shape=jax.ShapeDtypeStruct((b, t, d), F32),
        scratch_shapes=[pltpu.VMEM((m, 2 * d), BF16)],
        compiler_params=_cparams("parallel", "arbitrary"),
        name="xattn",
    )(h3, g, wq, mem, g_mem, wkv, wo)


FFN_CHUNK = 256


def _ffn_kernel(h_ref, g_ref, wgu_ref, wd_ref, gf_ref, o_ref):
    h = h_ref[...]
    hn = _rms(h, g_ref[...]).astype(BF16)
    dff = wd_ref.shape[0]
    acc = jnp.zeros(h.shape, F32)
    for c0 in range(0, dff, FFN_CHUNK):
        gate = jnp.dot(hn, wgu_ref[:, c0:c0 + FFN_CHUNK], preferred_element_type=F32)
        up = jnp.dot(hn, wgu_ref[:, dff + c0:dff + c0 + FFN_CHUNK], preferred_element_type=F32)
        act = (gate * _sigmoid(gate) * up).astype(BF16)
        acc = acc + jnp.dot(act, wd_ref[c0:c0 + FFN_CHUNK, :], preferred_element_type=F32)
    o_ref[...] = _rms(h + acc, gf_ref[...])


def _ffn(h2, g, wgu, wd, gf, tm):
    n, d = h2.shape
    assert wd.shape[0] % FFN_CHUNK == 0
    return pl.pallas_call(
        _ffn_kernel,
        grid=(n // tm,),
        in_specs=[pl.BlockSpec((tm, d), lambda i: (i, 0)), _const_spec(g.shape),
                  _const_spec(wgu.shape), _const_spec(wd.shape), _const_spec(gf.shape)],
        out_specs=pl.BlockSpec((tm, d), lambda i: (i, 0)),
        out_shape=jax.ShapeDtypeStruct((n, d), F32),
        compiler_params=_cparams("parallel"),
        name="ffn",
    )(h2, g, wgu, wd, gf)


def _head_sum_matrix():
    idx = np.arange(LANES) // HEAD_DIM
    return jnp.asarray(idx[:, None] == idx[None, :], BF16)


def _prefix_matrix():
    L = RW_CHUNK
    tri = np.tril(np.ones((L, L), np.float32))
    return jnp.asarray(np.concatenate([tri, np.ones((16, L), np.float32)], axis=0), BF16)


def _rope_tables(t):
    half = HEAD_DIM // 2
    inv_freq = ROPE_THETA ** (-np.arange(half, dtype=np.float64) / half)
    ang = np.arange(t, dtype=np.float64)[:, None] * inv_freq[None, :]
    cos, sin = np.cos(ang), np.sin(ang)
    cos128 = np.tile(cos, (1, LANES // half))
    sin128 = np.tile(np.concatenate([-sin, sin], axis=1), (1, LANES // HEAD_DIM))
    return jnp.asarray(cos128, F32), jnp.asarray(sin128, F32)


def _overlap_matrix(t):
    n_cmp = (t - CMP_BLOCK) // CMP_STRIDE + 1
    n_sel = t // SEL_BLOCK
    cs = np.arange(n_cmp) * CMP_STRIDE
    ss = np.arange(n_sel) * SEL_BLOCK
    ov = np.clip(np.minimum(cs[:, None] + CMP_BLOCK, ss[None, :] + SEL_BLOCK)
                 - np.maximum(cs[:, None], ss[None, :]), 0, None) / CMP_BLOCK
    out = np.zeros((n_sel, LANES), np.float32)
    out[:, :n_cmp] = ov.T
    return jnp.asarray(out, BF16)


def _expand_matrix(t):
    blk = np.arange(t) // SEL_BLOCK
    return jnp.asarray(blk[:, None] == np.arange(t // SEL_BLOCK)[None, :], BF16)


def kernel(x, mem, norm_mix_g, w_in, shift_mu, rw_w_up, rw_w0, rw_a_up, rw_a0, rw_g_up, rw_k_k, rw_k_a,
           rw_r_k, rw_ln_g, rw_ln_b, nsa_pe_k, nsa_pe_v, nsa_ck1, nsa_ck2, nsa_cv1, nsa_cv2, w_up_rw,
           w_up_nsa, w_out, norm_xa_g, norm_mem_g, xa_wq, xa_wkv, xa_wo, norm_ffn_g, ffn_w_gu,
           ffn_w_down, final_norm_g):
    b, t, d = x.shape
    n = b * t
    h = x.reshape(n, d)
    hsum = _head_sum_matrix()
    tri = _prefix_matrix()
    cos, sin = _rope_tables(t)
    ov = _overlap_matrix(t)
    ex = _expand_matrix(t)
    row = lambda a: a.reshape(1, -1)
    n_half = t // CMP_STRIDE
    assert t // SEL_BLOCK <= LANES and n_half <= LANES

    for l in range(w_in.shape[0]):
        c_g = RW_IN + NSA_WIDTH + 6 * KV_WIDTH
        n_gate = 3 * NSA_HEADS
        w_gate = jnp.pad(w_in[l][:, c_g:c_g + n_gate], ((0, 0), (0, LANES - n_gate))).astype(BF16)
        z64 = jnp.zeros((DECAY_LORA, RW_WIDTH), F32)
        wlora = jnp.concatenate([jnp.concatenate([rw_w_up[l], z64], axis=1),
                                 jnp.concatenate([z64, rw_a_up[l]], axis=1)], axis=0).astype(BF16)
        prep_consts = (row(shift_mu[l]), wlora, row(rw_w0[l]), row(rw_a0[l]), rw_g_up[l].astype(BF16),
                       row(rw_k_k[l]), row(rw_k_a[l]), row(rw_r_k[l]), tri, hsum)
        (q, kc, vc, kv, gn, gm, abar, rbar, btil, ktil, v_rw, gam, g_rw, bonus) = _in_proj(
            h, row(norm_mix_g[l]), w_in[l][:, :c_g].astype(BF16), w_gate,
            w_in[l][:, c_g + n_gate:].astype(BF16), prep_consts, 512, t)

        seq = lambda a: a.reshape(b, t, RW_WIDTH)
        abar, rbar, btil, ktil, v_rw = seq(abar), seq(rbar), seq(btil), seq(ktil), seq(v_rw)
        gam = gam.reshape(b, t // RW_CHUNK, RW_WIDTH)
        yn = _rwkv_scan(abar, rbar, btil, ktil, v_rw, gam)

        def w1_tokens(w1):
            hw = CMP_STRIDE * HEAD_DIM
            per_tok = jnp.concatenate([w1[:hw].reshape(CMP_STRIDE, HEAD_DIM, -1),
                                       w1[hw:].reshape(CMP_STRIDE, HEAD_DIM, -1)], axis=2)
            zero = jnp.zeros_like(per_tok)
            return jnp.concatenate([jnp.concatenate([per_tok, zero], axis=2),
                                    jnp.concatenate([zero, per_tok], axis=2)], axis=1).astype(BF16)

        dup = lambda w2: jnp.concatenate([w2, w2], axis=1).astype(BF16)
        kc2, vc2 = _nsa_cmp(kc.reshape(b, t, KV_WIDTH), vc.reshape(b, t, KV_WIDTH),
                            nsa_pe_k[l].reshape(1, -1), nsa_pe_v[l].reshape(1, -1),
                            w1_tokens(nsa_ck1[l]), nsa_ck1[l].astype(BF16), dup(nsa_ck2[l]),
                            w1_tokens(nsa_cv1[l]), nsa_cv1[l].astype(BF16), dup(nsa_cv2[l]))
        o_nsa = _nsa_attn(q.reshape(b, t, NSA_WIDTH), gn.reshape(b, t, LANES),
                          kv.reshape(b, t, 4 * KV_WIDTH), kc2, vc2, cos, sin, ov, ex)

        h = _merge_out(h, yn.reshape(n, RW_WIDTH), bonus.reshape(n, RW_WIDTH), g_rw.reshape(n, RW_WIDTH),
                       row(rw_ln_g[l]), row(rw_ln_b[l]), o_nsa.reshape(n, NSA_WIDTH), gm,
                       w_up_rw[l].astype(BF16), w_up_nsa[l].astype(BF16), w_out[l].astype(BF16), 512)

        h = _xattn(h.reshape(b, t, d), row(norm_xa_g[l]), xa_wq[l].astype(BF16), mem, row(norm_mem_g[l]),
                   xa_wkv[l].astype(BF16), xa_wo[l].astype(BF16), 512).reshape(n, d)

        last = l == w_in.shape[0] - 1
        gf = row(final_norm_g) if last else None
        assert last, "only a single layer is fused with the final norm"
        h = _ffn(h, row(norm_ffn_g[l]), ffn_w_gu[l].astype(BF16), ffn_w_down[l].astype(BF16), gf, 512)
    return h.reshape(b, t, d)
```

```python
import functools
import math

import numpy as np
import jax
import jax.numpy as jnp
from jax import lax
from jax.experimental import pallas as pl
from jax.experimental.pallas import tpu as pltpu

F32 = jnp.float32
BF16 = jnp.bfloat16

HEAD_DIM = 64
NORM_EPS = 1e-6
ROPE_THETA = 10000.0
RW_HEADS = 8
RW_WIDTH = RW_HEADS * HEAD_DIM
DECAY_LORA = 64
AAA_LORA = 64
GATE_LORA = 128
RW_GN_EPS = 64e-5
RW_IN = 3 * RW_WIDTH + DECAY_LORA + AAA_LORA + GATE_LORA
NSA_HEADS = 8
NSA_KV_GROUPS = 2
NSA_WIDTH = NSA_HEADS * HEAD_DIM
KV_WIDTH = NSA_KV_GROUPS * HEAD_DIM
CMP_BLOCK = 32
CMP_STRIDE = 16
CMP_HIDDEN = 128
SEL_BLOCK = 64
SEL_TOPK = 8
WINDOW = 512
XA_HEADS = 4

LANES = 128
VMEM_LIMIT = 56 * 1024 * 1024
RW_CHUNK = 64
NEG_BIG = -1e30
LOG2E = 1.4426950408889634


def _cparams(*sem):
    return pltpu.CompilerParams(dimension_semantics=sem, vmem_limit_bytes=VMEM_LIMIT)


def _mm(a, b):
    return jnp.dot(a.astype(BF16), b.astype(BF16), preferred_element_type=F32)


def _mm_nt(a, b):
    return lax.dot_general(a.astype(BF16), b.astype(BF16), (((1,), (1,)), ((), ())),
                           preferred_element_type=F32)


def _mm_tn(a, b):
    return lax.dot_general(a.astype(BF16), b.astype(BF16), (((0,), (0,)), ((), ())),
                           preferred_element_type=F32)


def _split(x, parts):
    out = []
    for _ in range(parts - 1):
        piece = x.astype(BF16)
        out.append(piece)
        x = x - piece.astype(F32)
    return out + [x.astype(BF16)]


def _split3(x):
    return _split(x, 3)


def _mm_exact_rhs(x, m_bf16, parts=3):
    return sum(jnp.dot(p, m_bf16, preferred_element_type=F32) for p in _split(x, parts))


def _head_sums(x, pair_ones, parts):
    return jnp.concatenate([_mm_exact_rhs(x[:, c:c + LANES], pair_ones, parts)
                            for c in range(0, x.shape[1], LANES)], axis=1)


def _mm_exact_lhs(m_bf16, x, parts=3):
    return sum(jnp.dot(m_bf16, p, preferred_element_type=F32) for p in _split(x, parts))


def _rms(x, g):
    return x * lax.rsqrt(jnp.mean(x * x, axis=-1, keepdims=True) + NORM_EPS) * g


def _sigmoid(x):
    return 1.0 / (1.0 + jnp.exp(-x))


def _const_spec(shape):
    nd = len(shape)
    return pl.BlockSpec(shape, lambda *_: (0,) * nd)


def _in_proj_kernel(x_ref, g_ref, w_main_ref, w_gate_ref, w_merge_ref,
                    mu_ref, wlora_ref, w0_ref, a0_ref, gup_ref, kk_ref, ka_ref, rk_ref, tri_ref, hsum_ref,
                    q_ref, kc_ref, vc_ref, kv_ref, gn_ref, gm_ref,
                    abar_ref, rbar_ref, btil_ref, ktil_ref, v_ref, gam_ref, gate_ref, bonus_ref,
                    p_sc, last_sc, *, rows_per_seq):
    i = pl.program_id(0)
    tm = x_ref.shape[0]
    w = RW_WIDTH
    hn = _rms(x_ref[...], g_ref[...]).astype(BF16)

    def project(o_ref, w_ref, base, c0, c1):
        o_ref[:, c0:c1] = jnp.dot(hn, w_ref[:, base + c0:base + c1],
                                  preferred_element_type=F32).astype(o_ref.dtype)

    def project_all(o_ref, w_ref, base):
        width = o_ref.shape[1]
        for c0 in range(0, width, 512):
            project(o_ref, w_ref, base, c0, min(c0 + 512, width))

    project_all(p_sc, w_main_ref, 0)
    off = RW_IN
    for o_ref in (q_ref, kc_ref, vc_ref, kv_ref):
        project_all(o_ref, w_main_ref, off)
        off += o_ref.shape[1]
    project_all(gn_ref, w_gate_ref, 0)

    cur = p_sc[...]
    prev_row = jnp.where((i * tm) % rows_per_seq == 0, 0.0, last_sc[7:8, :])
    row = lax.broadcasted_iota(jnp.int32, (tm, 1), 0)
    prev = jnp.where(row == 0, prev_row, pltpu.roll(cur, 1, 0))
    p = cur + mu_ref[...] * (prev - cur)
    last_sc[7:8, :] = cur[tm - 1:tm, :]

    r = p[:, 0:w]
    k = p[:, w:2 * w]
    v = p[:, 2 * w:3 * w]
    x_wa = p[:, 3 * w:3 * w + LANES]
    x_g = p[:, 3 * w + LANES:3 * w + 2 * LANES]
    lane = lax.broadcasted_iota(jnp.int32, (1, LANES), 1)
    x_wa = jnp.where(lane < DECAY_LORA, jnp.tanh(x_wa), x_wa)
    lin = _mm(x_wa, wlora_ref[...])
    w_lin = w0_ref[...] + lin[:, 0:w]
    a = _sigmoid(a0_ref[...] + lin[:, w:2 * w])
    gate_ref[...] = _mm(_sigmoid(x_g), gup_ref[...]).astype(BF16)

    logw = _sigmoid(w_lin) * (-math.exp(-0.5) * LOG2E)

    hsum = hsum_ref[...]
    kk = k * kk_ref[...]
    kk = kk * lax.rsqrt(jnp.maximum(_head_sums(kk * kk, hsum, 1), 1e-12))
    k2 = k * (1.0 + (a - 1.0) * ka_ref[...])
    bonus_ref[...] = (_head_sums(r * k2 * rk_ref[...], hsum, 1) * v).astype(BF16)
    v_ref[...] = v.astype(BF16)

    half = gm_ref.shape[1] // 2
    for c0 in range(0, half, 512):
        project(gm_ref, w_merge_ref, 0, c0, c0 + 512)

    L = RW_CHUNK
    tri = tri_ref[...]
    kka = kk * a
    gam_rows = []
    for c in range(tm // L):
        sl = slice(c * L, (c + 1) * L)
        cs = _mm_exact_lhs(tri, logw[sl], 2)
        cum, gam = cs[0:L], jnp.exp2(cs[L:L + 1])
        e_cum = jnp.exp2(cum)
        e_neg = 1.0 / e_cum
        abar_ref[sl, :] = (-kk[sl] * jnp.exp2(cum - logw[sl])).astype(BF16)
        rbar_ref[sl, :] = (r[sl] * e_cum).astype(BF16)
        btil_ref[sl, :] = (kka[sl] * e_neg).astype(BF16)
        ktil_ref[sl, :] = (k2[sl] * e_neg).astype(BF16)
        gam_rows.append(gam)
    gam_ref[...] = jnp.concatenate(gam_rows, axis=0)

    for c0 in range(half, 2 * half, 512):
        project(gm_ref, w_merge_ref, 0, c0, c0 + 512)


def _in_proj(x2, g, w_main, w_gate, w_merge, prep_consts, tm, rows_per_seq):
    n, d = x2.shape
    w = RW_WIDTH
    widths = (NSA_WIDTH, KV_WIDTH, KV_WIDTH, 4 * KV_WIDTH, LANES, 2 * d)
    dtypes = (BF16, F32, F32, BF16, F32, BF16)
    assert RW_IN + sum(widths[:-2]) == w_main.shape[1] and widths[-2:] == (w_gate.shape[1], w_merge.shape[1])
    assert n % tm == 0 and rows_per_seq % tm == 0 and tm % (8 * RW_CHUNK) == 0 and (2 * d) % 1024 == 0
    rows = lambda width: pl.BlockSpec((tm, width), lambda i: (i, 0))
    seq_bf = jax.ShapeDtypeStruct((n, w), BF16)
    return pl.pallas_call(
        functools.partial(_in_proj_kernel, rows_per_seq=rows_per_seq),
        grid=(n // tm,),
        in_specs=[rows(d), _const_spec((1, d)),
                  _const_spec(w_main.shape), _const_spec(w_gate.shape), _const_spec(w_merge.shape)]
        + [_const_spec(c.shape) for c in prep_consts],
        out_specs=[rows(wd) for wd in widths] + [rows(w)] * 5
        + [pl.BlockSpec((tm // RW_CHUNK, w), lambda i: (i, 0)), rows(w), rows(w)],
        out_shape=[jax.ShapeDtypeStruct((n, wd), dt) for wd, dt in zip(widths, dtypes)] + [seq_bf] * 5
        + [jax.ShapeDtypeStruct((n // RW_CHUNK, w), F32), seq_bf, seq_bf],
        scratch_shapes=[pltpu.VMEM((tm, RW_IN), F32), pltpu.VMEM((8, RW_IN), F32)],
        compiler_params=_cparams("arbitrary"),
        name="in_proj",
    )(x2, g, w_main, w_gate, w_merge, *prep_consts)


def _rwkv_scan_kernel(abar_ref, rbar_ref, btil_ref, ktil_ref, v_ref, gam_ref,
                      y_ref, s_ref):
    c = pl.program_id(1)
    L = RW_CHUNK

    @pl.when(c == 0)
    def _():
        s_ref[...] = jnp.zeros_like(s_ref)

    lane = lax.broadcasted_iota(jnp.int32, (1, LANES), 1)
    m0 = lane < HEAD_DIM
    ri = lax.broadcasted_iota(jnp.int32, (2 * L, 2 * L), 0)
    ci = lax.broadcasted_iota(jnp.int32, (2 * L, 2 * L), 1)
    same = (ri // L) == (ci // L)
    low_strict = same & (ci < ri)
    low_incl = same & (ci <= ri)
    zero = jnp.zeros((), BF16)
    nb = abar_ref.shape[0]
    n_pair = RW_HEADS // 2

    def stack(ref, bi, ls):
        x = ref[bi, :, ls]
        return jnp.concatenate([jnp.where(m0, x, zero), jnp.where(m0, zero, x)], axis=0)

    chains = [(bi, pr) for bi in range(nb) for pr in range(n_pair)]
    nch = len(chains)
    lanes_of = lambda pr: slice(pr * LANES, (pr + 1) * LANES)

    x_all = [jnp.concatenate([stack(abar_ref, bi, lanes_of(pr)), stack(rbar_ref, bi, lanes_of(pr))], axis=0)
             for bi, pr in chains]
    vs = [stack(v_ref, bi, lanes_of(pr)) for bi, pr in chains]
    s0 = [s_ref[bi * n_pair + pr] for bi, pr in chains]
    bk = [jnp.concatenate([stack(btil_ref, bi, lanes_of(pr)), stack(ktil_ref, bi, lanes_of(pr))], axis=0)
          for bi, pr in chains]
    amat = [_mm_nt(x_all[i], bk[i]) for i in range(nch)]
    xs = [_mm_nt(x_all[i], s0[i]) for i in range(nch)]

    u = [xs[i][0:2 * L] + _mm(jnp.where(low_strict, amat[i][0:2 * L, 2 * L:4 * L], 0.0), vs[i])
         for i in range(nch)]
    pw = [jnp.where(low_strict, a[0:2 * L, 0:2 * L], 0.0).astype(BF16) for a in amat]
    n_sq = int(math.log2(L))
    for step in range(n_sq):
        u = [u[i] + _mm(pw[i], u[i]) for i in range(nch)]
        if step + 1 < n_sq:
            pw = [_mm(q, q).astype(BF16) for q in pw]

    uv = [jnp.concatenate([u[i].astype(BF16), vs[i]], axis=0) for i in range(nch)]
    gam_rows = [gam_ref[bi, pl.ds(c % 8, 1), :] for bi in range(nb)]
    for i, (bi, pr) in enumerate(chains):
        gam = gam_rows[bi][:, lanes_of(pr)]
        bk_hat = bk[i].astype(F32) * gam
        s_ref[bi * n_pair + pr] = s0[i] * gam + _mm_tn(uv[i], bk_hat)
    def head_mean(z):
        lo = jnp.sum(jnp.where(m0, z, 0.0), axis=-1, keepdims=True)
        hi = jnp.sum(jnp.where(m0, 0.0, z), axis=-1, keepdims=True)
        return jnp.where(m0, lo, hi) * (1.0 / HEAD_DIM)

    for i, (bi, pr) in enumerate(chains):
        a_r =jnp.concatenate([jnp.where(low_incl, amat[i][2 * L:4 * L, 0:2 * L], 0.0),
                               jnp.where(low_incl, amat[i][2 * L:4 * L, 2 * L:4 * L], 0.0)], axis=1)
        y2 = xs[i][2 * L:4 * L] + _mm(a_r, uv[i])
        y = y2[0:L] + y2[L:2 * L]
        d = y - head_mean(y)
        y_ref[bi, :, lanes_of(pr)] = (d * lax.rsqrt(head_mean(d * d) + RW_GN_EPS)).astype(BF16)


RW_SCAN_BATCH = 8


def _rwkv_scan(abar, rbar, btil, ktil, v, gam):
    b, t, w = abar.shape
    L = RW_CHUNK
    nb = RW_SCAN_BATCH
    assert b % nb == 0
    tile = lambda: pl.BlockSpec((nb, L, w), lambda bi, c: (bi, c, 0))
    return pl.pallas_call(
        _rwkv_scan_kernel,
        grid=(b // nb, t // L),
        in_specs=[tile(), tile(), tile(), tile(), tile(),
                  pl.BlockSpec((nb, 8, w), lambda bi, c: (bi, c // 8, 0))],
        out_specs=tile(),
        out_shape=jax.ShapeDtypeStruct((b, t, w), BF16),
        scratch_shapes=[pltpu.VMEM((nb * (RW_HEADS // 2), LANES, LANES), F32)],
        compiler_params=_cparams("parallel", "arbitrary"),
        name="rwkv_scan",
    )(abar, rbar, btil, ktil, v, gam)


def _gelu_tanh(x):
    return 0.5 * x * (1.0 + jnp.tanh(math.sqrt(2.0 / math.pi) * (x + 0.044715 * x * x * x)))


def _nsa_cmp_kernel(kc_ref, vc_ref, pek_ref, pev_ref, k1_ref, k1f_ref, k2_ref, v1_ref, v1f_ref, v2_ref,
                    ko_ref, vo_ref):
    nh = ko_ref.shape[1]
    hid = CMP_HIDDEN
    for x_ref, pe_ref, w1_ref, w1f_ref, w2_ref, o_ref in (
            (kc_ref, pek_ref, k1_ref, k1f_ref, k2_ref, ko_ref),
            (vc_ref, pev_ref, v1_ref, v1f_ref, v2_ref, vo_ref)):
        z = jnp.zeros((nh, 2 * NSA_KV_GROUPS * hid), F32)
        for tok in range(CMP_STRIDE):
            z = z + _mm(x_ref[0, pl.ds(tok, nh, stride=CMP_STRIDE), :], w1_ref[tok])
        bias = _mm(jnp.broadcast_to(pe_ref[...], (8, pe_ref.shape[1])), w1f_ref[...])[0:1]
        for g in range(NSA_KV_GROUPS):
            zg = z[:, 2 * g * hid:2 * (g + 1) * hid]
            pre = zg[:, 0:hid] + pltpu.roll(zg[:, hid:2 * hid], nh - 1, 0) + bias
            o_ref[g] = _mm(_gelu_tanh(pre), w2_ref[...])


def _nsa_cmp(kc, vc, pe_k, pe_v, k1, k1f, k2d, v1, v1f, v2d):
    b, t, _ = kc.shape
    nh = t // CMP_STRIDE
    tile_in = lambda: pl.BlockSpec((1, t, LANES), lambda i: (i, 0, 0))
    tile_out = lambda: pl.BlockSpec((NSA_KV_GROUPS, nh, LANES), lambda i: (i, 0, 0))
    out = jax.ShapeDtypeStruct((b * NSA_KV_GROUPS, nh, LANES), F32)
    consts = (pe_k, pe_v, k1, k1f, k2d, v1, v1f, v2d)
    return pl.pallas_call(
        _nsa_cmp_kernel,
        grid=(b,),
        in_specs=[tile_in(), tile_in()] + [_const_spec(c.shape) for c in consts],
        out_specs=[tile_out(), tile_out()],
        out_shape=[out, out],
        compiler_params=_cparams("parallel"),
        name="nsa_cmp",
    )(kc, vc, pe_k, pe_v, k1, k1f, k2d, v1, v1f, v2d)


NSA_TQ = 256
NSA_TK = 256
NSA_LOOKAHEAD = 1


def _rope_swap(x):
    lane = lax.broadcasted_iota(jnp.int32, (1, LANES), 1)
    first = (lane % HEAD_DIM) < (HEAD_DIM // 2)
    return jnp.where(first, pltpu.roll(x, LANES - HEAD_DIM // 2, 1), pltpu.roll(x, HEAD_DIM // 2, 1))


def _masked_softmax_parts(s, mask):
    s = jnp.where(mask, s, NEG_BIG)
    e = jnp.where(mask, jnp.exp(s - jnp.max(s, axis=-1, keepdims=True)), 0.0)
    return e / jnp.maximum(jnp.sum(e, axis=-1, keepdims=True), 1e-30)


NSA_ACC_ROWS = HEAD_DIM + 16


def _nsa_attn_kernel(q_ref, gn_ref, ks_ref, vs_ref, kw_ref, vw_ref, kc_ref, vc_ref,
                     cosq_ref, sinq_ref, cosk_ref, sink_ref, ovt_ref, ext_ref,
                     o_ref,
                     k2_sc, vt_sc, qrot_sc, sel_sc, m_sc, acc_sc):
    i = pl.program_id(1)
    tq, tk = NSA_TQ, NSA_TK
    t = ks_ref.shape[1]
    n_kt = t // tk
    n_sel = t // SEL_BLOCK
    lane = lax.broadcasted_iota(jnp.int32, (1, LANES), 1)
    lo_half = lane < HEAD_DIM

    @pl.when(i == 0)
    def _():
        ones = jnp.ones((NSA_ACC_ROWS - HEAD_DIM, tk), BF16)
        for jt in range(n_kt):
            rs = slice(jt * tk, (jt + 1) * tk)
            cos, sin = cosk_ref[rs, :], sink_ref[rs, :]
            for br, k_src, v_src in ((0, ks_ref, vs_ref), (1, kw_ref, vw_ref)):
                x = k_src[0, rs, :].astype(F32)
                x = x * cos + _rope_swap(x) * sin
                xr = pltpu.roll(x, HEAD_DIM, 1)
                k2_sc[br, 0, rs, :] = jnp.where(lo_half, x, xr).astype(BF16)
                k2_sc[br, 1, rs, :] = jnp.where(lo_half, xr, x).astype(BF16)
                v_t = v_src[0, rs, :].astype(F32).T.astype(BF16)
                for g in range(NSA_KV_GROUPS):
                    vt_sc[br, g, jt, 0:HEAD_DIM, :] = v_t[g * HEAD_DIM:(g + 1) * HEAD_DIM]
                    vt_sc[br, g, jt, HEAD_DIM:NSA_ACC_ROWS, :] = ones

    tq_row = i * tq + lax.broadcasted_iota(jnp.int32, (1, tq), 1)
    tq4_row = jnp.concatenate([tq_row] * 4, axis=1)
    gates_t = _sigmoid(gn_ref[0]).T
    cosq, sinq = cosq_ref[...], sinq_ref[...]
    qscale = HEAD_DIM ** -0.5 * LOG2E
    nt_dims = (((1,), (1,)), ((), ()))
    cols = lambda hh: slice(hh * tq, (hh + 1) * tq)

    def stack_heads(xa, xb):
        parts = [jnp.where(lo_half, xa, 0.0), jnp.where(lo_half, 0.0, xa),
                 jnp.where(lo_half, xb, 0.0), jnp.where(lo_half, 0.0, xb)]
        return (jnp.concatenate(parts, axis=0) * qscale).astype(BF16)

    j_last = (i * tq + tq - 1) // tk
    j_first_win = jnp.maximum(i * tq - (WINDOW - 1), 0) // tk
    blk = lax.broadcasted_iota(jnp.int32, (n_sel, 1), 0)
    cur = tq_row // SEL_BLOCK
    forced = (blk == 0) | (blk == cur) | (blk == cur - 1)
    allowed = blk <= cur
    n_valid = (t - CMP_BLOCK) // CMP_STRIDE + 1
    cblk = lax.broadcasted_iota(jnp.int32, (LANES, 1), 0)
    cmask = (cblk * CMP_STRIDE + (CMP_BLOCK - 1) <= tq4_row) & (cblk < n_valid)

    o_cmp = []
    for g in range(NSA_KV_GROUPS):
        qa = q_ref[0, :, (2 * g) * LANES:(2 * g + 1) * LANES].astype(F32)
        qb = q_ref[0, :, (2 * g + 1) * LANES:(2 * g + 2) * LANES].astype(F32)
        q_cmp = stack_heads(qa, qb)
        qrot_sc[g] = stack_heads(qa * cosq + _rope_swap(qa) * sinq, qb * cosq + _rope_swap(qb) * sinq)

        s_c = lax.dot_general(kc_ref[g].astype(BF16), q_cmp, nt_dims, preferred_element_type=F32)
        s_c = jnp.where(cmask, s_c, NEG_BIG)
        e_c = jnp.where(cmask, jnp.exp2(s_c - jnp.max(s_c, axis=0, keepdims=True)), 0.0)
        p_c = e_c / jnp.maximum(jnp.sum(e_c, axis=0, keepdims=True), 1e-30)
        o_cmp.append(_mm(vc_ref[g].T[0:HEAD_DIM], p_c))

        p_sum = p_c[:, cols(0)] + p_c[:, cols(1)] + p_c[:, cols(2)] + p_c[:, cols(3)]
        imp = sum(jnp.dot(ovt_ref[...], part, preferred_element_type=F32) for part in _split3(p_sum))
        imp = jnp.where(forced, 1e4, jnp.where(allowed, imp, -1.0))
        rank = jnp.zeros((n_sel, tq), F32)
        for b in range(n_sel):
            row_b = imp[b:b + 1, :]
            rank = rank + jnp.where((row_b > imp) | ((row_b == imp) & (b < blk)), 1.0, 0.0)
        sel_sc[g] = jnp.where(rank < min(SEL_TOPK, n_sel), 1.0, 0.0).astype(BF16)

    for ch in range(2 * NSA_KV_GROUPS):
        m_sc[ch] = jnp.full(m_sc.shape[1:], NEG_BIG, F32)
        acc_sc[ch] = jnp.zeros(acc_sc.shape[1:], F32)

    def run(items):
        def scores(j, pos0, br, g):
            r0 = pl.multiple_of(j * tk, tk)
            kpos = pos0 + lax.broadcasted_iota(jnp.int32, (tk, 1), 0)
            s = lax.dot_general(k2_sc[br, g, pl.ds(r0, tk), :], qrot_sc[g], nt_dims,
                                preferred_element_type=F32)
            if br == 0:
                hit = jnp.dot(ext_ref[pl.ds(r0, tk), :], sel_sc[g], preferred_element_type=F32)
                keep = (kpos <= tq_row) & (hit > 0.5)
            else:
                keep = (kpos <= tq_row) & (kpos > tq_row - WINDOW)
            return s + jnp.concatenate([jnp.where(keep, 0.0, NEG_BIG)] * 4, axis=1)

        pending = [scores(*it) for it in items[:NSA_LOOKAHEAD]]
        for n, (j, _, br, g) in enumerate(items):
            if n + NSA_LOOKAHEAD < len(items):
                pending.append(scores(*items[n + NSA_LOOKAHEAD]))
            s_cur = pending.pop(0)
            ch = br * NSA_KV_GROUPS + g
            m_old = m_sc[ch]
            m_new = jnp.maximum(m_old, jnp.max(s_cur, axis=0, keepdims=True))
            p = jnp.exp2(s_cur - m_new).astype(BF16)
            m_sc[ch] = m_new
            acc_sc[ch] = jnp.exp2(m_old - m_new) * acc_sc[ch] + jnp.dot(
                vt_sc[br, g, j], p, preferred_element_type=F32)

    def sel_only(j, carry):
        run([(j, j * tk, 0, g) for g in range(NSA_KV_GROUPS)])
        return carry

    def sel_and_win(j, carry):
        run([(j, j * tk, br, g) for g in range(NSA_KV_GROUPS) for br in (0, 1)])
        return carry

    lax.fori_loop(0, j_first_win, sel_only, 0)
    lax.fori_loop(j_first_win, j_last + 1, sel_and_win, 0)

    def finish(ch):
        acc = acc_sc[ch]
        out = acc[0:HEAD_DIM] / jnp.maximum(acc[HEAD_DIM:HEAD_DIM + 1], 1e-30)
        return jnp.where(m_sc[ch] > 0.5 * NEG_BIG, out, 0.0)

    for g in range(NSA_KV_GROUPS):
        o_sel, o_win = finish(g), finish(NSA_KV_GROUPS + g)

        def head_out(hh):
            r = (4 * g + hh) * 3
            return (gates_t[r:r + 1] * o_cmp[g][:, cols(hh)] + gates_t[r + 1:r + 2] * o_sel[:, cols(hh)]
                    + gates_t[r + 2:r + 3] * o_win[:, cols(hh)])

        for pr in range(2):
            o_ref[0, :, (2 * g + pr) * LANES:(2 * g + pr + 1) * LANES] = jnp.concatenate(
                [head_out(2 * pr), head_out(2 * pr + 1)], axis=0).T.astype(BF16)


def _nsa_attn(q, gn, kv, kc2, vc2, cos, sin, ovt, ex):
    b, t, _ = q.shape
    tq, tk = NSA_TQ, NSA_TK
    g = NSA_KV_GROUPS
    assert t % tq == 0 and t % tk == 0
    full = lambda col: pl.BlockSpec((1, t, LANES), lambda bi, i: (bi, 0, col))
    return pl.pallas_call(
        _nsa_attn_kernel,
        grid=(b, t // tq),
        in_specs=[pl.BlockSpec((1, tq, NSA_WIDTH), lambda bi, i: (bi, i, 0)),
                  pl.BlockSpec((1, tq, LANES), lambda bi, i: (bi, i, 0)),
                  full(0), full(1), full(2), full(3),
                  pl.BlockSpec((g, kc2.shape[1], LANES), lambda bi, i: (bi, 0, 0)),
                  pl.BlockSpec((g, vc2.shape[1], LANES), lambda bi, i: (bi, 0, 0)),
                  pl.BlockSpec((tq, LANES), lambda bi, i: (i, 0)),
                  pl.BlockSpec((tq, LANES), lambda bi, i: (i, 0)),
                  _const_spec(cos.shape), _const_spec(sin.shape),
                  _const_spec(ovt.shape), _const_spec(ex.shape)],
        out_specs=pl.BlockSpec((1, tq, NSA_WIDTH), lambda bi, i: (bi, i, 0)),
        out_shape=jax.ShapeDtypeStruct((b, t, NSA_WIDTH), BF16),
        scratch_shapes=[pltpu.VMEM((2, g, t, LANES), BF16),
                        pltpu.VMEM((2, g, t // tk, NSA_ACC_ROWS, tk), BF16),
                        pltpu.VMEM((g, 4 * tq, LANES), BF16),
                        pltpu.VMEM((g, t // SEL_BLOCK, tq), BF16),
                        pltpu.VMEM((2 * g, 1, 4 * tq), F32),
                        pltpu.VMEM((2 * g, NSA_ACC_ROWS, 4 * tq), F32)],
        compiler_params=_cparams("parallel", "arbitrary"),
        name="nsa_attn",
    )(q, gn, kv, kv, kv, kv, kc2, vc2, cos, sin, cos, sin, ovt, ex)


def _mix_xattn_kernel(x_ref, yn_ref, bonus_ref, g_ref, lng_ref, lnb_ref, on_ref, gm_ref,
                      wrw_ref, wnsa_ref, wout_ref,
                      gx_ref, wq_ref, mem_ref, gmem_ref, wkv_ref, wo_ref,
                      o_ref, kv_sc):
    d = x_ref.shape[2]
    hd = d // XA_HEADS

    @pl.when(pl.program_id(1) == 0)
    def _():
        kv_sc[...] = _mm(_rms(mem_ref[0], gmem_ref[...]), wkv_ref[...]).astype(BF16)

    y_rw = ((yn_ref[0].astype(F32) * lng_ref[...] + lnb_ref[...] + bonus_ref[0].astype(F32))
            * g_ref[0].astype(F32))
    t_rw = _mm(y_rw, wrw_ref[...])
    t_ns = _mm(on_ref[0], wnsa_ref[...])
    gm = gm_ref[0].astype(F32)
    mix = _sigmoid(gm[:, 0:d]) * t_rw + _sigmoid(gm[:, d:2 * d]) * t_ns
    h = x_ref[0] + _mm(mix, wout_ref[...])

    q = (_mm(_rms(h, gx_ref[...]), wq_ref[...]) * hd ** -0.5).astype(BF16)

    def scores(hh):
        cs = slice(hh * hd, (hh + 1) * hd)
        return lax.dot_general(q[:, cs], kv_sc[:, cs], (((1,), (1,)), ((), ())), preferred_element_type=F32)

    outs = []
    s_next = scores(0)
    for hh in range(XA_HEADS):
        s, s_next = s_next, (scores(hh + 1) if hh + 1 < XA_HEADS else None)
        e = jnp.exp(s - jnp.max(s, axis=-1, keepdims=True))
        p = e * (1.0 / jnp.sum(e, axis=-1, keepdims=True))
        outs.append(jnp.dot(p.astype(BF16), kv_sc[:, d + hh * hd:d + (hh + 1) * hd],
                            preferred_element_type=F32))
    o_ref[0] = h + _mm(jnp.concatenate(outs, axis=1), wo_ref[...])


def _mix_xattn(x3, yn, bonus, g, ln_g, ln_b, o_nsa, gm, w_rw, w_nsa, w_out, g_xa, wq, mem, g_mem, wkv, wo, tm):
    b, t, d = x3.shape
    m = mem.shape[1]
    rows = lambda width: pl.BlockSpec((1, tm, width), lambda bi, i: (bi, i, 0))
    consts = lambda *arrs: [_const_spec(a.shape) for a in arrs]
    return pl.pallas_call(
        _mix_xattn_kernel,
        grid=(b, t // tm),
        in_specs=[rows(d), rows(RW_WIDTH), rows(RW_WIDTH), rows(RW_WIDTH)] + consts(ln_g, ln_b)
        + [rows(NSA_WIDTH), rows(2 * d)] + consts(w_rw, w_nsa, w_out, g_xa, wq)
        + [pl.BlockSpec((1, m, d), lambda bi, i: (bi, 0, 0))] + consts(g_mem, wkv, wo),
        out_specs=rows(d),
        out_shape=jax.ShapeDtypeStruct((b, t, d), F32),
        scratch_shapes=[pltpu.VMEM((m, 2 * d), BF16)],
        compiler_params=_cparams("parallel", "arbitrary"),
        name="mix_xattn",
    )(x3, yn, bonus, g, ln_g, ln_b, o_nsa, gm, w_rw, w_nsa, w_out, g_xa, wq, mem, g_mem, wkv, wo)


FFN_CHUNK = 256


def _ffn_kernel(h_ref, g_ref, wgu_ref, wd_ref, gf_ref, o_ref):
    h = h_ref[...]
    hn = _rms(h, g_ref[...]).astype(BF16)
    dff = wd_ref.shape[0]
    acc = jnp.zeros(h.shape, F32)
    for c0 in range(0, dff, FFN_CHUNK):
        gate = jnp.dot(hn, wgu_ref[:, c0:c0 + FFN_CHUNK], preferred_element_type=F32)
        up = jnp.dot(hn, wgu_ref[:, dff + c0:dff + c0 + FFN_CHUNK], preferred_element_type=F32)
        act = (gate * _sigmoid(gate) * up).astype(BF16)
        acc = acc + jnp.dot(act, wd_ref[c0:c0 + FFN_CHUNK, :], preferred_element_type=F32)
    o_ref[...] = _rms(h + acc, gf_ref[...])


def _ffn(h2, g, wgu, wd, gf, tm):
    n, d = h2.shape
    assert wd.shape[0] % FFN_CHUNK == 0
    return pl.pallas_call(
        _ffn_kernel,
        grid=(n // tm,),
        in_specs=[pl.BlockSpec((tm, d), lambda i: (i, 0)), _const_spec(g.shape),
                  _const_spec(wgu.shape), _const_spec(wd.shape), _const_spec(gf.shape)],
        out_specs=pl.BlockSpec((tm, d), lambda i: (i, 0)),
        out_shape=jax.ShapeDtypeStruct((n, d), F32),
        compiler_params=_cparams("parallel"),
        name="ffn",
    )(h2, g, wgu, wd, gf)


def _head_sum_matrix():
    idx = np.arange(LANES) // HEAD_DIM
    return jnp.asarray(idx[:, None] == idx[None, :], BF16)


def _prefix_matrix():
    L = RW_CHUNK
    tri = np.tril(np.ones((L, L), np.float32))
    return jnp.asarray(np.concatenate([tri, np.ones((16, L), np.float32)], axis=0), BF16)


def _rope_tables(t):
    half = HEAD_DIM // 2
    inv_freq = ROPE_THETA ** (-np.arange(half, dtype=np.float64) / half)
    ang = np.arange(t, dtype=np.float64)[:, None] * inv_freq[None, :]
    cos, sin = np.cos(ang), np.sin(ang)
    cos128 = np.tile(cos, (1, LANES // half))
    sin128 = np.tile(np.concatenate([-sin, sin], axis=1), (1, LANES // HEAD_DIM))
    return jnp.asarray(cos128, F32), jnp.asarray(sin128, F32)


def _overlap_matrix(t):
    n_cmp = (t - CMP_BLOCK) // CMP_STRIDE + 1
    n_sel = t // SEL_BLOCK
    cs = np.arange(n_cmp) * CMP_STRIDE
    ss = np.arange(n_sel) * SEL_BLOCK
    ov = np.clip(np.minimum(cs[:, None] + CMP_BLOCK, ss[None, :] + SEL_BLOCK)
                 - np.maximum(cs[:, None], ss[None, :]), 0, None) / CMP_BLOCK
    out = np.zeros((n_sel, LANES), np.float32)
    out[:, :n_cmp] = ov.T
    return jnp.asarray(out, BF16)


def _expand_matrix(t):
    blk = np.arange(t) // SEL_BLOCK
    return jnp.asarray(blk[:, None] == np.arange(t // SEL_BLOCK)[None, :], BF16)


def kernel(x, mem, norm_mix_g, w_in, shift_mu, rw_w_up, rw_w0, rw_a_up, rw_a0, rw_g_up, rw_k_k, rw_k_a,
           rw_r_k, rw_ln_g, rw_ln_b, nsa_pe_k, nsa_pe_v, nsa_ck1, nsa_ck2, nsa_cv1, nsa_cv2, w_up_rw,
           w_up_nsa, w_out, norm_xa_g, norm_mem_g, xa_wq, xa_wkv, xa_wo, norm_ffn_g, ffn_w_gu,
           ffn_w_down, final_norm_g):
    b, t, d = x.shape
    n = b * t
    h = x.reshape(n, d)
    hsum = _head_sum_matrix()
    tri = _prefix_matrix()
    cos, sin = _rope_tables(t)
    ov = _overlap_matrix(t)
    ex = _expand_matrix(t)
    row = lambda a: a.reshape(1, -1)
    n_half = t // CMP_STRIDE
    assert t // SEL_BLOCK <= LANES and n_half <= LANES

    for l in range(w_in.shape[0]):
        c_g = RW_IN + NSA_WIDTH + 6 * KV_WIDTH
        n_gate = 3 * NSA_HEADS
        w_gate = jnp.pad(w_in[l][:, c_g:c_g + n_gate], ((0, 0), (0, LANES - n_gate))).astype(BF16)
        z64 = jnp.zeros((DECAY_LORA, RW_WIDTH), F32)
        wlora = jnp.concatenate([jnp.concatenate([rw_w_up[l], z64], axis=1),
                                 jnp.concatenate([z64, rw_a_up[l]], axis=1)], axis=0).astype(BF16)
        prep_consts = (row(shift_mu[l]), wlora, row(rw_w0[l]), row(rw_a0[l]), rw_g_up[l].astype(BF16),
                       row(rw_k_k[l]), row(rw_k_a[l]), row(rw_r_k[l]), tri, hsum)
        (q, kc, vc, kv, gn, gm, abar, rbar, btil, ktil, v_rw, gam, g_rw, bonus) = _in_proj(
            h, row(norm_mix_g[l]), w_in[l][:, :c_g].astype(BF16), w_gate,
            w_in[l][:, c_g + n_gate:].astype(BF16), prep_consts, 512, t)

        seq = lambda a: a.reshape(b, t, RW_WIDTH)
        abar, rbar, btil, ktil, v_rw = seq(abar), seq(rbar), seq(btil), seq(ktil), seq(v_rw)
        gam = gam.reshape(b, t // RW_CHUNK, RW_WIDTH)
        yn = _rwkv_scan(abar, rbar, btil, ktil, v_rw, gam)

        def w1_tokens(w1):
            hw = CMP_STRIDE * HEAD_DIM
            per_tok = jnp.concatenate([w1[:hw].reshape(CMP_STRIDE, HEAD_DIM, -1),
                                       w1[hw:].reshape(CMP_STRIDE, HEAD_DIM, -1)], axis=2)
            zero = jnp.zeros_like(per_tok)
            return jnp.concatenate([jnp.concatenate([per_tok, zero], axis=2),
                                    jnp.concatenate([zero, per_tok], axis=2)], axis=1).astype(BF16)

        dup = lambda w2: jnp.concatenate([w2, w2], axis=1).astype(BF16)
        kc2, vc2 = _nsa_cmp(kc.reshape(b, t, KV_WIDTH), vc.reshape(b, t, KV_WIDTH),
                            nsa_pe_k[l].reshape(1, -1), nsa_pe_v[l].reshape(1, -1),
                            w1_tokens(nsa_ck1[l]), nsa_ck1[l].astype(BF16), dup(nsa_ck2[l]),
                            w1_tokens(nsa_cv1[l]), nsa_cv1[l].astype(BF16), dup(nsa_cv2[l]))
        o_nsa = _nsa_attn(q.reshape(b, t, NSA_WIDTH), gn.reshape(b, t, LANES),
                          kv.reshape(b, t, 4 * KV_WIDTH), kc2, vc2, cos, sin, ov, ex)

        seq_of = lambda a, width: a.reshape(b, t, width)
        h = _mix_xattn(seq_of(h, d), yn, seq_of(bonus, RW_WIDTH), seq_of(g_rw, RW_WIDTH),
                       row(rw_ln_g[l]), row(rw_ln_b[l]), o_nsa, seq_of(gm, 2 * d),
                       w_up_rw[l].astype(BF16), w_up_nsa[l].astype(BF16), w_out[l].astype(BF16),
                       row(norm_xa_g[l]), xa_wq[l].astype(BF16), mem, row(norm_mem_g[l]),
                       xa_wkv[l].astype(BF16), xa_wo[l].astype(BF16), 512).reshape(n, d)

        last = l == w_in.shape[0] - 1
        gf = row(final_norm_g) if last else None
        assert last, "only a single layer is fused with the final norm"
        h = _ffn(h, row(norm_ffn_g[l]), ffn_w_gu[l].astype(BF16), ffn_w_down[l].astype(BF16), gf, 512)
    return h.reshape(b, t, d)
```

```python
import functools
import math

import numpy as np
import jax
import jax.numpy as jnp
from jax import lax
from jax.experimental import pallas as pl
from jax.experimental.pallas import tpu as pltpu

F32 = jnp.float32
BF16 = jnp.bfloat16

HEAD_DIM = 64
NORM_EPS = 1e-6
ROPE_THETA = 10000.0
RW_HEADS = 8
RW_WIDTH = RW_HEADS * HEAD_DIM
DECAY_LORA = 64
AAA_LORA = 64
GATE_LORA = 128
RW_GN_EPS = 64e-5
RW_IN = 3 * RW_WIDTH + DECAY_LORA + AAA_LORA + GATE_LORA
NSA_HEADS = 8
NSA_KV_GROUPS = 2
NSA_WIDTH = NSA_HEADS * HEAD_DIM
KV_WIDTH = NSA_KV_GROUPS * HEAD_DIM
CMP_BLOCK = 32
CMP_STRIDE = 16
CMP_HIDDEN = 128
SEL_BLOCK = 64
SEL_TOPK = 8
WINDOW = 512
XA_HEADS = 4

LANES = 128
VMEM_LIMIT = 56 * 1024 * 1024
RW_CHUNK = 64
NEG_BIG = -1e30
LOG2E = 1.4426950408889634


def _cparams(*sem):
    return pltpu.CompilerParams(dimension_semantics=sem, vmem_limit_bytes=VMEM_LIMIT)


def _mm(a, b):
    return jnp.dot(a.astype(BF16), b.astype(BF16), preferred_element_type=F32)


def _mm_nt(a, b):
    return lax.dot_general(a.astype(BF16), b.astype(BF16), (((1,), (1,)), ((), ())),
                           preferred_element_type=F32)


def _mm_tn(a, b):
    return lax.dot_general(a.astype(BF16), b.astype(BF16), (((0,), (0,)), ((), ())),
                           preferred_element_type=F32)


def _split(x, parts):
    out = []
    for _ in range(parts - 1):
        piece = x.astype(BF16)
        out.append(piece)
        x = x - piece.astype(F32)
    return out + [x.astype(BF16)]


def _split3(x):
    return _split(x, 3)


def _head_sums(x):
    lo_half = lax.broadcasted_iota(jnp.int32, (1, LANES), 1) < HEAD_DIM
    out = []
    for c in range(0, x.shape[1], LANES):
        t = x[:, c:c + LANES]
        lo = jnp.sum(jnp.where(lo_half, t, 0.0), axis=-1, keepdims=True)
        hi = jnp.sum(jnp.where(lo_half, 0.0, t), axis=-1, keepdims=True)
        out.append(jnp.where(lo_half, lo, hi))
    return jnp.concatenate(out, axis=1)


def _chunk_prefix_sums(x, chunk):
    pos = lax.broadcasted_iota(jnp.int32, (x.shape[0], 1), 0) % chunk
    shift = 1
    while shift < chunk:
        x = x + jnp.where(pos >= shift, pltpu.roll(x, shift, 0), 0.0)
        shift *= 2
    return x


def _rms(x, g):
    return x * lax.rsqrt(jnp.mean(x * x, axis=-1, keepdims=True) + NORM_EPS) * g


def _sigmoid(x):
    return 1.0 / (1.0 + jnp.exp(-x))


def _const_spec(shape):
    nd = len(shape)
    return pl.BlockSpec(shape, lambda *_: (0,) * nd)


def _in_proj_kernel(x_ref, g_ref, w_main_ref, w_gate_ref, w_merge_ref,
                    mu_ref, wlora_ref, w0_ref, a0_ref, gup_ref, kk_ref, ka_ref, rk_ref,
                    q_ref, kc_ref, vc_ref, kv_ref, gn_ref, gm_ref,
                    abar_ref, rbar_ref, btil_ref, ktil_ref, v_ref, gam_ref, gate_ref, bonus_ref,
                    p_sc, last_sc, *, rows_per_seq):
    i = pl.program_id(0)
    tm = x_ref.shape[0]
    w = RW_WIDTH
    hn = _rms(x_ref[...], g_ref[...]).astype(BF16)

    def project(o_ref, w_ref, base, c0, c1):
        o_ref[:, c0:c1] = jnp.dot(hn, w_ref[:, base + c0:base + c1],
                                  preferred_element_type=F32).astype(o_ref.dtype)

    def project_all(o_ref, w_ref, base):
        width = o_ref.shape[1]
        for c0 in range(0, width, 512):
            project(o_ref, w_ref, base, c0, min(c0 + 512, width))

    project_all(p_sc, w_main_ref, 0)
    off = RW_IN
    for o_ref in (q_ref, kc_ref, vc_ref, kv_ref):
        project_all(o_ref, w_main_ref, off)
        off += o_ref.shape[1]
    project_all(gn_ref, w_gate_ref, 0)

    cur = p_sc[...]
    prev_row = jnp.where((i * tm) % rows_per_seq == 0, 0.0, last_sc[7:8, :])
    row = lax.broadcasted_iota(jnp.int32, (tm, 1), 0)
    prev = jnp.where(row == 0, prev_row, pltpu.roll(cur, 1, 0))
    p = cur + mu_ref[...] * (prev - cur)
    last_sc[7:8, :] = cur[tm - 1:tm, :]

    r = p[:, 0:w]
    k = p[:, w:2 * w]
    v = p[:, 2 * w:3 * w]
    x_wa = p[:, 3 * w:3 * w + LANES]
    x_g = p[:, 3 * w + LANES:3 * w + 2 * LANES]
    lane = lax.broadcasted_iota(jnp.int32, (1, LANES), 1)
    x_wa = jnp.where(lane < DECAY_LORA, jnp.tanh(x_wa), x_wa)
    lin = _mm(x_wa, wlora_ref[...])
    w_lin = w0_ref[...] + lin[:, 0:w]
    a = _sigmoid(a0_ref[...] + lin[:, w:2 * w])
    gate_ref[...] = _mm(_sigmoid(x_g), gup_ref[...]).astype(BF16)

    logw = _sigmoid(w_lin) * (-math.exp(-0.5) * LOG2E)

    kk = k * kk_ref[...]
    kk = kk * lax.rsqrt(jnp.maximum(_head_sums(kk * kk), 1e-12))
    k2 = k * (1.0 + (a - 1.0) * ka_ref[...])
    bonus_ref[...] = (_head_sums(r * k2 * rk_ref[...]) * v).astype(BF16)
    v_ref[...] = v.astype(BF16)

    half = gm_ref.shape[1] // 2
    for c0 in range(0, half, 512):
        project(gm_ref, w_merge_ref, 0, c0, c0 + 512)

    L = RW_CHUNK
    cum_all = _chunk_prefix_sums(logw, L)
    kka = kk * a
    gam_rows = []
    for c in range(tm // L):
        sl = slice(c * L, (c + 1) * L)
        cum, gam = cum_all[sl], jnp.exp2(cum_all[(c + 1) * L - 1:(c + 1) * L])
        e_cum = jnp.exp2(cum)
        e_neg = 1.0 / e_cum
        abar_ref[sl, :] = (-kk[sl] * jnp.exp2(cum - logw[sl])).astype(BF16)
        rbar_ref[sl, :] = (r[sl] * e_cum).astype(BF16)
        btil_ref[sl, :] = (kka[sl] * e_neg).astype(BF16)
        ktil_ref[sl, :] = (k2[sl] * e_neg).astype(BF16)
        gam_rows.append(gam)
    gam_ref[...] = jnp.concatenate(gam_rows, axis=0)

    for c0 in range(half, 2 * half, 512):
        project(gm_ref, w_merge_ref, 0, c0, c0 + 512)


def _in_proj(x2, g, w_main, w_gate, w_merge, prep_consts, tm, rows_per_seq):
    n, d = x2.shape
    w = RW_WIDTH
    widths = (NSA_WIDTH, KV_WIDTH, KV_WIDTH, 4 * KV_WIDTH, LANES, 2 * d)
    dtypes = (BF16, F32, F32, BF16, F32, BF16)
    assert RW_IN + sum(widths[:-2]) == w_main.shape[1] and widths[-2:] == (w_gate.shape[1], w_merge.shape[1])
    assert n % tm == 0 and rows_per_seq % tm == 0 and tm % (8 * RW_CHUNK) == 0 and (2 * d) % 1024 == 0
    rows = lambda width: pl.BlockSpec((tm, width), lambda i: (i, 0))
    seq_bf = jax.ShapeDtypeStruct((n, w), BF16)
    return pl.pallas_call(
        functools.partial(_in_proj_kernel, rows_per_seq=rows_per_seq),
        grid=(n // tm,),
        in_specs=[rows(d), _const_spec((1, d)),
                  _const_spec(w_main.shape), _const_spec(w_gate.shape), _const_spec(w_merge.shape)]
        + [_const_spec(c.shape) for c in prep_consts],
        out_specs=[rows(wd) for wd in widths] + [rows(w)] * 5
        + [pl.BlockSpec((tm // RW_CHUNK, w), lambda i: (i, 0)), rows(w), rows(w)],
        out_shape=[jax.ShapeDtypeStruct((n, wd), dt) for wd, dt in zip(widths, dtypes)] + [seq_bf] * 5
        + [jax.ShapeDtypeStruct((n // RW_CHUNK, w), F32), seq_bf, seq_bf],
        scratch_shapes=[pltpu.VMEM((tm, RW_IN), F32), pltpu.VMEM((8, RW_IN), F32)],
        compiler_params=_cparams("arbitrary"),
        name="in_proj",
    )(x2, g, w_main, w_gate, w_merge, *prep_consts)


def _rwkv_scan_kernel(abar_ref, rbar_ref, btil_ref, ktil_ref, v_ref, gam_ref,
                      y_ref, s_ref):
    c = pl.program_id(1)
    L = RW_CHUNK

    @pl.when(c == 0)
    def _():
        s_ref[...] = jnp.zeros_like(s_ref)

    lane = lax.broadcasted_iota(jnp.int32, (1, LANES), 1)
    m0 = lane < HEAD_DIM
    ri = lax.broadcasted_iota(jnp.int32, (2 * L, 2 * L), 0)
    ci = lax.broadcasted_iota(jnp.int32, (2 * L, 2 * L), 1)
    same = (ri // L) == (ci // L)
    low_strict = same & (ci < ri)
    low_incl = same & (ci <= ri)
    zero = jnp.zeros((), BF16)
    nb = abar_ref.shape[0]
    n_pair = RW_HEADS // 2

    def stack(ref, bi, ls):
        x = ref[bi, :, ls]
        return jnp.concatenate([jnp.where(m0, x, zero), jnp.where(m0, zero, x)], axis=0)

    chains = [(bi, pr) for bi in range(nb) for pr in range(n_pair)]
    nch = len(chains)
    lanes_of = lambda pr: slice(pr * LANES, (pr + 1) * LANES)

    x_all = [jnp.concatenate([stack(abar_ref, bi, lanes_of(pr)), stack(rbar_ref, bi, lanes_of(pr))], axis=0)
             for bi, pr in chains]
    vs = [stack(v_ref, bi, lanes_of(pr)) for bi, pr in chains]
    s0 = [s_ref[bi * n_pair + pr] for bi, pr in chains]
    bk = [jnp.concatenate([stack(btil_ref, bi, lanes_of(pr)), stack(ktil_ref, bi, lanes_of(pr))], axis=0)
          for bi, pr in chains]
    amat = [_mm_nt(x_all[i], bk[i]) for i in range(nch)]
    xs = [_mm_nt(x_all[i], s0[i]) for i in range(nch)]

    u = [xs[i][0:2 * L] + _mm(jnp.where(low_strict, amat[i][0:2 * L, 2 * L:4 * L], 0.0), vs[i])
         for i in range(nch)]
    pw = [jnp.where(low_strict, a[0:2 * L, 0:2 * L], 0.0).astype(BF16) for a in amat]
    n_sq = int(math.log2(L))
    for step in range(n_sq):
        u = [u[i] + _mm(pw[i], u[i]) for i in range(nch)]
        if step + 1 < n_sq:
            pw = [_mm(q, q).astype(BF16) for q in pw]

    uv = [jnp.concatenate([u[i].astype(BF16), vs[i]], axis=0) for i in range(nch)]
    gam_rows = [gam_ref[bi, pl.ds(c % 8, 1), :] for bi in range(nb)]
    for i, (bi, pr) in enumerate(chains):
        gam = gam_rows[bi][:, lanes_of(pr)]
        bk_hat = bk[i].astype(F32) * gam
        s_ref[bi * n_pair + pr] = s0[i] * gam + _mm_tn(uv[i], bk_hat)
    def head_mean(z):
        lo = jnp.sum(jnp.where(m0, z, 0.0), axis=-1, keepdims=True)
        hi = jnp.sum(jnp.where(m0, 0.0, z), axis=-1, keepdims=True)
        return jnp.where(m0, lo, hi) * (1.0 / HEAD_DIM)

    for i, (bi, pr) in enumerate(chains):
        a_r =jnp.concatenate([jnp.where(low_incl, amat[i][2 * L:4 * L, 0:2 * L], 0.0),
                               jnp.where(low_incl, amat[i][2 * L:4 * L, 2 * L:4 * L], 0.0)], axis=1)
        y2 = xs[i][2 * L:4 * L] + _mm(a_r, uv[i])
        y = y2[0:L] + y2[L:2 * L]
        d = y - head_mean(y)
        y_ref[bi, :, lanes_of(pr)] = (d * lax.rsqrt(head_mean(d * d) + RW_GN_EPS)).astype(BF16)


RW_SCAN_BATCH = 8


def _rwkv_scan(abar, rbar, btil, ktil, v, gam):
    b, t, w = abar.shape
    L = RW_CHUNK
    nb = RW_SCAN_BATCH
    assert b % nb == 0
    tile = lambda: pl.BlockSpec((nb, L, w), lambda bi, c: (bi, c, 0))
    return pl.pallas_call(
        _rwkv_scan_kernel,
        grid=(b // nb, t // L),
        in_specs=[tile(), tile(), tile(), tile(), tile(),
                  pl.BlockSpec((nb, 8, w), lambda bi, c: (bi, c // 8, 0))],
        out_specs=tile(),
        out_shape=jax.ShapeDtypeStruct((b, t, w), BF16),
        scratch_shapes=[pltpu.VMEM((nb * (RW_HEADS // 2), LANES, LANES), F32)],
        compiler_params=_cparams("parallel", "arbitrary"),
        name="rwkv_scan",
    )(abar, rbar, btil, ktil, v, gam)


def _gelu_tanh(x):
    return 0.5 * x * (1.0 + jnp.tanh(math.sqrt(2.0 / math.pi) * (x + 0.044715 * x * x * x)))


def _nsa_cmp_kernel(kc_ref, vc_ref, pek_ref, pev_ref, k1_ref, k1f_ref, k2_ref, v1_ref, v1f_ref, v2_ref,
                    ko_ref, vo_ref):
    nh = ko_ref.shape[1]
    hid = CMP_HIDDEN
    for x_ref, pe_ref, w1_ref, w1f_ref, w2_ref, o_ref in (
            (kc_ref, pek_ref, k1_ref, k1f_ref, k2_ref, ko_ref),
            (vc_ref, pev_ref, v1_ref, v1f_ref, v2_ref, vo_ref)):
        z = jnp.zeros((nh, 2 * NSA_KV_GROUPS * hid), F32)
        for tok in range(CMP_STRIDE):
            z = z + _mm(x_ref[0, pl.ds(tok, nh, stride=CMP_STRIDE), :], w1_ref[tok])
        bias = _mm(jnp.broadcast_to(pe_ref[...], (8, pe_ref.shape[1])), w1f_ref[...])[0:1]
        for g in range(NSA_KV_GROUPS):
            zg = z[:, 2 * g * hid:2 * (g + 1) * hid]
            pre = zg[:, 0:hid] + pltpu.roll(zg[:, hid:2 * hid], nh - 1, 0) + bias
            o_ref[g] = _mm(_gelu_tanh(pre), w2_ref[...])


def _nsa_cmp(kc, vc, pe_k, pe_v, k1, k1f, k2d, v1, v1f, v2d):
    b, t, _ = kc.shape
    nh = t // CMP_STRIDE
    tile_in = lambda: pl.BlockSpec((1, t, LANES), lambda i: (i, 0, 0))
    tile_out = lambda: pl.BlockSpec((NSA_KV_GROUPS, nh, LANES), lambda i: (i, 0, 0))
    out = jax.ShapeDtypeStruct((b * NSA_KV_GROUPS, nh, LANES), F32)
    consts = (pe_k, pe_v, k1, k1f, k2d, v1, v1f, v2d)
    return pl.pallas_call(
        _nsa_cmp_kernel,
        grid=(b,),
        in_specs=[tile_in(), tile_in()] + [_const_spec(c.shape) for c in consts],
        out_specs=[tile_out(), tile_out()],
        out_shape=[out, out],
        compiler_params=_cparams("parallel"),
        name="nsa_cmp",
    )(kc, vc, pe_k, pe_v, k1, k1f, k2d, v1, v1f, v2d)


NSA_TQ = 256
NSA_TK = 256
NSA_LOOKAHEAD = 1
NSA_UNROLL = 3


def _rope_swap(x):
    lane = lax.broadcasted_iota(jnp.int32, (1, LANES), 1)
    first = (lane % HEAD_DIM) < (HEAD_DIM // 2)
    return jnp.where(first, pltpu.roll(x, LANES - HEAD_DIM // 2, 1), pltpu.roll(x, HEAD_DIM // 2, 1))


def _masked_softmax_parts(s, mask):
    s = jnp.where(mask, s, NEG_BIG)
    e = jnp.where(mask, jnp.exp(s - jnp.max(s, axis=-1, keepdims=True)), 0.0)
    return e / jnp.maximum(jnp.sum(e, axis=-1, keepdims=True), 1e-30)


NSA_ACC_ROWS = HEAD_DIM + 16


def _nsa_attn_kernel(q_ref, gn_ref, ks_ref, vs_ref, kw_ref, vw_ref, kc_ref, vc_ref,
                     cosq_ref, sinq_ref, cosk_ref, sink_ref, ovt_ref, ext_ref,
                     o_ref,
                     k2_sc, vt_sc, qrot_sc, sel_sc, m_sc, acc_sc):
    i = pl.program_id(1)
    tq, tk = NSA_TQ, NSA_TK
    t = ks_ref.shape[1]
    n_kt = t // tk
    n_sel = t // SEL_BLOCK
    lane = lax.broadcasted_iota(jnp.int32, (1, LANES), 1)
    lo_half = lane < HEAD_DIM

    @pl.when(i == 0)
    def _():
        ones = jnp.ones((NSA_ACC_ROWS - HEAD_DIM, tk), BF16)
        for jt in range(n_kt):
            rs = slice(jt * tk, (jt + 1) * tk)
            cos, sin = cosk_ref[rs, :], sink_ref[rs, :]
            for br, k_src, v_src in ((0, ks_ref, vs_ref), (1, kw_ref, vw_ref)):
                x = k_src[0, rs, :].astype(F32)
                x = x * cos + _rope_swap(x) * sin
                xr = pltpu.roll(x, HEAD_DIM, 1)
                k2_sc[br, 0, rs, :] = jnp.where(lo_half, x, xr).astype(BF16)
                k2_sc[br, 1, rs, :] = jnp.where(lo_half, xr, x).astype(BF16)
                v_t = v_src[0, rs, :].astype(F32).T.astype(BF16)
                for g in range(NSA_KV_GROUPS):
                    vt_sc[br, g, jt, 0:HEAD_DIM, :] = v_t[g * HEAD_DIM:(g + 1) * HEAD_DIM]
                    vt_sc[br, g, jt, HEAD_DIM:NSA_ACC_ROWS, :] = ones

    tq_row = i * tq + lax.broadcasted_iota(jnp.int32, (1, tq), 1)
    tq4_row = jnp.concatenate([tq_row] * 4, axis=1)
    gates_t = _sigmoid(gn_ref[0]).T
    cosq, sinq = cosq_ref[...], sinq_ref[...]
    qscale = HEAD_DIM ** -0.5 * LOG2E
    nt_dims = (((1,), (1,)), ((), ()))
    cols = lambda hh: slice(hh * tq, (hh + 1) * tq)

    def stack_heads(xa, xb):
        parts = [jnp.where(lo_half, xa, 0.0), jnp.where(lo_half, 0.0, xa),
                 jnp.where(lo_half, xb, 0.0), jnp.where(lo_half, 0.0, xb)]
        return (jnp.concatenate(parts, axis=0) * qscale).astype(BF16)

    j_last = (i * tq + tq - 1) // tk
    j_first_win = jnp.maximum(i * tq - (WINDOW - 1), 0) // tk
    blk = lax.broadcasted_iota(jnp.int32, (n_sel, 1), 0)
    cur = tq_row // SEL_BLOCK
    forced = (blk == 0) | (blk == cur) | (blk == cur - 1)
    allowed = blk <= cur
    n_valid = (t - CMP_BLOCK) // CMP_STRIDE + 1
    cblk = lax.broadcasted_iota(jnp.int32, (LANES, 1), 0)
    cmask = (cblk * CMP_STRIDE + (CMP_BLOCK - 1) <= tq4_row) & (cblk < n_valid)

    o_cmp = []
    for g in range(NSA_KV_GROUPS):
        qa = q_ref[0, :, (2 * g) * LANES:(2 * g + 1) * LANES].astype(F32)
        qb = q_ref[0, :, (2 * g + 1) * LANES:(2 * g + 2) * LANES].astype(F32)
        q_cmp = stack_heads(qa, qb)
        qrot_sc[g] = stack_heads(qa * cosq + _rope_swap(qa) * sinq, qb * cosq + _rope_swap(qb) * sinq)

        s_c = lax.dot_general(kc_ref[g].astype(BF16), q_cmp, nt_dims, preferred_element_type=F32)
        s_c = jnp.where(cmask, s_c, NEG_BIG)
        e_c = jnp.where(cmask, jnp.exp2(s_c - jnp.max(s_c, axis=0, keepdims=True)), 0.0)
        p_c = e_c / jnp.maximum(jnp.sum(e_c, axis=0, keepdims=True), 1e-30)
        o_cmp.append(_mm(vc_ref[g].T[0:HEAD_DIM], p_c))

        p_sum = p_c[:, cols(0)] + p_c[:, cols(1)] + p_c[:, cols(2)] + p_c[:, cols(3)]
        imp = sum(jnp.dot(ovt_ref[...], part, preferred_element_type=F32) for part in _split3(p_sum))
        imp = jnp.where(forced, 1e4, jnp.where(allowed, imp, -1.0))
        rank = jnp.zeros((n_sel, tq), F32)
        for b in range(n_sel):
            row_b = imp[b:b + 1, :]
            rank = rank + jnp.where((row_b > imp) | ((row_b == imp) & (b < blk)), 1.0, 0.0)
        sel_sc[g] = jnp.where(rank < min(SEL_TOPK, n_sel), 1.0, 0.0).astype(BF16)

    for ch in range(2 * NSA_KV_GROUPS):
        m_sc[ch] = jnp.full(m_sc.shape[1:], NEG_BIG, F32)
        acc_sc[ch] = jnp.zeros(acc_sc.shape[1:], F32)

    def run(items):
        def scores(j, pos0, br, g):
            r0 = pl.multiple_of(j * tk, tk)
            kpos = pos0 + lax.broadcasted_iota(jnp.int32, (tk, 1), 0)
            s = lax.dot_general(k2_sc[br, g, pl.ds(r0, tk), :], qrot_sc[g], nt_dims,
                                preferred_element_type=F32)
            if br == 0:
                hit = jnp.dot(ext_ref[pl.ds(r0, tk), :], sel_sc[g], preferred_element_type=F32)
                keep = (kpos <= tq_row) & (hit > 0.5)
            else:
                keep = (kpos <= tq_row) & (kpos > tq_row - WINDOW)
            return s + jnp.concatenate([jnp.where(keep, 0.0, NEG_BIG)] * 4, axis=1)

        pending = [scores(*it) for it in items[:NSA_LOOKAHEAD]]
        for n, (j, _, br, g) in enumerate(items):
            if n + NSA_LOOKAHEAD < len(items):
                pending.append(scores(*items[n + NSA_LOOKAHEAD]))
            s_cur = pending.pop(0)
            ch = br * NSA_KV_GROUPS + g
            m_old = m_sc[ch]
            m_new = jnp.maximum(m_old, jnp.max(s_cur, axis=0, keepdims=True))
            p = jnp.exp2(s_cur - m_new).astype(BF16)
            m_sc[ch] = m_new
            acc_sc[ch] = jnp.exp2(m_old - m_new) * acc_sc[ch] + jnp.dot(
                vt_sc[br, g, j], p, preferred_element_type=F32)

    def key_tiles(lo, hi, branches):
        done = lo
        for width in range(NSA_UNROLL, 0, -1):
            n_blocks = (hi - done) // width

            def body(blk, carry, width=width, start=done):
                j0 = start + width * blk
                run([(j, j * tk, br, g) for j in [j0 + d for d in range(width)]
                     for g in range(NSA_KV_GROUPS) for br in branches])
                return carry

            lax.fori_loop(0, n_blocks, body, 0)
            done = done + width * n_blocks

    key_tiles(0, j_first_win, (0,))
    key_tiles(j_first_win, j_last + 1, (0, 1))

    def finish(ch):
        acc = acc_sc[ch]
        out = acc[0:HEAD_DIM] / jnp.maximum(acc[HEAD_DIM:HEAD_DIM + 1], 1e-30)
        return jnp.where(m_sc[ch] > 0.5 * NEG_BIG, out, 0.0)

    for g in range(NSA_KV_GROUPS):
        o_sel, o_win = finish(g), finish(NSA_KV_GROUPS + g)

        def head_out(hh):
            r = (4 * g + hh) * 3
            return (gates_t[r:r + 1] * o_cmp[g][:, cols(hh)] + gates_t[r + 1:r + 2] * o_sel[:, cols(hh)]
                    + gates_t[r + 2:r + 3] * o_win[:, cols(hh)])

        for pr in range(2):
            o_ref[0, :, (2 * g + pr) * LANES:(2 * g + pr + 1) * LANES] = jnp.concatenate(
                [head_out(2 * pr), head_out(2 * pr + 1)], axis=0).T.astype(BF16)


def _nsa_attn(q, gn, kv, kc2, vc2, cos, sin, ovt, ex):
    b, t, _ = q.shape
    tq, tk = NSA_TQ, NSA_TK
    g = NSA_KV_GROUPS
    assert t % tq == 0 and t % tk == 0
    full = lambda col: pl.BlockSpec((1, t, LANES), lambda bi, i: (bi, 0, col))
    return pl.pallas_call(
        _nsa_attn_kernel,
        grid=(b, t // tq),
        in_specs=[pl.BlockSpec((1, tq, NSA_WIDTH), lambda bi, i: (bi, i, 0)),
                  pl.BlockSpec((1, tq, LANES), lambda bi, i: (bi, i, 0)),
                  full(0), full(1), full(2), full(3),
                  pl.BlockSpec((g, kc2.shape[1], LANES), lambda bi, i: (bi, 0, 0)),
                  pl.BlockSpec((g, vc2.shape[1], LANES), lambda bi, i: (bi, 0, 0)),
                  pl.BlockSpec((tq, LANES), lambda bi, i: (i, 0)),
                  pl.BlockSpec((tq, LANES), lambda bi, i: (i, 0)),
                  _const_spec(cos.shape), _const_spec(sin.shape),
                  _const_spec(ovt.shape), _const_spec(ex.shape)],
        out_specs=pl.BlockSpec((1, tq, NSA_WIDTH), lambda bi, i: (bi, i, 0)),
        out_shape=jax.ShapeDtypeStruct((b, t, NSA_WIDTH), BF16),
        scratch_shapes=[pltpu.VMEM((2, g, t, LANES), BF16),
                        pltpu.VMEM((2, g, t // tk, NSA_ACC_ROWS, tk), BF16),
                        pltpu.VMEM((g, 4 * tq, LANES), BF16),
                        pltpu.VMEM((g, t // SEL_BLOCK, tq), BF16),
                        pltpu.VMEM((2 * g, 1, 4 * tq), F32),
                        pltpu.VMEM((2 * g, NSA_ACC_ROWS, 4 * tq), F32)],
        compiler_params=_cparams("parallel", "arbitrary"),
        name="nsa_attn",
    )(q, gn, kv, kv, kv, kv, kc2, vc2, cos, sin, cos, sin, ovt, ex)


def _mix_xattn_kernel(x_ref, yn_ref, bonus_ref, g_ref, lng_ref, lnb_ref, on_ref, gm_ref,
                      wrw_ref, wnsa_ref, wout_ref,
                      gx_ref, wq_ref, mem_ref, gmem_ref, wkv_ref, wo_ref,
                      o_ref, kv_sc):
    d = x_ref.shape[2]
    hd = d // XA_HEADS

    @pl.when(pl.program_id(1) == 0)
    def _():
        kv_sc[...] = _mm(_rms(mem_ref[0], gmem_ref[...]), wkv_ref[...]).astype(BF16)

    y_rw = ((yn_ref[0].astype(F32) * lng_ref[...] + lnb_ref[...] + bonus_ref[0].astype(F32))
            * g_ref[0].astype(F32))
    t_rw = _mm(y_rw, wrw_ref[...])
    t_ns = _mm(on_ref[0], wnsa_ref[...])
    gm = gm_ref[0].astype(F32)
    mix = _sigmoid(gm[:, 0:d]) * t_rw + _sigmoid(gm[:, d:2 * d]) * t_ns
    h = x_ref[0] + _mm(mix, wout_ref[...])

    q = (_mm(_rms(h, gx_ref[...]), wq_ref[...]) * hd ** -0.5).astype(BF16)

    def scores(hh):
        cs = slice(hh * hd, (hh + 1) * hd)
        return lax.dot_general(q[:, cs], kv_sc[:, cs], (((1,), (1,)), ((), ())), preferred_element_type=F32)

    outs = []
    s_next = scores(0)
    for hh in range(XA_HEADS):
        s, s_next = s_next, (scores(hh + 1) if hh + 1 < XA_HEADS else None)
        e = jnp.exp(s - jnp.max(s, axis=-1, keepdims=True))
        p = e * (1.0 / jnp.sum(e, axis=-1, keepdims=True))
        outs.append(jnp.dot(p.astype(BF16), kv_sc[:, d + hh * hd:d + (hh + 1) * hd],
                            preferred_element_type=F32))
    o_ref[0] = h + _mm(jnp.concatenate(outs, axis=1), wo_ref[...])


def _mix_xattn(x3, yn, bonus, g, ln_g, ln_b, o_nsa, gm, w_rw, w_nsa, w_out, g_xa, wq, mem, g_mem, wkv, wo, tm):
    b, t, d = x3.shape
    m = mem.shape[1]
    rows = lambda width: pl.BlockSpec((1, tm, width), lambda bi, i: (bi, i, 0))
    consts = lambda *arrs: [_const_spec(a.shape) for a in arrs]
    return pl.pallas_call(
        _mix_xattn_kernel,
        grid=(b, t // tm),
        in_specs=[rows(d), rows(RW_WIDTH), rows(RW_WIDTH), rows(RW_WIDTH)] + consts(ln_g, ln_b)
        + [rows(NSA_WIDTH), rows(2 * d)] + consts(w_rw, w_nsa, w_out, g_xa, wq)
        + [pl.BlockSpec((1, m, d), lambda bi, i: (bi, 0, 0))] + consts(g_mem, wkv, wo),
        out_specs=rows(d),
        out_shape=jax.ShapeDtypeStruct((b, t, d), F32),
        scratch_shapes=[pltpu.VMEM((m, 2 * d), BF16)],
        compiler_params=_cparams("parallel", "arbitrary"),
        name="mix_xattn",
    )(x3, yn, bonus, g, ln_g, ln_b, o_nsa, gm, w_rw, w_nsa, w_out, g_xa, wq, mem, g_mem, wkv, wo)


FFN_CHUNK = 256


def _ffn_kernel(h_ref, g_ref, wgu_ref, wd_ref, gf_ref, o_ref):
    h = h_ref[...]
    hn = _rms(h, g_ref[...]).astype(BF16)
    dff = wd_ref.shape[0]
    acc = jnp.zeros(h.shape, F32)
    for c0 in range(0, dff, FFN_CHUNK):
        gate = jnp.dot(hn, wgu_ref[:, c0:c0 + FFN_CHUNK], preferred_element_type=F32)
        up = jnp.dot(hn, wgu_ref[:, dff + c0:dff + c0 + FFN_CHUNK], preferred_element_type=F32)
        act = (gate * _sigmoid(gate) * up).astype(BF16)
        acc = acc + jnp.dot(act, wd_ref[c0:c0 + FFN_CHUNK, :], preferred_element_type=F32)
    o_ref[...] = _rms(h + acc, gf_ref[...])


def _ffn(h2, g, wgu, wd, gf, tm):
    n, d = h2.shape
    assert wd.shape[0] % FFN_CHUNK == 0
    return pl.pallas_call(
        _ffn_kernel,
        grid=(n // tm,),
        in_specs=[pl.BlockSpec((tm, d), lambda i: (i, 0)), _const_spec(g.shape),
                  _const_spec(wgu.shape), _const_spec(wd.shape), _const_spec(gf.shape)],
        out_specs=pl.BlockSpec((tm, d), lambda i: (i, 0)),
        out_shape=jax.ShapeDtypeStruct((n, d), F32),
        compiler_params=_cparams("parallel"),
        name="ffn",
    )(h2, g, wgu, wd, gf)


def _rope_tables(t):
    half = HEAD_DIM // 2
    inv_freq = ROPE_THETA ** (-np.arange(half, dtype=np.float64) / half)
    ang = np.arange(t, dtype=np.float64)[:, None] * inv_freq[None, :]
    cos, sin = np.cos(ang), np.sin(ang)
    cos128 = np.tile(cos, (1, LANES // half))
    sin128 = np.tile(np.concatenate([-sin, sin], axis=1), (1, LANES // HEAD_DIM))
    return jnp.asarray(cos128, F32), jnp.asarray(sin128, F32)


def _overlap_matrix(t):
    n_cmp = (t - CMP_BLOCK) // CMP_STRIDE + 1
    n_sel = t // SEL_BLOCK
    cs = np.arange(n_cmp) * CMP_STRIDE
    ss = np.arange(n_sel) * SEL_BLOCK
    ov = np.clip(np.minimum(cs[:, None] + CMP_BLOCK, ss[None, :] + SEL_BLOCK)
                 - np.maximum(cs[:, None], ss[None, :]), 0, None) / CMP_BLOCK
    out = np.zeros((n_sel, LANES), np.float32)
    out[:, :n_cmp] = ov.T
    return jnp.asarray(out, BF16)


def _expand_matrix(t):
    blk = np.arange(t) // SEL_BLOCK
    return jnp.asarray(blk[:, None] == np.arange(t // SEL_BLOCK)[None, :], BF16)


def kernel(x, mem, norm_mix_g, w_in, shift_mu, rw_w_up, rw_w0, rw_a_up, rw_a0, rw_g_up, rw_k_k, rw_k_a,
           rw_r_k, rw_ln_g, rw_ln_b, nsa_pe_k, nsa_pe_v, nsa_ck1, nsa_ck2, nsa_cv1, nsa_cv2, w_up_rw,
           w_up_nsa, w_out, norm_xa_g, norm_mem_g, xa_wq, xa_wkv, xa_wo, norm_ffn_g, ffn_w_gu,
           ffn_w_down, final_norm_g):
    b, t, d = x.shape
    n = b * t
    h = x.reshape(n, d)
    cos, sin = _rope_tables(t)
    ov = _overlap_matrix(t)
    ex = _expand_matrix(t)
    row = lambda a: a.reshape(1, -1)
    n_half = t // CMP_STRIDE
    assert t // SEL_BLOCK <= LANES and n_half <= LANES

    for l in range(w_in.shape[0]):
        c_g = RW_IN + NSA_WIDTH + 6 * KV_WIDTH
        n_gate = 3 * NSA_HEADS
        w_gate = jnp.pad(w_in[l][:, c_g:c_g + n_gate], ((0, 0), (0, LANES - n_gate))).astype(BF16)
        z64 = jnp.zeros((DECAY_LORA, RW_WIDTH), F32)
        wlora = jnp.concatenate([jnp.concatenate([rw_w_up[l], z64], axis=1),
                                 jnp.concatenate([z64, rw_a_up[l]], axis=1)], axis=0).astype(BF16)
        prep_consts = (row(shift_mu[l]), wlora, row(rw_w0[l]), row(rw_a0[l]), rw_g_up[l].astype(BF16),
                       row(rw_k_k[l]), row(rw_k_a[l]), row(rw_r_k[l]))
        (q, kc, vc, kv, gn, gm, abar, rbar, btil, ktil, v_rw, gam, g_rw, bonus) = _in_proj(
            h, row(norm_mix_g[l]), w_in[l][:, :c_g].astype(BF16), w_gate,
            w_in[l][:, c_g + n_gate:].astype(BF16), prep_consts, 512, t)

        seq = lambda a: a.reshape(b, t, RW_WIDTH)
        abar, rbar, btil, ktil, v_rw = seq(abar), seq(rbar), seq(btil), seq(ktil), seq(v_rw)
        gam = gam.reshape(b, t // RW_CHUNK, RW_WIDTH)
        yn = _rwkv_scan(abar, rbar, btil, ktil, v_rw, gam)

        def w1_tokens(w1):
            hw = CMP_STRIDE * HEAD_DIM
            per_tok = jnp.concatenate([w1[:hw].reshape(CMP_STRIDE, HEAD_DIM, -1),
                                       w1[hw:].reshape(CMP_STRIDE, HEAD_DIM, -1)], axis=2)
            zero = jnp.zeros_like(per_tok)
            return jnp.concatenate([jnp.concatenate([per_tok, zero], axis=2),
                                    jnp.concatenate([zero, per_tok], axis=2)], axis=1).astype(BF16)

        dup = lambda w2: jnp.concatenate([w2, w2], axis=1).astype(BF16)
        kc2, vc2 = _nsa_cmp(kc.reshape(b, t, KV_WIDTH), vc.reshape(b, t, KV_WIDTH),
                            nsa_pe_k[l].reshape(1, -1), nsa_pe_v[l].reshape(1, -1),
                            w1_tokens(nsa_ck1[l]), nsa_ck1[l].astype(BF16), dup(nsa_ck2[l]),
                            w1_tokens(nsa_cv1[l]), nsa_cv1[l].astype(BF16), dup(nsa_cv2[l]))
        o_nsa = _nsa_attn(q.reshape(b, t, NSA_WIDTH), gn.reshape(b, t, LANES),
                          kv.reshape(b, t, 4 * KV_WIDTH), kc2, vc2, cos, sin, ov, ex)

        seq_of = lambda a, width: a.reshape(b, t, width)
        h = _mix_xattn(seq_of(h, d), yn, seq_of(bonus, RW_WIDTH), seq_of(g_rw, RW_WIDTH),
                       row(rw_ln_g[l]), row(rw_ln_b[l]), o_nsa, seq_of(gm, 2 * d),
                       w_up_rw[l].astype(BF16), w_up_nsa[l].astype(BF16), w_out[l].astype(BF16),
                       row(norm_xa_g[l]), xa_wq[l].astype(BF16), mem, row(norm_mem_g[l]),
                       xa_wkv[l].astype(BF16), xa_wo[l].astype(BF16), 512).reshape(n, d)

        last = l == w_in.shape[0] - 1
        gf = row(final_norm_g) if last else None
        assert last, "only a single layer is fused with the final norm"
        h = _ffn(h, row(norm_ffn_g[l]), ffn_w_gu[l].astype(BF16), ffn_w_down[l].astype(BF16), gf, 512)
    return h.reshape(b, t, d)
```

```python
import functools
import math

import numpy as np
import jax
import jax.numpy as jnp
from jax import lax
from jax.experimental import pallas as pl
from jax.experimental.pallas import tpu as pltpu

F32 = jnp.float32
BF16 = jnp.bfloat16

HEAD_DIM = 64
NORM_EPS = 1e-6
ROPE_THETA = 10000.0
RW_HEADS = 8
RW_WIDTH = RW_HEADS * HEAD_DIM
DECAY_LORA = 64
AAA_LORA = 64
GATE_LORA = 128
RW_GN_EPS = 64e-5
RW_IN = 3 * RW_WIDTH + DECAY_LORA + AAA_LORA + GATE_LORA
NSA_HEADS = 8
NSA_KV_GROUPS = 2
NSA_WIDTH = NSA_HEADS * HEAD_DIM
KV_WIDTH = NSA_KV_GROUPS * HEAD_DIM
CMP_BLOCK = 32
CMP_STRIDE = 16
CMP_HIDDEN = 128
SEL_BLOCK = 64
SEL_TOPK = 8
WINDOW = 512
XA_HEADS = 4

LANES = 128
VMEM_LIMIT = 56 * 1024 * 1024
RW_CHUNK = 64
ROW_TILE = 512
NEG_BIG = -1e30
LOG2E = 1.4426950408889634


def _cparams(*sem):
    return pltpu.CompilerParams(dimension_semantics=sem, vmem_limit_bytes=VMEM_LIMIT)


def _mm(a, b):
    return jnp.dot(a.astype(BF16), b.astype(BF16), preferred_element_type=F32)


def _mm_nt(a, b):
    return lax.dot_general(a.astype(BF16), b.astype(BF16), (((1,), (1,)), ((), ())),
                           preferred_element_type=F32)


def _mm_tn(a, b):
    return lax.dot_general(a.astype(BF16), b.astype(BF16), (((0,), (0,)), ((), ())),
                           preferred_element_type=F32)


def _split(x, parts):
    out = []
    for _ in range(parts - 1):
        piece = x.astype(BF16)
        out.append(piece)
        x = x - piece.astype(F32)
    return out + [x.astype(BF16)]


def _split3(x):
    return _split(x, 3)


def _head_sums(x):
    lo_half = lax.broadcasted_iota(jnp.int32, (1, LANES), 1) < HEAD_DIM
    out = []
    for c in range(0, x.shape[1], LANES):
        t = x[:, c:c + LANES]
        lo = jnp.sum(jnp.where(lo_half, t, 0.0), axis=-1, keepdims=True)
        hi = jnp.sum(jnp.where(lo_half, 0.0, t), axis=-1, keepdims=True)
        out.append(jnp.where(lo_half, lo, hi))
    return jnp.concatenate(out, axis=1)


def _chunk_prefix_sums(x, chunk):
    pos = lax.broadcasted_iota(jnp.int32, (x.shape[0], 1), 0) % chunk
    shift = 1
    while shift < chunk:
        x = x + jnp.where(pos >= shift, pltpu.roll(x, shift, 0), 0.0)
        shift *= 2
    return x


def _rms(x, g):
    return x * lax.rsqrt(jnp.mean(x * x, axis=-1, keepdims=True) + NORM_EPS) * g


def _sigmoid(x):
    return 1.0 / (1.0 + jnp.exp(-x))


def _const_spec(shape):
    nd = len(shape)
    return pl.BlockSpec(shape, lambda *_: (0,) * nd)


def _in_proj_kernel(x_ref, g_ref, w_main_ref, w_gate_ref, w_merge_ref,
                    mu_ref, wlora_ref, w0_ref, a0_ref, gup_ref, kk_ref, ka_ref, rk_ref,
                    q_ref, kc_ref, vc_ref, kv_ref, gn_ref, gm_ref,
                    abar_ref, rbar_ref, btil_ref, ktil_ref, v_ref, gam_ref, gate_ref, bonus_ref,
                    p_sc, last_sc, *, rows_per_seq):
    i = pl.program_id(0)
    tm = x_ref.shape[0]
    w = RW_WIDTH
    hn = _rms(x_ref[...], g_ref[...]).astype(BF16)

    def project(o_ref, w_ref, base, c0, c1):
        o_ref[:, c0:c1] = jnp.dot(hn, w_ref[:, base + c0:base + c1],
                                  preferred_element_type=F32).astype(o_ref.dtype)

    def project_all(o_ref, w_ref, base):
        width = o_ref.shape[1]
        for c0 in range(0, width, 512):
            project(o_ref, w_ref, base, c0, min(c0 + 512, width))

    project_all(p_sc, w_main_ref, 0)
    off = RW_IN
    for o_ref in (q_ref, kc_ref, vc_ref, kv_ref):
        project_all(o_ref, w_main_ref, off)
        off += o_ref.shape[1]
    project_all(gn_ref, w_gate_ref, 0)

    cur = p_sc[...]
    prev_row = jnp.where((i * tm) % rows_per_seq == 0, 0.0, last_sc[7:8, :])
    row = lax.broadcasted_iota(jnp.int32, (tm, 1), 0)
    prev = jnp.where(row == 0, prev_row, pltpu.roll(cur, 1, 0))
    p = cur + mu_ref[...] * (prev - cur)
    last_sc[7:8, :] = cur[tm - 1:tm, :]

    r = p[:, 0:w]
    k = p[:, w:2 * w]
    v = p[:, 2 * w:3 * w]
    x_wa = p[:, 3 * w:3 * w + LANES]
    x_g = p[:, 3 * w + LANES:3 * w + 2 * LANES]
    lane = lax.broadcasted_iota(jnp.int32, (1, LANES), 1)
    x_wa = jnp.where(lane < DECAY_LORA, jnp.tanh(x_wa), x_wa)
    lin = _mm(x_wa, wlora_ref[...])
    w_lin = w0_ref[...] + lin[:, 0:w]
    a = _sigmoid(a0_ref[...] + lin[:, w:2 * w])
    gate_ref[...] = _mm(_sigmoid(x_g), gup_ref[...]).astype(BF16)

    logw = _sigmoid(w_lin) * (-math.exp(-0.5) * LOG2E)

    kk = k * kk_ref[...]
    kk = kk * lax.rsqrt(jnp.maximum(_head_sums(kk * kk), 1e-12))
    k2 = k * (1.0 + (a - 1.0) * ka_ref[...])
    bonus_ref[...] = (_head_sums(r * k2 * rk_ref[...]) * v).astype(BF16)
    v_ref[...] = v.astype(BF16)

    half = gm_ref.shape[1] // 2
    for c0 in range(0, half, 512):
        project(gm_ref, w_merge_ref, 0, c0, c0 + 512)

    L = RW_CHUNK
    cum_all = _chunk_prefix_sums(logw, L)
    kka = kk * a
    gam_rows = []
    for c in range(tm // L):
        sl = slice(c * L, (c + 1) * L)
        cum, gam = cum_all[sl], jnp.exp2(cum_all[(c + 1) * L - 1:(c + 1) * L])
        e_cum = jnp.exp2(cum)
        e_neg = 1.0 / e_cum
        abar_ref[sl, :] = (-kk[sl] * jnp.exp2(cum - logw[sl])).astype(BF16)
        rbar_ref[sl, :] = (r[sl] * e_cum).astype(BF16)
        btil_ref[sl, :] = (kka[sl] * e_neg).astype(BF16)
        ktil_ref[sl, :] = (k2[sl] * e_neg).astype(BF16)
        gam_rows.append(gam)
    gam_ref[...] = jnp.concatenate(gam_rows, axis=0)

    for c0 in range(half, 2 * half, 512):
        project(gm_ref, w_merge_ref, 0, c0, c0 + 512)


def _in_proj(x2, g, w_main, w_gate, w_merge, prep_consts, tm, rows_per_seq):
    n, d = x2.shape
    w = RW_WIDTH
    widths = (NSA_WIDTH, KV_WIDTH, KV_WIDTH, 4 * KV_WIDTH, LANES, 2 * d)
    dtypes = (BF16, F32, F32, BF16, F32, BF16)
    assert RW_IN + sum(widths[:-2]) == w_main.shape[1] and widths[-2:] == (w_gate.shape[1], w_merge.shape[1])
    assert n % tm == 0 and rows_per_seq % tm == 0 and tm % (8 * RW_CHUNK) == 0 and (2 * d) % 1024 == 0
    rows = lambda width: pl.BlockSpec((tm, width), lambda i: (i, 0))
    seq_bf = jax.ShapeDtypeStruct((n, w), BF16)
    return pl.pallas_call(
        functools.partial(_in_proj_kernel, rows_per_seq=rows_per_seq),
        grid=(n // tm,),
        in_specs=[rows(d), _const_spec((1, d)),
                  _const_spec(w_main.shape), _const_spec(w_gate.shape), _const_spec(w_merge.shape)]
        + [_const_spec(c.shape) for c in prep_consts],
        out_specs=[rows(wd) for wd in widths] + [rows(w)] * 5
        + [pl.BlockSpec((tm // RW_CHUNK, w), lambda i: (i, 0)), rows(w), rows(w)],
        out_shape=[jax.ShapeDtypeStruct((n, wd), dt) for wd, dt in zip(widths, dtypes)] + [seq_bf] * 5
        + [jax.ShapeDtypeStruct((n // RW_CHUNK, w), F32), seq_bf, seq_bf],
        scratch_shapes=[pltpu.VMEM((tm, RW_IN), F32), pltpu.VMEM((8, RW_IN), F32)],
        compiler_params=_cparams("arbitrary"),
        name="in_proj",
    )(x2, g, w_main, w_gate, w_merge, *prep_consts)


def _rwkv_scan_kernel(abar_ref, rbar_ref, btil_ref, ktil_ref, v_ref, gam_ref,
                      y_ref, s_ref):
    c = pl.program_id(1)
    L = RW_CHUNK

    @pl.when(c == 0)
    def _():
        s_ref[...] = jnp.zeros_like(s_ref)

    lane = lax.broadcasted_iota(jnp.int32, (1, LANES), 1)
    m0 = lane < HEAD_DIM
    ri = lax.broadcasted_iota(jnp.int32, (2 * L, 2 * L), 0)
    ci = lax.broadcasted_iota(jnp.int32, (2 * L, 2 * L), 1)
    same = (ri // L) == (ci // L)
    low_strict = same & (ci < ri)
    low_incl = same & (ci <= ri)
    zero = jnp.zeros((), BF16)
    nb = abar_ref.shape[0]
    n_pair = RW_HEADS // 2

    def stack(ref, bi, ls):
        x = ref[bi, :, ls]
        return jnp.concatenate([jnp.where(m0, x, zero), jnp.where(m0, zero, x)], axis=0)

    chains = [(bi, pr) for bi in range(nb) for pr in range(n_pair)]
    nch = len(chains)
    lanes_of = lambda pr: slice(pr * LANES, (pr + 1) * LANES)

    x_all = [jnp.concatenate([stack(abar_ref, bi, lanes_of(pr)), stack(rbar_ref, bi, lanes_of(pr))], axis=0)
             for bi, pr in chains]
    vs = [stack(v_ref, bi, lanes_of(pr)) for bi, pr in chains]
    s0 = [s_ref[bi * n_pair + pr] for bi, pr in chains]
    bk = [jnp.concatenate([stack(btil_ref, bi, lanes_of(pr)), stack(ktil_ref, bi, lanes_of(pr))], axis=0)
          for bi, pr in chains]
    amat = [_mm_nt(x_all[i], bk[i]) for i in range(nch)]
    xs = [_mm_nt(x_all[i], s0[i]) for i in range(nch)]

    u = [xs[i][0:2 * L] + _mm(jnp.where(low_strict, amat[i][0:2 * L, 2 * L:4 * L], 0.0), vs[i])
         for i in range(nch)]
    pw = [jnp.where(low_strict, a[0:2 * L, 0:2 * L], 0.0).astype(BF16) for a in amat]
    n_sq = int(math.log2(L))
    for step in range(n_sq):
        u = [u[i] + _mm(pw[i], u[i]) for i in range(nch)]
        if step + 1 < n_sq:
            pw = [_mm(q, q).astype(BF16) for q in pw]

    uv = [jnp.concatenate([u[i].astype(BF16), vs[i]], axis=0) for i in range(nch)]
    gam_rows = [gam_ref[bi, pl.ds(c % 8, 1), :] for bi in range(nb)]
    for i, (bi, pr) in enumerate(chains):
        gam = gam_rows[bi][:, lanes_of(pr)]
        bk_hat = bk[i].astype(F32) * gam
        s_ref[bi * n_pair + pr] = s0[i] * gam + _mm_tn(uv[i], bk_hat)
    def head_mean(z):
        lo = jnp.sum(jnp.where(m0, z, 0.0), axis=-1, keepdims=True)
        hi = jnp.sum(jnp.where(m0, 0.0, z), axis=-1, keepdims=True)
        return jnp.where(m0, lo, hi) * (1.0 / HEAD_DIM)

    for i, (bi, pr) in enumerate(chains):
        a_r =jnp.concatenate([jnp.where(low_incl, amat[i][2 * L:4 * L, 0:2 * L], 0.0),
                               jnp.where(low_incl, amat[i][2 * L:4 * L, 2 * L:4 * L], 0.0)], axis=1)
        y2 = xs[i][2 * L:4 * L] + _mm(a_r, uv[i])
        y = y2[0:L] + y2[L:2 * L]
        d = y - head_mean(y)
        y_ref[bi, :, lanes_of(pr)] = (d * lax.rsqrt(head_mean(d * d) + RW_GN_EPS)).astype(BF16)


RW_SCAN_BATCH = 16


def _rwkv_scan(abar, rbar, btil, ktil, v, gam):
    b, t, w = abar.shape
    L = RW_CHUNK
    nb = RW_SCAN_BATCH
    assert b % nb == 0
    tile = lambda: pl.BlockSpec((nb, L, w), lambda bi, c: (bi, c, 0))
    return pl.pallas_call(
        _rwkv_scan_kernel,
        grid=(b // nb, t // L),
        in_specs=[tile(), tile(), tile(), tile(), tile(),
                  pl.BlockSpec((nb, 8, w), lambda bi, c: (bi, c // 8, 0))],
        out_specs=tile(),
        out_shape=jax.ShapeDtypeStruct((b, t, w), BF16),
        scratch_shapes=[pltpu.VMEM((nb * (RW_HEADS // 2), LANES, LANES), F32)],
        compiler_params=_cparams("parallel", "arbitrary"),
        name="rwkv_scan",
    )(abar, rbar, btil, ktil, v, gam)


def _gelu_tanh(x):
    return 0.5 * x * (1.0 + jnp.tanh(math.sqrt(2.0 / math.pi) * (x + 0.044715 * x * x * x)))


def _nsa_cmp_kernel(kc_ref, vc_ref, pek_ref, pev_ref, k1_ref, k1f_ref, k2_ref, v1_ref, v1f_ref, v2_ref,
                    ko_ref, vo_ref):
    nh = ko_ref.shape[1]
    hid = CMP_HIDDEN
    for x_ref, pe_ref, w1_ref, w1f_ref, w2_ref, o_ref in (
            (kc_ref, pek_ref, k1_ref, k1f_ref, k2_ref, ko_ref),
            (vc_ref, pev_ref, v1_ref, v1f_ref, v2_ref, vo_ref)):
        z = jnp.zeros((nh, 2 * NSA_KV_GROUPS * hid), F32)
        for tok in range(CMP_STRIDE):
            z = z + _mm(x_ref[0, pl.ds(tok, nh, stride=CMP_STRIDE), :], w1_ref[tok])
        bias = _mm(jnp.broadcast_to(pe_ref[...], (8, pe_ref.shape[1])), w1f_ref[...])[0:1]
        for g in range(NSA_KV_GROUPS):
            zg = z[:, 2 * g * hid:2 * (g + 1) * hid]
            pre = zg[:, 0:hid] + pltpu.roll(zg[:, hid:2 * hid], nh - 1, 0) + bias
            o_ref[g] = _mm(_gelu_tanh(pre), w2_ref[...])


def _nsa_cmp(kc, vc, pe_k, pe_v, k1, k1f, k2d, v1, v1f, v2d):
    b, t, _ = kc.shape
    nh = t // CMP_STRIDE
    tile_in = lambda: pl.BlockSpec((1, t, LANES), lambda i: (i, 0, 0))
    tile_out = lambda: pl.BlockSpec((NSA_KV_GROUPS, nh, LANES), lambda i: (i, 0, 0))
    out = jax.ShapeDtypeStruct((b * NSA_KV_GROUPS, nh, LANES), F32)
    consts = (pe_k, pe_v, k1, k1f, k2d, v1, v1f, v2d)
    return pl.pallas_call(
        _nsa_cmp_kernel,
        grid=(b,),
        in_specs=[tile_in(), tile_in()] + [_const_spec(c.shape) for c in consts],
        out_specs=[tile_out(), tile_out()],
        out_shape=[out, out],
        compiler_params=_cparams("parallel"),
        name="nsa_cmp",
    )(kc, vc, pe_k, pe_v, k1, k1f, k2d, v1, v1f, v2d)


NSA_TQ = 256
NSA_TK = 256
NSA_LOOKAHEAD = 1
NSA_UNROLL = 3


def _rope_swap(x):
    lane = lax.broadcasted_iota(jnp.int32, (1, LANES), 1)
    first = (lane % HEAD_DIM) < (HEAD_DIM // 2)
    return jnp.where(first, pltpu.roll(x, LANES - HEAD_DIM // 2, 1), pltpu.roll(x, HEAD_DIM // 2, 1))


def _masked_softmax_parts(s, mask):
    s = jnp.where(mask, s, NEG_BIG)
    e = jnp.where(mask, jnp.exp(s - jnp.max(s, axis=-1, keepdims=True)), 0.0)
    return e / jnp.maximum(jnp.sum(e, axis=-1, keepdims=True), 1e-30)


NSA_ACC_ROWS = HEAD_DIM + 16


def _nsa_attn_kernel(q_ref, gn_ref, ks_ref, vs_ref, kw_ref, vw_ref, kc_ref, vc_ref,
                     cosq_ref, sinq_ref, cosk_ref, sink_ref, ovt_ref, ext_ref,
                     o_ref,
                     k2_sc, vt_sc, qrot_sc, sel_sc, m_sc, acc_sc):
    i = pl.program_id(1)
    tq, tk = NSA_TQ, NSA_TK
    t = ks_ref.shape[1]
    n_kt = t // tk
    n_sel = t // SEL_BLOCK
    lane = lax.broadcasted_iota(jnp.int32, (1, LANES), 1)
    lo_half = lane < HEAD_DIM

    @pl.when(i == 0)
    def _():
        ones = jnp.ones((NSA_ACC_ROWS - HEAD_DIM, tk), BF16)
        for jt in range(n_kt):
            rs = slice(jt * tk, (jt + 1) * tk)
            cos, sin = cosk_ref[rs, :], sink_ref[rs, :]
            for br, k_src, v_src in ((0, ks_ref, vs_ref), (1, kw_ref, vw_ref)):
                x = k_src[0, rs, :].astype(F32)
                x = x * cos + _rope_swap(x) * sin
                xr = pltpu.roll(x, HEAD_DIM, 1)
                k2_sc[br, 0, rs, :] = jnp.where(lo_half, x, xr).astype(BF16)
                k2_sc[br, 1, rs, :] = jnp.where(lo_half, xr, x).astype(BF16)
                v_t = v_src[0, rs, :].astype(F32).T.astype(BF16)
                for g in range(NSA_KV_GROUPS):
                    vt_sc[br, g, jt, 0:HEAD_DIM, :] = v_t[g * HEAD_DIM:(g + 1) * HEAD_DIM]
                    vt_sc[br, g, jt, HEAD_DIM:NSA_ACC_ROWS, :] = ones

    tq_row = i * tq + lax.broadcasted_iota(jnp.int32, (1, tq), 1)
    tq4_row = jnp.concatenate([tq_row] * 4, axis=1)
    gates_t = _sigmoid(gn_ref[0]).T
    cosq, sinq = cosq_ref[...], sinq_ref[...]
    qscale = HEAD_DIM ** -0.5 * LOG2E
    nt_dims = (((1,), (1,)), ((), ()))
    cols = lambda hh: slice(hh * tq, (hh + 1) * tq)

    def stack_heads(xa, xb):
        parts = [jnp.where(lo_half, xa, 0.0), jnp.where(lo_half, 0.0, xa),
                 jnp.where(lo_half, xb, 0.0), jnp.where(lo_half, 0.0, xb)]
        return (jnp.concatenate(parts, axis=0) * qscale).astype(BF16)

    j_last = (i * tq + tq - 1) // tk
    j_first_win = jnp.maximum(i * tq - (WINDOW - 1), 0) // tk
    blk = lax.broadcasted_iota(jnp.int32, (n_sel, 1), 0)
    cur = tq_row // SEL_BLOCK
    forced = (blk == 0) | (blk == cur) | (blk == cur - 1)
    allowed = blk <= cur
    n_valid = (t - CMP_BLOCK) // CMP_STRIDE + 1
    cblk = lax.broadcasted_iota(jnp.int32, (LANES, 1), 0)
    cmask = (cblk * CMP_STRIDE + (CMP_BLOCK - 1) <= tq4_row) & (cblk < n_valid)

    o_cmp = []
    for g in range(NSA_KV_GROUPS):
        qa = q_ref[0, :, (2 * g) * LANES:(2 * g + 1) * LANES].astype(F32)
        qb = q_ref[0, :, (2 * g + 1) * LANES:(2 * g + 2) * LANES].astype(F32)
        q_cmp = stack_heads(qa, qb)
        qrot_sc[g] = stack_heads(qa * cosq + _rope_swap(qa) * sinq, qb * cosq + _rope_swap(qb) * sinq)

        s_c = lax.dot_general(kc_ref[g].astype(BF16), q_cmp, nt_dims, preferred_element_type=F32)
        s_c = jnp.where(cmask, s_c, NEG_BIG)
        e_c = jnp.where(cmask, jnp.exp2(s_c - jnp.max(s_c, axis=0, keepdims=True)), 0.0)
        p_c = e_c / jnp.maximum(jnp.sum(e_c, axis=0, keepdims=True), 1e-30)
        o_cmp.append(_mm(vc_ref[g].T[0:HEAD_DIM], p_c))

        p_sum = p_c[:, cols(0)] + p_c[:, cols(1)] + p_c[:, cols(2)] + p_c[:, cols(3)]
        imp = sum(jnp.dot(ovt_ref[...], part, preferred_element_type=F32) for part in _split3(p_sum))
        imp = jnp.where(forced, 1e4, jnp.where(allowed, imp, -1.0))
        rank = jnp.zeros((n_sel, tq), F32)
        for b in range(n_sel):
            row_b = imp[b:b + 1, :]
            rank = rank + jnp.where((row_b > imp) | ((row_b == imp) & (b < blk)), 1.0, 0.0)
        sel_sc[g] = jnp.where(rank < min(SEL_TOPK, n_sel), 1.0, 0.0).astype(BF16)

    for ch in range(2 * NSA_KV_GROUPS):
        m_sc[ch] = jnp.full(m_sc.shape[1:], NEG_BIG, F32)
        acc_sc[ch] = jnp.zeros(acc_sc.shape[1:], F32)

    def run(items):
        def scores(j, pos0, br, g):
            r0 = pl.multiple_of(j * tk, tk)
            kpos = pos0 + lax.broadcasted_iota(jnp.int32, (tk, 1), 0)
            s = lax.dot_general(k2_sc[br, g, pl.ds(r0, tk), :], qrot_sc[g], nt_dims,
                                preferred_element_type=F32)
            if br == 0:
                hit = jnp.dot(ext_ref[pl.ds(r0, tk), :], sel_sc[g], preferred_element_type=F32)
                keep = (kpos <= tq_row) & (hit > 0.5)
            else:
                keep = (kpos <= tq_row) & (kpos > tq_row - WINDOW)
            return s + jnp.concatenate([jnp.where(keep, 0.0, NEG_BIG)] * 4, axis=1)

        pending = [scores(*it) for it in items[:NSA_LOOKAHEAD]]
        for n, (j, _, br, g) in enumerate(items):
            if n + NSA_LOOKAHEAD < len(items):
                pending.append(scores(*items[n + NSA_LOOKAHEAD]))
            s_cur = pending.pop(0)
            ch = br * NSA_KV_GROUPS + g
            m_old = m_sc[ch]
            m_new = jnp.maximum(m_old, jnp.max(s_cur, axis=0, keepdims=True))
            p = jnp.exp2(s_cur - m_new).astype(BF16)
            m_sc[ch] = m_new
            acc_sc[ch] = jnp.exp2(m_old - m_new) * acc_sc[ch] + jnp.dot(
                vt_sc[br, g, j], p, preferred_element_type=F32)

    def key_tiles(lo, hi, branches):
        done = lo
        for width in range(NSA_UNROLL, 0, -1):
            n_blocks = (hi - done) // width

            def body(blk, carry, width=width, start=done):
                j0 = start + width * blk
                run([(j, j * tk, br, g) for j in [j0 + d for d in range(width)]
                     for g in range(NSA_KV_GROUPS) for br in branches])
                return carry

            lax.fori_loop(0, n_blocks, body, 0)
            done = done + width * n_blocks

    key_tiles(0, j_first_win, (0,))
    key_tiles(j_first_win, j_last + 1, (0, 1))

    def finish(ch):
        acc = acc_sc[ch]
        out = acc[0:HEAD_DIM] / jnp.maximum(acc[HEAD_DIM:HEAD_DIM + 1], 1e-30)
        return jnp.where(m_sc[ch] > 0.5 * NEG_BIG, out, 0.0)

    for g in range(NSA_KV_GROUPS):
        o_sel, o_win = finish(g), finish(NSA_KV_GROUPS + g)

        def head_out(hh):
            r = (4 * g + hh) * 3
            return (gates_t[r:r + 1] * o_cmp[g][:, cols(hh)] + gates_t[r + 1:r + 2] * o_sel[:, cols(hh)]
                    + gates_t[r + 2:r + 3] * o_win[:, cols(hh)])

        for pr in range(2):
            o_ref[0, :, (2 * g + pr) * LANES:(2 * g + pr + 1) * LANES] = jnp.concatenate(
                [head_out(2 * pr), head_out(2 * pr + 1)], axis=0).T.astype(BF16)


def _nsa_attn(q, gn, kv, kc2, vc2, cos, sin, ovt, ex):
    b, t, _ = q.shape
    tq, tk = NSA_TQ, NSA_TK
    g = NSA_KV_GROUPS
    assert t % tq == 0 and t % tk == 0
    full = lambda col: pl.BlockSpec((1, t, LANES), lambda bi, i: (bi, 0, col))
    return pl.pallas_call(
        _nsa_attn_kernel,
        grid=(b, t // tq),
        in_specs=[pl.BlockSpec((1, tq, NSA_WIDTH), lambda bi, i: (bi, i, 0)),
                  pl.BlockSpec((1, tq, LANES), lambda bi, i: (bi, i, 0)),
                  full(0), full(1), full(2), full(3),
                  pl.BlockSpec((g, kc2.shape[1], LANES), lambda bi, i: (bi, 0, 0)),
                  pl.BlockSpec((g, vc2.shape[1], LANES), lambda bi, i: (bi, 0, 0)),
                  pl.BlockSpec((tq, LANES), lambda bi, i: (i, 0)),
                  pl.BlockSpec((tq, LANES), lambda bi, i: (i, 0)),
                  _const_spec(cos.shape), _const_spec(sin.shape),
                  _const_spec(ovt.shape), _const_spec(ex.shape)],
        out_specs=pl.BlockSpec((1, tq, NSA_WIDTH), lambda bi, i: (bi, i, 0)),
        out_shape=jax.ShapeDtypeStruct((b, t, NSA_WIDTH), BF16),
        scratch_shapes=[pltpu.VMEM((2, g, t, LANES), BF16),
                        pltpu.VMEM((2, g, t // tk, NSA_ACC_ROWS, tk), BF16),
                        pltpu.VMEM((g, 4 * tq, LANES), BF16),
                        pltpu.VMEM((g, t // SEL_BLOCK, tq), BF16),
                        pltpu.VMEM((2 * g, 1, 4 * tq), F32),
                        pltpu.VMEM((2 * g, NSA_ACC_ROWS, 4 * tq), F32)],
        compiler_params=_cparams("parallel", "arbitrary"),
        name="nsa_attn",
    )(q, gn, kv, kv, kv, kv, kc2, vc2, cos, sin, cos, sin, ovt, ex)


def _mix_xattn_kernel(x_ref, yn_ref, bonus_ref, g_ref, lng_ref, lnb_ref, on_ref, gm_ref,
                      wrw_ref, wnsa_ref, wout_ref,
                      gx_ref, wq_ref, mem_ref, gmem_ref, wkv_ref, wo_ref,
                      o_ref, kv_sc):
    d = x_ref.shape[2]
    hd = d // XA_HEADS

    @pl.when(pl.program_id(1) == 0)
    def _():
        kv_sc[...] = _mm(_rms(mem_ref[0], gmem_ref[...]), wkv_ref[...]).astype(BF16)

    y_rw = ((yn_ref[0].astype(F32) * lng_ref[...] + lnb_ref[...] + bonus_ref[0].astype(F32))
            * g_ref[0].astype(F32))
    t_rw = _mm(y_rw, wrw_ref[...])
    t_ns = _mm(on_ref[0], wnsa_ref[...])
    gm = gm_ref[0].astype(F32)
    mix = _sigmoid(gm[:, 0:d]) * t_rw + _sigmoid(gm[:, d:2 * d]) * t_ns
    h = x_ref[0] + _mm(mix, wout_ref[...])

    q = (_mm(_rms(h, gx_ref[...]), wq_ref[...]) * hd ** -0.5).astype(BF16)

    def scores(hh):
        cs = slice(hh * hd, (hh + 1) * hd)
        return lax.dot_general(q[:, cs], kv_sc[:, cs], (((1,), (1,)), ((), ())), preferred_element_type=F32)

    outs = []
    s_next = scores(0)
    for hh in range(XA_HEADS):
        s, s_next = s_next, (scores(hh + 1) if hh + 1 < XA_HEADS else None)
        e = jnp.exp(s - jnp.max(s, axis=-1, keepdims=True))
        p = e * (1.0 / jnp.sum(e, axis=-1, keepdims=True))
        outs.append(jnp.dot(p.astype(BF16), kv_sc[:, d + hh * hd:d + (hh + 1) * hd],
                            preferred_element_type=F32))
    o_ref[0] = h + _mm(jnp.concatenate(outs, axis=1), wo_ref[...])


def _mix_xattn(x3, yn, bonus, g, ln_g, ln_b, o_nsa, gm, w_rw, w_nsa, w_out, g_xa, wq, mem, g_mem, wkv, wo, tm):
    b, t, d = x3.shape
    m = mem.shape[1]
    rows = lambda width: pl.BlockSpec((1, tm, width), lambda bi, i: (bi, i, 0))
    consts = lambda *arrs: [_const_spec(a.shape) for a in arrs]
    return pl.pallas_call(
        _mix_xattn_kernel,
        grid=(b, t // tm),
        in_specs=[rows(d), rows(RW_WIDTH), rows(RW_WIDTH), rows(RW_WIDTH)] + consts(ln_g, ln_b)
        + [rows(NSA_WIDTH), rows(2 * d)] + consts(w_rw, w_nsa, w_out, g_xa, wq)
        + [pl.BlockSpec((1, m, d), lambda bi, i: (bi, 0, 0))] + consts(g_mem, wkv, wo),
        out_specs=rows(d),
        out_shape=jax.ShapeDtypeStruct((b, t, d), F32),
        scratch_shapes=[pltpu.VMEM((m, 2 * d), BF16)],
        compiler_params=_cparams("parallel", "arbitrary"),
        name="mix_xattn",
    )(x3, yn, bonus, g, ln_g, ln_b, o_nsa, gm, w_rw, w_nsa, w_out, g_xa, wq, mem, g_mem, wkv, wo)


FFN_CHUNK = 256


def _ffn_kernel(h_ref, g_ref, wgu_ref, wd_ref, gf_ref, o_ref):
    h = h_ref[...]
    hn = _rms(h, g_ref[...]).astype(BF16)
    dff = wd_ref.shape[0]
    acc = jnp.zeros(h.shape, F32)
    for c0 in range(0, dff, FFN_CHUNK):
        gate = jnp.dot(hn, wgu_ref[:, c0:c0 + FFN_CHUNK], preferred_element_type=F32)
        up = jnp.dot(hn, wgu_ref[:, dff + c0:dff + c0 + FFN_CHUNK], preferred_element_type=F32)
        act = (gate * _sigmoid(gate) * up).astype(BF16)
        acc = acc + jnp.dot(act, wd_ref[c0:c0 + FFN_CHUNK, :], preferred_element_type=F32)
    o_ref[...] = _rms(h + acc, gf_ref[...])


def _ffn(h2, g, wgu, wd, gf, tm):
    n, d = h2.shape
    assert wd.shape[0] % FFN_CHUNK == 0
    return pl.pallas_call(
        _ffn_kernel,
        grid=(n // tm,),
        in_specs=[pl.BlockSpec((tm, d), lambda i: (i, 0)), _const_spec(g.shape),
                  _const_spec(wgu.shape), _const_spec(wd.shape), _const_spec(gf.shape)],
        out_specs=pl.BlockSpec((tm, d), lambda i: (i, 0)),
        out_shape=jax.ShapeDtypeStruct((n, d), F32),
        compiler_params=_cparams("parallel"),
        name="ffn",
    )(h2, g, wgu, wd, gf)


def _rope_tables(t):
    half = HEAD_DIM // 2
    inv_freq = ROPE_THETA ** (-np.arange(half, dtype=np.float64) / half)
    ang = np.arange(t, dtype=np.float64)[:, None] * inv_freq[None, :]
    cos, sin = np.cos(ang), np.sin(ang)
    cos128 = np.tile(cos, (1, LANES // half))
    sin128 = np.tile(np.concatenate([-sin, sin], axis=1), (1, LANES // HEAD_DIM))
    return jnp.asarray(cos128, F32), jnp.asarray(sin128, F32)


def _overlap_matrix(t):
    n_cmp = (t - CMP_BLOCK) // CMP_STRIDE + 1
    n_sel = t // SEL_BLOCK
    cs = np.arange(n_cmp) * CMP_STRIDE
    ss = np.arange(n_sel) * SEL_BLOCK
    ov = np.clip(np.minimum(cs[:, None] + CMP_BLOCK, ss[None, :] + SEL_BLOCK)
                 - np.maximum(cs[:, None], ss[None, :]), 0, None) / CMP_BLOCK
    out = np.zeros((n_sel, LANES), np.float32)
    out[:, :n_cmp] = ov.T
    return jnp.asarray(out, BF16)


def _expand_matrix(t):
    blk = np.arange(t) // SEL_BLOCK
    return jnp.asarray(blk[:, None] == np.arange(t // SEL_BLOCK)[None, :], BF16)


def kernel(x, mem, norm_mix_g, w_in, shift_mu, rw_w_up, rw_w0, rw_a_up, rw_a0, rw_g_up, rw_k_k, rw_k_a,
           rw_r_k, rw_ln_g, rw_ln_b, nsa_pe_k, nsa_pe_v, nsa_ck1, nsa_ck2, nsa_cv1, nsa_cv2, w_up_rw,
           w_up_nsa, w_out, norm_xa_g, norm_mem_g, xa_wq, xa_wkv, xa_wo, norm_ffn_g, ffn_w_gu,
           ffn_w_down, final_norm_g):
    b, t, d = x.shape
    n = b * t
    h = x.reshape(n, d)
    cos, sin = _rope_tables(t)
    ov = _overlap_matrix(t)
    ex = _expand_matrix(t)
    row = lambda a: a.reshape(1, -1)
    n_half = t // CMP_STRIDE
    assert t // SEL_BLOCK <= LANES and n_half <= LANES

    for l in range(w_in.shape[0]):
        c_g = RW_IN + NSA_WIDTH + 6 * KV_WIDTH
        n_gate = 3 * NSA_HEADS
        w_gate = jnp.pad(w_in[l][:, c_g:c_g + n_gate], ((0, 0), (0, LANES - n_gate))).astype(BF16)
        z64 = jnp.zeros((DECAY_LORA, RW_WIDTH), F32)
        wlora = jnp.concatenate([jnp.concatenate([rw_w_up[l], z64], axis=1),
                                 jnp.concatenate([z64, rw_a_up[l]], axis=1)], axis=0).astype(BF16)
        prep_consts = (row(shift_mu[l]), wlora, row(rw_w0[l]), row(rw_a0[l]), rw_g_up[l].astype(BF16),
                       row(rw_k_k[l]), row(rw_k_a[l]), row(rw_r_k[l]))
        (q, kc, vc, kv, gn, gm, abar, rbar, btil, ktil, v_rw, gam, g_rw, bonus) = _in_proj(
            h, row(norm_mix_g[l]), w_in[l][:, :c_g].astype(BF16), w_gate,
            w_in[l][:, c_g + n_gate:].astype(BF16), prep_consts, ROW_TILE, t)

        seq = lambda a: a.reshape(b, t, RW_WIDTH)
        abar, rbar, btil, ktil, v_rw = seq(abar), seq(rbar), seq(btil), seq(ktil), seq(v_rw)
        gam = gam.reshape(b, t // RW_CHUNK, RW_WIDTH)
        yn = _rwkv_scan(abar, rbar, btil, ktil, v_rw, gam)

        def w1_tokens(w1):
            hw = CMP_STRIDE * HEAD_DIM
            per_tok = jnp.concatenate([w1[:hw].reshape(CMP_STRIDE, HEAD_DIM, -1),
                                       w1[hw:].reshape(CMP_STRIDE, HEAD_DIM, -1)], axis=2)
            zero = jnp.zeros_like(per_tok)
            return jnp.concatenate([jnp.concatenate([per_tok, zero], axis=2),
                                    jnp.concatenate([zero, per_tok], axis=2)], axis=1).astype(BF16)

        dup = lambda w2: jnp.concatenate([w2, w2], axis=1).astype(BF16)
        kc2, vc2 = _nsa_cmp(kc.reshape(b, t, KV_WIDTH), vc.reshape(b, t, KV_WIDTH),
                            nsa_pe_k[l].reshape(1, -1), nsa_pe_v[l].reshape(1, -1),
                            w1_tokens(nsa_ck1[l]), nsa_ck1[l].astype(BF16), dup(nsa_ck2[l]),
                            w1_tokens(nsa_cv1[l]), nsa_cv1[l].astype(BF16), dup(nsa_cv2[l]))
        o_nsa = _nsa_attn(q.reshape(b, t, NSA_WIDTH), gn.reshape(b, t, LANES),
                          kv.reshape(b, t, 4 * KV_WIDTH), kc2, vc2, cos, sin, ov, ex)

        seq_of = lambda a, width: a.reshape(b, t, width)
        h = _mix_xattn(seq_of(h, d), yn, seq_of(bonus, RW_WIDTH), seq_of(g_rw, RW_WIDTH),
                       row(rw_ln_g[l]), row(rw_ln_b[l]), o_nsa, seq_of(gm, 2 * d),
                       w_up_rw[l].astype(BF16), w_up_nsa[l].astype(BF16), w_out[l].astype(BF16),
                       row(norm_xa_g[l]), xa_wq[l].astype(BF16), mem, row(norm_mem_g[l]),
                       xa_wkv[l].astype(BF16), xa_wo[l].astype(BF16), ROW_TILE).reshape(n, d)

        last = l == w_in.shape[0] - 1
        gf = row(final_norm_g) if last else None
        assert last, "only a single layer is fused with the final norm"
        h = _ffn(h, row(norm_ffn_g[l]), ffn_w_gu[l].astype(BF16), ffn_w_down[l].astype(BF16), gf, ROW_TILE)
    return h.reshape(b, t, d)
```

```python
import functools
import math

import numpy as np
import jax
import jax.numpy as jnp
from jax import lax
from jax.experimental import pallas as pl
from jax.experimental.pallas import tpu as pltpu

F32 = jnp.float32
BF16 = jnp.bfloat16

HEAD_DIM = 64
NORM_EPS = 1e-6
ROPE_THETA = 10000.0
RW_HEADS = 8
RW_WIDTH = RW_HEADS * HEAD_DIM
DECAY_LORA = 64
AAA_LORA = 64
GATE_LORA = 128
RW_GN_EPS = 64e-5
RW_IN = 3 * RW_WIDTH + DECAY_LORA + AAA_LORA + GATE_LORA
NSA_HEADS = 8
NSA_KV_GROUPS = 2
NSA_WIDTH = NSA_HEADS * HEAD_DIM
KV_WIDTH = NSA_KV_GROUPS * HEAD_DIM
CMP_BLOCK = 32
CMP_STRIDE = 16
CMP_HIDDEN = 128
SEL_BLOCK = 64
SEL_TOPK = 8
WINDOW = 512
XA_HEADS = 4

LANES = 128
VMEM_LIMIT = 56 * 1024 * 1024
RW_CHUNK = 64
ROW_TILE = 512
NEG_BIG = -1e30
LOG2E = 1.4426950408889634


def _cparams(*sem):
    return pltpu.CompilerParams(dimension_semantics=sem, vmem_limit_bytes=VMEM_LIMIT)


def _mm(a, b):
    return jnp.dot(a.astype(BF16), b.astype(BF16), preferred_element_type=F32)


def _mm_nt(a, b):
    return lax.dot_general(a.astype(BF16), b.astype(BF16), (((1,), (1,)), ((), ())),
                           preferred_element_type=F32)


def _mm_tn(a, b):
    return lax.dot_general(a.astype(BF16), b.astype(BF16), (((0,), (0,)), ((), ())),
                           preferred_element_type=F32)


def _split(x, parts):
    out = []
    for _ in range(parts - 1):
        piece = x.astype(BF16)
        out.append(piece)
        x = x - piece.astype(F32)
    return out + [x.astype(BF16)]


def _split3(x):
    return _split(x, 3)


def _head_sums(x):
    lo_half = lax.broadcasted_iota(jnp.int32, (1, LANES), 1) < HEAD_DIM
    out = []
    for c in range(0, x.shape[1], LANES):
        t = x[:, c:c + LANES]
        lo = jnp.sum(jnp.where(lo_half, t, 0.0), axis=-1, keepdims=True)
        hi = jnp.sum(jnp.where(lo_half, 0.0, t), axis=-1, keepdims=True)
        out.append(jnp.where(lo_half, lo, hi))
    return jnp.concatenate(out, axis=1)


def _chunk_prefix_sums(x, chunk):
    pos = lax.broadcasted_iota(jnp.int32, (x.shape[0], 1), 0) % chunk
    shift = 1
    while shift < chunk:
        x = x + jnp.where(pos >= shift, pltpu.roll(x, shift, 0), 0.0)
        shift *= 2
    return x


def _rms(x, g):
    return x * lax.rsqrt(jnp.mean(x * x, axis=-1, keepdims=True) + NORM_EPS) * g


def _sigmoid(x):
    return 1.0 / (1.0 + jnp.exp(-x))


def _const_spec(shape, single_buffer=False):
    nd = len(shape)
    if single_buffer:
        return pl.BlockSpec(shape, lambda *_: (0,) * nd, pipeline_mode=pl.Buffered(1))
    return pl.BlockSpec(shape, lambda *_: (0,) * nd)


def _in_proj_kernel(x_ref, g_ref, w_main_ref, w_gate_ref, w_merge_ref,
                    mu_ref, wlora_ref, w0_ref, a0_ref, gup_ref, kk_ref, ka_ref, rk_ref,
                    q_ref, kc_ref, vc_ref, kv_ref, gn_ref, gm_ref,
                    abar_ref, rbar_ref, btil_ref, ktil_ref, v_ref, gam_ref, gate_ref, bonus_ref,
                    p_sc, last_sc, *, rows_per_seq):
    i = pl.program_id(0)
    tm = x_ref.shape[0]
    w = RW_WIDTH
    hn = _rms(x_ref[...], g_ref[...]).astype(BF16)

    def project(o_ref, w_ref, base, c0, c1):
        o_ref[:, c0:c1] = jnp.dot(hn, w_ref[:, base + c0:base + c1],
                                  preferred_element_type=F32).astype(o_ref.dtype)

    def project_all(o_ref, w_ref, base):
        width = o_ref.shape[1]
        for c0 in range(0, width, 512):
            project(o_ref, w_ref, base, c0, min(c0 + 512, width))

    project_all(p_sc, w_main_ref, 0)
    off = RW_IN
    for o_ref in (q_ref, kc_ref, vc_ref, kv_ref):
        project_all(o_ref, w_main_ref, off)
        off += o_ref.shape[1]
    project_all(gn_ref, w_gate_ref, 0)

    cur = p_sc[...]
    prev_row = jnp.where((i * tm) % rows_per_seq == 0, 0.0, last_sc[7:8, :])
    row = lax.broadcasted_iota(jnp.int32, (tm, 1), 0)
    prev = jnp.where(row == 0, prev_row, pltpu.roll(cur, 1, 0))
    p = cur + mu_ref[...] * (prev - cur)
    last_sc[7:8, :] = cur[tm - 1:tm, :]

    r = p[:, 0:w]
    k = p[:, w:2 * w]
    v = p[:, 2 * w:3 * w]
    x_wa = p[:, 3 * w:3 * w + LANES]
    x_g = p[:, 3 * w + LANES:3 * w + 2 * LANES]
    lane = lax.broadcasted_iota(jnp.int32, (1, LANES), 1)
    x_wa = jnp.where(lane < DECAY_LORA, jnp.tanh(x_wa), x_wa)
    lin = _mm(x_wa, wlora_ref[...])
    w_lin = w0_ref[...] + lin[:, 0:w]
    a = _sigmoid(a0_ref[...] + lin[:, w:2 * w])
    gate_ref[...] = _mm(_sigmoid(x_g), gup_ref[...]).astype(BF16)

    logw = _sigmoid(w_lin) * (-math.exp(-0.5) * LOG2E)

    kk = k * kk_ref[...]
    kk = kk * lax.rsqrt(jnp.maximum(_head_sums(kk * kk), 1e-12))
    k2 = k * (1.0 + (a - 1.0) * ka_ref[...])
    bonus_ref[...] = (_head_sums(r * k2 * rk_ref[...]) * v).astype(BF16)
    v_ref[...] = v.astype(BF16)

    half = gm_ref.shape[1] // 2
    for c0 in range(0, half, 512):
        project(gm_ref, w_merge_ref, 0, c0, c0 + 512)

    L = RW_CHUNK
    cum_all = _chunk_prefix_sums(logw, L)
    kka = kk * a
    gam_rows = []
    for c in range(tm // L):
        sl = slice(c * L, (c + 1) * L)
        cum, gam = cum_all[sl], jnp.exp2(cum_all[(c + 1) * L - 1:(c + 1) * L])
        e_cum = jnp.exp2(cum)
        e_neg = 1.0 / e_cum
        abar_ref[sl, :] = (-kk[sl] * jnp.exp2(cum - logw[sl])).astype(BF16)
        rbar_ref[sl, :] = (r[sl] * e_cum).astype(BF16)
        btil_ref[sl, :] = (kka[sl] * e_neg).astype(BF16)
        ktil_ref[sl, :] = (k2[sl] * e_neg).astype(BF16)
        gam_rows.append(gam)
    gam_ref[...] = jnp.concatenate(gam_rows, axis=0)

    for c0 in range(half, 2 * half, 512):
        project(gm_ref, w_merge_ref, 0, c0, c0 + 512)


def _in_proj(x2, g, w_main, w_gate, w_merge, prep_consts, tm, rows_per_seq):
    n, d = x2.shape
    w = RW_WIDTH
    widths = (NSA_WIDTH, KV_WIDTH, KV_WIDTH, 4 * KV_WIDTH, LANES, 2 * d)
    dtypes = (BF16, F32, F32, BF16, F32, BF16)
    assert RW_IN + sum(widths[:-2]) == w_main.shape[1] and widths[-2:] == (w_gate.shape[1], w_merge.shape[1])
    assert n % tm == 0 and rows_per_seq % tm == 0 and tm % (8 * RW_CHUNK) == 0 and (2 * d) % 1024 == 0
    rows = lambda width: pl.BlockSpec((tm, width), lambda i: (i, 0))
    seq_bf = jax.ShapeDtypeStruct((n, w), BF16)
    return pl.pallas_call(
        functools.partial(_in_proj_kernel, rows_per_seq=rows_per_seq),
        grid=(n // tm,),
        in_specs=[rows(d), _const_spec((1, d)),
                  _const_spec(w_main.shape), _const_spec(w_gate.shape), _const_spec(w_merge.shape)]
        + [_const_spec(c.shape) for c in prep_consts],
        out_specs=[rows(wd) for wd in widths] + [rows(w)] * 5
        + [pl.BlockSpec((tm // RW_CHUNK, w), lambda i: (i, 0)), rows(w), rows(w)],
        out_shape=[jax.ShapeDtypeStruct((n, wd), dt) for wd, dt in zip(widths, dtypes)] + [seq_bf] * 5
        + [jax.ShapeDtypeStruct((n // RW_CHUNK, w), F32), seq_bf, seq_bf],
        scratch_shapes=[pltpu.VMEM((tm, RW_IN), F32), pltpu.VMEM((8, RW_IN), F32)],
        compiler_params=_cparams("arbitrary"),
        name="in_proj",
    )(x2, g, w_main, w_gate, w_merge, *prep_consts)


def _rwkv_scan_kernel(abar_ref, rbar_ref, btil_ref, ktil_ref, v_ref, gam_ref,
                      y_ref, s_ref):
    c = pl.program_id(1)
    L = RW_CHUNK

    @pl.when(c == 0)
    def _():
        s_ref[...] = jnp.zeros_like(s_ref)

    lane = lax.broadcasted_iota(jnp.int32, (1, LANES), 1)
    m0 = lane < HEAD_DIM
    ri = lax.broadcasted_iota(jnp.int32, (2 * L, 2 * L), 0)
    ci = lax.broadcasted_iota(jnp.int32, (2 * L, 2 * L), 1)
    same = (ri // L) == (ci // L)
    low_strict = same & (ci < ri)
    low_incl = same & (ci <= ri)
    zero = jnp.zeros((), BF16)
    nb = abar_ref.shape[0]
    n_pair = RW_HEADS // 2

    def stack(ref, bi, ls):
        x = ref[bi, :, ls]
        return jnp.concatenate([jnp.where(m0, x, zero), jnp.where(m0, zero, x)], axis=0)

    chains = [(bi, pr) for bi in range(nb) for pr in range(n_pair)]
    nch = len(chains)
    lanes_of = lambda pr: slice(pr * LANES, (pr + 1) * LANES)

    x_all = [jnp.concatenate([stack(abar_ref, bi, lanes_of(pr)), stack(rbar_ref, bi, lanes_of(pr))], axis=0)
             for bi, pr in chains]
    vs = [stack(v_ref, bi, lanes_of(pr)) for bi, pr in chains]
    s0 = [s_ref[bi * n_pair + pr] for bi, pr in chains]
    bk = [jnp.concatenate([stack(btil_ref, bi, lanes_of(pr)), stack(ktil_ref, bi, lanes_of(pr))], axis=0)
          for bi, pr in chains]
    amat = [_mm_nt(x_all[i], bk[i]) for i in range(nch)]
    xs = [_mm_nt(x_all[i], s0[i]) for i in range(nch)]

    u = [xs[i][0:2 * L] + _mm(jnp.where(low_strict, amat[i][0:2 * L, 2 * L:4 * L], 0.0), vs[i])
         for i in range(nch)]
    pw = [jnp.where(low_strict, a[0:2 * L, 0:2 * L], 0.0).astype(BF16) for a in amat]
    n_sq = int(math.log2(L))
    for step in range(n_sq):
        u = [u[i] + _mm(pw[i], u[i]) for i in range(nch)]
        if step + 1 < n_sq:
            pw = [_mm(q, q).astype(BF16) for q in pw]

    uv = [jnp.concatenate([u[i].astype(BF16), vs[i]], axis=0) for i in range(nch)]
    gam_rows = [gam_ref[bi, pl.ds(c % 8, 1), :] for bi in range(nb)]
    for i, (bi, pr) in enumerate(chains):
        gam = gam_rows[bi][:, lanes_of(pr)]
        bk_hat = bk[i].astype(F32) * gam
        s_ref[bi * n_pair + pr] = s0[i] * gam + _mm_tn(uv[i], bk_hat)
    def head_mean(z):
        lo = jnp.sum(jnp.where(m0, z, 0.0), axis=-1, keepdims=True)
        hi = jnp.sum(jnp.where(m0, 0.0, z), axis=-1, keepdims=True)
        return jnp.where(m0, lo, hi) * (1.0 / HEAD_DIM)

    for i, (bi, pr) in enumerate(chains):
        a_r =jnp.concatenate([jnp.where(low_incl, amat[i][2 * L:4 * L, 0:2 * L], 0.0),
                               jnp.where(low_incl, amat[i][2 * L:4 * L, 2 * L:4 * L], 0.0)], axis=1)
        y2 = xs[i][2 * L:4 * L] + _mm(a_r, uv[i])
        y = y2[0:L] + y2[L:2 * L]
        d = y - head_mean(y)
        y_ref[bi, :, lanes_of(pr)] = (d * lax.rsqrt(head_mean(d * d) + RW_GN_EPS)).astype(BF16)


RW_SCAN_BATCH = 16


def _rwkv_scan(abar, rbar, btil, ktil, v, gam):
    b, t, w = abar.shape
    L = RW_CHUNK
    nb = RW_SCAN_BATCH
    assert b % nb == 0
    tile = lambda: pl.BlockSpec((nb, L, w), lambda bi, c: (bi, c, 0))
    return pl.pallas_call(
        _rwkv_scan_kernel,
        grid=(b // nb, t // L),
        in_specs=[tile(), tile(), tile(), tile(), tile(),
                  pl.BlockSpec((nb, 8, w), lambda bi, c: (bi, c // 8, 0))],
        out_specs=tile(),
        out_shape=jax.ShapeDtypeStruct((b, t, w), BF16),
        scratch_shapes=[pltpu.VMEM((nb * (RW_HEADS // 2), LANES, LANES), F32)],
        compiler_params=_cparams("parallel", "arbitrary"),
        name="rwkv_scan",
    )(abar, rbar, btil, ktil, v, gam)


def _gelu_tanh(x):
    return 0.5 * x * (1.0 + jnp.tanh(math.sqrt(2.0 / math.pi) * (x + 0.044715 * x * x * x)))


def _nsa_cmp_kernel(kc_ref, vc_ref, pek_ref, pev_ref, k1_ref, k1f_ref, k2_ref, v1_ref, v1f_ref, v2_ref,
                    ko_ref, vo_ref):
    nh = ko_ref.shape[1]
    hid = CMP_HIDDEN
    for x_ref, pe_ref, w1_ref, w1f_ref, w2_ref, o_ref in (
            (kc_ref, pek_ref, k1_ref, k1f_ref, k2_ref, ko_ref),
            (vc_ref, pev_ref, v1_ref, v1f_ref, v2_ref, vo_ref)):
        z = jnp.zeros((nh, 2 * NSA_KV_GROUPS * hid), F32)
        for tok in range(CMP_STRIDE):
            z = z + _mm(x_ref[0, pl.ds(tok, nh, stride=CMP_STRIDE), :], w1_ref[tok])
        bias = _mm(jnp.broadcast_to(pe_ref[...], (8, pe_ref.shape[1])), w1f_ref[...])[0:1]
        for g in range(NSA_KV_GROUPS):
            zg = z[:, 2 * g * hid:2 * (g + 1) * hid]
            pre = zg[:, 0:hid] + pltpu.roll(zg[:, hid:2 * hid], nh - 1, 0) + bias
            o_ref[g] = _mm(_gelu_tanh(pre), w2_ref[...])


def _nsa_cmp(kc, vc, pe_k, pe_v, k1, k1f, k2d, v1, v1f, v2d):
    b, t, _ = kc.shape
    nh = t // CMP_STRIDE
    tile_in = lambda: pl.BlockSpec((1, t, LANES), lambda i: (i, 0, 0))
    tile_out = lambda: pl.BlockSpec((NSA_KV_GROUPS, nh, LANES), lambda i: (i, 0, 0))
    out = jax.ShapeDtypeStruct((b * NSA_KV_GROUPS, nh, LANES), F32)
    consts = (pe_k, pe_v, k1, k1f, k2d, v1, v1f, v2d)
    return pl.pallas_call(
        _nsa_cmp_kernel,
        grid=(b,),
        in_specs=[tile_in(), tile_in()] + [_const_spec(c.shape) for c in consts],
        out_specs=[tile_out(), tile_out()],
        out_shape=[out, out],
        compiler_params=_cparams("parallel"),
        name="nsa_cmp",
    )(kc, vc, pe_k, pe_v, k1, k1f, k2d, v1, v1f, v2d)


NSA_TQ = 256
NSA_TK = 256
NSA_LOOKAHEAD = 1
NSA_UNROLL = 3


def _rope_swap(x):
    lane = lax.broadcasted_iota(jnp.int32, (1, LANES), 1)
    first = (lane % HEAD_DIM) < (HEAD_DIM // 2)
    return jnp.where(first, pltpu.roll(x, LANES - HEAD_DIM // 2, 1), pltpu.roll(x, HEAD_DIM // 2, 1))


def _masked_softmax_parts(s, mask):
    s = jnp.where(mask, s, NEG_BIG)
    e = jnp.where(mask, jnp.exp(s - jnp.max(s, axis=-1, keepdims=True)), 0.0)
    return e / jnp.maximum(jnp.sum(e, axis=-1, keepdims=True), 1e-30)


NSA_ACC_ROWS = HEAD_DIM + 16


def _nsa_attn_kernel(q_ref, gn_ref, ks_ref, vs_ref, kw_ref, vw_ref, kc_ref, vc_ref,
                     cosq_ref, sinq_ref, cosk_ref, sink_ref, ovt_ref, ext_ref,
                     o_ref,
                     k2_sc, vt_sc, qrot_sc, sel_sc, m_sc, acc_sc):
    i = pl.program_id(1)
    tq, tk = NSA_TQ, NSA_TK
    t = ks_ref.shape[1]
    n_kt = t // tk
    n_sel = t // SEL_BLOCK
    lane = lax.broadcasted_iota(jnp.int32, (1, LANES), 1)
    lo_half = lane < HEAD_DIM

    @pl.when(i == 0)
    def _():
        ones = jnp.ones((NSA_ACC_ROWS - HEAD_DIM, tk), BF16)
        for jt in range(n_kt):
            rs = slice(jt * tk, (jt + 1) * tk)
            cos, sin = cosk_ref[rs, :], sink_ref[rs, :]
            for br, k_src, v_src in ((0, ks_ref, vs_ref), (1, kw_ref, vw_ref)):
                x = k_src[0, rs, :].astype(F32)
                x = x * cos + _rope_swap(x) * sin
                xr = pltpu.roll(x, HEAD_DIM, 1)
                k2_sc[br, 0, rs, :] = jnp.where(lo_half, x, xr).astype(BF16)
                k2_sc[br, 1, rs, :] = jnp.where(lo_half, xr, x).astype(BF16)
                v_t = v_src[0, rs, :].astype(F32).T.astype(BF16)
                for g in range(NSA_KV_GROUPS):
                    vt_sc[br, g, jt, 0:HEAD_DIM, :] = v_t[g * HEAD_DIM:(g + 1) * HEAD_DIM]
                    vt_sc[br, g, jt, HEAD_DIM:NSA_ACC_ROWS, :] = ones

    tq_row = i * tq + lax.broadcasted_iota(jnp.int32, (1, tq), 1)
    tq4_row = jnp.concatenate([tq_row] * 4, axis=1)
    gates_t = _sigmoid(gn_ref[0]).T
    cosq, sinq = cosq_ref[...], sinq_ref[...]
    qscale = HEAD_DIM ** -0.5 * LOG2E
    nt_dims = (((1,), (1,)), ((), ()))
    cols = lambda hh: slice(hh * tq, (hh + 1) * tq)

    def stack_heads(xa, xb):
        parts = [jnp.where(lo_half, xa, 0.0), jnp.where(lo_half, 0.0, xa),
                 jnp.where(lo_half, xb, 0.0), jnp.where(lo_half, 0.0, xb)]
        return (jnp.concatenate(parts, axis=0) * qscale).astype(BF16)

    j_last = (i * tq + tq - 1) // tk
    j_first_win = jnp.maximum(i * tq - (WINDOW - 1), 0) // tk
    blk = lax.broadcasted_iota(jnp.int32, (n_sel, 1), 0)
    cur = tq_row // SEL_BLOCK
    forced = (blk == 0) | (blk == cur) | (blk == cur - 1)
    allowed = blk <= cur
    n_valid = (t - CMP_BLOCK) // CMP_STRIDE + 1
    cblk = lax.broadcasted_iota(jnp.int32, (LANES, 1), 0)
    cmask = (cblk * CMP_STRIDE + (CMP_BLOCK - 1) <= tq4_row) & (cblk < n_valid)

    o_cmp = []
    for g in range(NSA_KV_GROUPS):
        qa = q_ref[0, :, (2 * g) * LANES:(2 * g + 1) * LANES].astype(F32)
        qb = q_ref[0, :, (2 * g + 1) * LANES:(2 * g + 2) * LANES].astype(F32)
        q_cmp = stack_heads(qa, qb)
        qrot_sc[g] = stack_heads(qa * cosq + _rope_swap(qa) * sinq, qb * cosq + _rope_swap(qb) * sinq)

        s_c = lax.dot_general(kc_ref[g].astype(BF16), q_cmp, nt_dims, preferred_element_type=F32)
        s_c = jnp.where(cmask, s_c, NEG_BIG)
        e_c = jnp.where(cmask, jnp.exp2(s_c - jnp.max(s_c, axis=0, keepdims=True)), 0.0)
        p_c = e_c / jnp.maximum(jnp.sum(e_c, axis=0, keepdims=True), 1e-30)
        o_cmp.append(_mm(vc_ref[g].T[0:HEAD_DIM], p_c))

        p_sum = p_c[:, cols(0)] + p_c[:, cols(1)] + p_c[:, cols(2)] + p_c[:, cols(3)]
        imp = sum(jnp.dot(ovt_ref[...], part, preferred_element_type=F32) for part in _split3(p_sum))
        imp = jnp.where(forced, 1e4, jnp.where(allowed, imp, -1.0))
        rank = jnp.zeros((n_sel, tq), F32)
        for b in range(n_sel):
            row_b = imp[b:b + 1, :]
            rank = rank + jnp.where((row_b > imp) | ((row_b == imp) & (b < blk)), 1.0, 0.0)
        sel_sc[g] = jnp.where(rank < min(SEL_TOPK, n_sel), 1.0, 0.0).astype(BF16)

    for ch in range(2 * NSA_KV_GROUPS):
        m_sc[ch] = jnp.full(m_sc.shape[1:], NEG_BIG, F32)
        acc_sc[ch] = jnp.zeros(acc_sc.shape[1:], F32)

    def run(items):
        def scores(j, pos0, br, g):
            r0 = pl.multiple_of(j * tk, tk)
            kpos = pos0 + lax.broadcasted_iota(jnp.int32, (tk, 1), 0)
            s = lax.dot_general(k2_sc[br, g, pl.ds(r0, tk), :], qrot_sc[g], nt_dims,
                                preferred_element_type=F32)
            if br == 0:
                hit = jnp.dot(ext_ref[pl.ds(r0, tk), :], sel_sc[g], preferred_element_type=F32)
                keep = (kpos <= tq_row) & (hit > 0.5)
            else:
                keep = (kpos <= tq_row) & (kpos > tq_row - WINDOW)
            return s + jnp.concatenate([jnp.where(keep, 0.0, NEG_BIG)] * 4, axis=1)

        pending = [scores(*it) for it in items[:NSA_LOOKAHEAD]]
        for n, (j, _, br, g) in enumerate(items):
            if n + NSA_LOOKAHEAD < len(items):
                pending.append(scores(*items[n + NSA_LOOKAHEAD]))
            s_cur = pending.pop(0)
            ch = br * NSA_KV_GROUPS + g
            m_old = m_sc[ch]
            m_new = jnp.maximum(m_old, jnp.max(s_cur, axis=0, keepdims=True))
            p = jnp.exp2(s_cur - m_new).astype(BF16)
            m_sc[ch] = m_new
            acc_sc[ch] = jnp.exp2(m_old - m_new) * acc_sc[ch] + jnp.dot(
                vt_sc[br, g, j], p, preferred_element_type=F32)

    def key_tiles(lo, hi, branches):
        done = lo
        for width in range(NSA_UNROLL, 0, -1):
            n_blocks = (hi - done) // width

            def body(blk, carry, width=width, start=done):
                j0 = start + width * blk
                run([(j, j * tk, br, g) for j in [j0 + d for d in range(width)]
                     for g in range(NSA_KV_GROUPS) for br in branches])
                return carry

            lax.fori_loop(0, n_blocks, body, 0)
            done = done + width * n_blocks

    key_tiles(0, j_first_win, (0,))
    key_tiles(j_first_win, j_last + 1, (0, 1))

    def finish(ch):
        acc = acc_sc[ch]
        out = acc[0:HEAD_DIM] / jnp.maximum(acc[HEAD_DIM:HEAD_DIM + 1], 1e-30)
        return jnp.where(m_sc[ch] > 0.5 * NEG_BIG, out, 0.0)

    for g in range(NSA_KV_GROUPS):
        o_sel, o_win = finish(g), finish(NSA_KV_GROUPS + g)

        def head_out(hh):
            r = (4 * g + hh) * 3
            return (gates_t[r:r + 1] * o_cmp[g][:, cols(hh)] + gates_t[r + 1:r + 2] * o_sel[:, cols(hh)]
                    + gates_t[r + 2:r + 3] * o_win[:, cols(hh)])

        for pr in range(2):
            o_ref[0, :, (2 * g + pr) * LANES:(2 * g + pr + 1) * LANES] = jnp.concatenate(
                [head_out(2 * pr), head_out(2 * pr + 1)], axis=0).T.astype(BF16)


def _nsa_attn(q, gn, kv, kc2, vc2, cos, sin, ovt, ex):
    b, t, _ = q.shape
    tq, tk = NSA_TQ, NSA_TK
    g = NSA_KV_GROUPS
    assert t % tq == 0 and t % tk == 0
    full = lambda col: pl.BlockSpec((1, t, LANES), lambda bi, i: (bi, 0, col))
    return pl.pallas_call(
        _nsa_attn_kernel,
        grid=(b, t // tq),
        in_specs=[pl.BlockSpec((1, tq, NSA_WIDTH), lambda bi, i: (bi, i, 0)),
                  pl.BlockSpec((1, tq, LANES), lambda bi, i: (bi, i, 0)),
                  full(0), full(1), full(2), full(3),
                  pl.BlockSpec((g, kc2.shape[1], LANES), lambda bi, i: (bi, 0, 0)),
                  pl.BlockSpec((g, vc2.shape[1], LANES), lambda bi, i: (bi, 0, 0)),
                  pl.BlockSpec((tq, LANES), lambda bi, i: (i, 0)),
                  pl.BlockSpec((tq, LANES), lambda bi, i: (i, 0)),
                  _const_spec(cos.shape), _const_spec(sin.shape),
                  _const_spec(ovt.shape), _const_spec(ex.shape)],
        out_specs=pl.BlockSpec((1, tq, NSA_WIDTH), lambda bi, i: (bi, i, 0)),
        out_shape=jax.ShapeDtypeStruct((b, t, NSA_WIDTH), BF16),
        scratch_shapes=[pltpu.VMEM((2, g, t, LANES), BF16),
                        pltpu.VMEM((2, g, t // tk, NSA_ACC_ROWS, tk), BF16),
                        pltpu.VMEM((g, 4 * tq, LANES), BF16),
                        pltpu.VMEM((g, t // SEL_BLOCK, tq), BF16),
                        pltpu.VMEM((2 * g, 1, 4 * tq), F32),
                        pltpu.VMEM((2 * g, NSA_ACC_ROWS, 4 * tq), F32)],
        compiler_params=_cparams("parallel", "arbitrary"),
        name="nsa_attn",
    )(q, gn, kv, kv, kv, kv, kc2, vc2, cos, sin, cos, sin, ovt, ex)


def _mix_xattn_kernel(x_ref, yn_ref, bonus_ref, g_ref, lng_ref, lnb_ref, on_ref, gm_ref,
                      wrw_ref, wnsa_ref, wout_ref,
                      gx_ref, wq_ref, mem_ref, gmem_ref, wkv_ref, wo_ref,
                      gffn_ref, wgu_ref, wd_ref, gfin_ref,
                      o_ref, kv_sc):
    d = x_ref.shape[2]
    hd = d // XA_HEADS

    @pl.when(pl.program_id(1) == 0)
    def _():
        kv_sc[...] = _mm(_rms(mem_ref[0], gmem_ref[...]), wkv_ref[...]).astype(BF16)

    y_rw = ((yn_ref[0].astype(F32) * lng_ref[...] + lnb_ref[...] + bonus_ref[0].astype(F32))
            * g_ref[0].astype(F32))
    t_rw = _mm(y_rw, wrw_ref[...])
    t_ns = _mm(on_ref[0], wnsa_ref[...])
    gm = gm_ref[0].astype(F32)
    mix = _sigmoid(gm[:, 0:d]) * t_rw + _sigmoid(gm[:, d:2 * d]) * t_ns
    h = x_ref[0] + _mm(mix, wout_ref[...])

    q = (_mm(_rms(h, gx_ref[...]), wq_ref[...]) * hd ** -0.5).astype(BF16)

    def scores(hh):
        cs = slice(hh * hd, (hh + 1) * hd)
        return lax.dot_general(q[:, cs], kv_sc[:, cs], (((1,), (1,)), ((), ())), preferred_element_type=F32)

    outs = []
    s_next = scores(0)
    for hh in range(XA_HEADS):
        s, s_next = s_next, (scores(hh + 1) if hh + 1 < XA_HEADS else None)
        e = jnp.exp(s - jnp.max(s, axis=-1, keepdims=True))
        p = e * (1.0 / jnp.sum(e, axis=-1, keepdims=True))
        outs.append(jnp.dot(p.astype(BF16), kv_sc[:, d + hh * hd:d + (hh + 1) * hd],
                            preferred_element_type=F32))
    h = h + _mm(jnp.concatenate(outs, axis=1), wo_ref[...])

    hn = _rms(h, gffn_ref[...]).astype(BF16)
    dff = wd_ref.shape[0]
    acc = jnp.zeros(h.shape, F32)
    for c0 in range(0, dff, FFN_CHUNK):
        gate = jnp.dot(hn, wgu_ref[:, c0:c0 + FFN_CHUNK], preferred_element_type=F32)
        up = jnp.dot(hn, wgu_ref[:, dff + c0:dff + c0 + FFN_CHUNK], preferred_element_type=F32)
        act = (gate * _sigmoid(gate) * up).astype(BF16)
        acc = acc + jnp.dot(act, wd_ref[c0:c0 + FFN_CHUNK, :], preferred_element_type=F32)
    o_ref[0] = _rms(h + acc, gfin_ref[...])


def _mix_xattn(x3, yn, bonus, g, ln_g, ln_b, o_nsa, gm, w_rw, w_nsa, w_out, g_xa, wq, mem, g_mem, wkv, wo,
               g_ffn, wgu, wd, g_fin, tm):
    b, t, d = x3.shape
    m = mem.shape[1]
    rows = lambda width: pl.BlockSpec((1, tm, width), lambda bi, i: (bi, i, 0))
    consts = lambda *arrs: [_const_spec(a.shape, single_buffer=True) for a in arrs]
    assert wd.shape[0] % FFN_CHUNK == 0
    return pl.pallas_call(
        _mix_xattn_kernel,
        grid=(b, t // tm),
        in_specs=[rows(d), rows(RW_WIDTH), rows(RW_WIDTH), rows(RW_WIDTH)] + consts(ln_g, ln_b)
        + [rows(NSA_WIDTH), rows(2 * d)] + consts(w_rw, w_nsa, w_out, g_xa, wq)
        + [pl.BlockSpec((1, m, d), lambda bi, i: (bi, 0, 0))] + consts(g_mem, wkv, wo, g_ffn, wgu, wd, g_fin),
        out_specs=rows(d),
        out_shape=jax.ShapeDtypeStruct((b, t, d), F32),
        scratch_shapes=[pltpu.VMEM((m, 2 * d), BF16)],
        compiler_params=_cparams("parallel", "arbitrary"),
        name="mix_xattn",
    )(x3, yn, bonus, g, ln_g, ln_b, o_nsa, gm, w_rw, w_nsa, w_out, g_xa, wq, mem, g_mem, wkv, wo, g_ffn, wgu, wd, g_fin)


FFN_CHUNK = 256


def _rope_tables(t):
    half = HEAD_DIM // 2
    inv_freq = ROPE_THETA ** (-np.arange(half, dtype=np.float64) / half)
    ang = np.arange(t, dtype=np.float64)[:, None] * inv_freq[None, :]
    cos, sin = np.cos(ang), np.sin(ang)
    cos128 = np.tile(cos, (1, LANES // half))
    sin128 = np.tile(np.concatenate([-sin, sin], axis=1), (1, LANES // HEAD_DIM))
    return jnp.asarray(cos128, F32), jnp.asarray(sin128, F32)


def _overlap_matrix(t):
    n_cmp = (t - CMP_BLOCK) // CMP_STRIDE + 1
    n_sel = t // SEL_BLOCK
    cs = np.arange(n_cmp) * CMP_STRIDE
    ss = np.arange(n_sel) * SEL_BLOCK
    ov = np.clip(np.minimum(cs[:, None] + CMP_BLOCK, ss[None, :] + SEL_BLOCK)
                 - np.maximum(cs[:, None], ss[None, :]), 0, None) / CMP_BLOCK
    out = np.zeros((n_sel, LANES), np.float32)
    out[:, :n_cmp] = ov.T
    return jnp.asarray(out, BF16)


def _expand_matrix(t):
    blk = np.arange(t) // SEL_BLOCK
    return jnp.asarray(blk[:, None] == np.arange(t // SEL_BLOCK)[None, :], BF16)


def kernel(x, mem, norm_mix_g, w_in, shift_mu, rw_w_up, rw_w0, rw_a_up, rw_a0, rw_g_up, rw_k_k, rw_k_a,
           rw_r_k, rw_ln_g, rw_ln_b, nsa_pe_k, nsa_pe_v, nsa_ck1, nsa_ck2, nsa_cv1, nsa_cv2, w_up_rw,
           w_up_nsa, w_out, norm_xa_g, norm_mem_g, xa_wq, xa_wkv, xa_wo, norm_ffn_g, ffn_w_gu,
           ffn_w_down, final_norm_g):
    b, t, d = x.shape
    n = b * t
    h = x.reshape(n, d)
    cos, sin = _rope_tables(t)
    ov = _overlap_matrix(t)
    ex = _expand_matrix(t)
    row = lambda a: a.reshape(1, -1)
    n_half = t // CMP_STRIDE
    assert t // SEL_BLOCK <= LANES and n_half <= LANES

    for l in range(w_in.shape[0]):
        c_g = RW_IN + NSA_WIDTH + 6 * KV_WIDTH
        n_gate = 3 * NSA_HEADS
        w_gate = jnp.pad(w_in[l][:, c_g:c_g + n_gate], ((0, 0), (0, LANES - n_gate))).astype(BF16)
        z64 = jnp.zeros((DECAY_LORA, RW_WIDTH), F32)
        wlora = jnp.concatenate([jnp.concatenate([rw_w_up[l], z64], axis=1),
                                 jnp.concatenate([z64, rw_a_up[l]], axis=1)], axis=0).astype(BF16)
        prep_consts = (row(shift_mu[l]), wlora, row(rw_w0[l]), row(rw_a0[l]), rw_g_up[l].astype(BF16),
                       row(rw_k_k[l]), row(rw_k_a[l]), row(rw_r_k[l]))
        (q, kc, vc, kv, gn, gm, abar, rbar, btil, ktil, v_rw, gam, g_rw, bonus) = _in_proj(
            h, row(norm_mix_g[l]), w_in[l][:, :c_g].astype(BF16), w_gate,
            w_in[l][:, c_g + n_gate:].astype(BF16), prep_consts, ROW_TILE, t)

        seq = lambda a: a.reshape(b, t, RW_WIDTH)
        abar, rbar, btil, ktil, v_rw = seq(abar), seq(rbar), seq(btil), seq(ktil), seq(v_rw)
        gam = gam.reshape(b, t // RW_CHUNK, RW_WIDTH)
        yn = _rwkv_scan(abar, rbar, btil, ktil, v_rw, gam)

        def w1_tokens(w1):
            hw = CMP_STRIDE * HEAD_DIM
            per_tok = jnp.concatenate([w1[:hw].reshape(CMP_STRIDE, HEAD_DIM, -1),
                                       w1[hw:].reshape(CMP_STRIDE, HEAD_DIM, -1)], axis=2)
            zero = jnp.zeros_like(per_tok)
            return jnp.concatenate([jnp.concatenate([per_tok, zero], axis=2),
                                    jnp.concatenate([zero, per_tok], axis=2)], axis=1).astype(BF16)

        dup = lambda w2: jnp.concatenate([w2, w2], axis=1).astype(BF16)
        kc2, vc2 = _nsa_cmp(kc.reshape(b, t, KV_WIDTH), vc.reshape(b, t, KV_WIDTH),
                            nsa_pe_k[l].reshape(1, -1), nsa_pe_v[l].reshape(1, -1),
                            w1_tokens(nsa_ck1[l]), nsa_ck1[l].astype(BF16), dup(nsa_ck2[l]),
                            w1_tokens(nsa_cv1[l]), nsa_cv1[l].astype(BF16), dup(nsa_cv2[l]))
        o_nsa = _nsa_attn(q.reshape(b, t, NSA_WIDTH), gn.reshape(b, t, LANES),
                          kv.reshape(b, t, 4 * KV_WIDTH), kc2, vc2, cos, sin, ov, ex)

        seq_of = lambda a, width: a.reshape(b, t, width)
        h = _mix_xattn(seq_of(h, d), yn, seq_of(bonus, RW_WIDTH), seq_of(g_rw, RW_WIDTH),
                       row(rw_ln_g[l]), row(rw_ln_b[l]), o_nsa, seq_of(gm, 2 * d),
                       w_up_rw[l].astype(BF16), w_up_nsa[l].astype(BF16), w_out[l].astype(BF16),
                       row(norm_xa_g[l]), xa_wq[l].astype(BF16), mem, row(norm_mem_g[l]),
                       xa_wkv[l].astype(BF16), xa_wo[l].astype(BF16),
                       row(norm_ffn_g[l]), ffn_w_gu[l].astype(BF16), ffn_w_down[l].astype(BF16),
                       row(final_norm_g), ROW_TILE).reshape(n, d)
        assert l == w_in.shape[0] - 1, "only a single layer is fused with the final norm"
    return h.reshape(b, t, d)
```
